```python
import math, functools
import jax, jax.numpy as jnp
from jax import lax
import numpy as np

D_MODEL = 1024
BATCH = 8
SEQ = 2048
DEPTH = 2
DEC_BATCH = 128
DEC_SEQ = 8
PAST_LEN = 2048
PAGE_SIZE = 128

D_MIX = D_MODEL
W_BRANCH = D_MIX // 4
A_HEADS = 4
A_QK = W_BRANCH // A_HEADS // 2
A_V = 2 * A_QK
Q_BLOCK = 128
LRU_BLOCKS = 4
LRU_BW = W_BRANCH // LRU_BLOCKS
CONV_W = 4
LRU_C = 8.0
RW_HEADS = 4
RW_HS = W_BRANCH // RW_HEADS
RW_LORA_W = 32
RW_LORA_A = 32
RW_NPROJ = 5
RW_GN_EPS = RW_HS * 1e-5
S5_CH = 16
S5_GROUPS = W_BRANCH // S5_CH
S5_STATE = 64
NORM_EPS = 1e-6
NEG_BIG = -1e30

OFF_AQ = 0
OFF_AK = OFF_AQ + W_BRANCH
OFF_AV = OFF_AK + W_BRANCH
OFF_B = OFF_AV + W_BRANCH
OFF_C = OFF_B + W_BRANCH
OFF_D = OFF_C + RW_NPROJ * W_BRANCH
OFF_G = OFF_D + W_BRANCH
D_IN = OFF_G + D_MIX

kernel_name = 'hymba_style_diffattn_rglru_rwkv7_s5_step'


def rms_norm(x, g):
    xf = x.astype(jnp.float32)
    y = xf * lax.rsqrt(jnp.mean(xf * xf, axis=-1, keepdims=True) + NORM_EPS)
    return (y * g.astype(jnp.float32)).astype(x.dtype)


def diff_attn_core(q, k, v, q_pos, k_pos, lam):
    s = jnp.einsum('bqhcd,bkhcd->bhcqk', q, k).astype(jnp.float32) * (A_QK ** -0.5)
    mask = k_pos[None, :] <= q_pos[:, None]
    p = jax.nn.softmax(jnp.where(mask, s, NEG_BIG), axis=-1)
    w = p[:, :, 0] - lam * p[:, :, 1]
    return jnp.einsum('bhqk,bkhd->bqhd', w.astype(v.dtype), v)


def diff_attn_prompt(q, k, v, lam):
    b, t = q.shape[0], q.shape[1]
    nblk = t // Q_BLOCK
    qb = jnp.swapaxes(q.reshape(b, nblk, Q_BLOCK, A_HEADS, 2, A_QK), 0, 1)
    k_pos = jnp.arange(t)

    def block(args):
        q_blk, i = args
        q_pos = i * Q_BLOCK + jnp.arange(Q_BLOCK)
        return diff_attn_core(q_blk, k, v, q_pos, k_pos, lam)

    o = lax.map(block, (qb, jnp.arange(nblk)))
    return jnp.swapaxes(o, 0, 1).reshape(b, t, A_HEADS, A_V)


def diff_attn_sample(q, k, v, lam, cache_k_l, cache_v_l, page_table):
    nb, tq = q.shape[0], q.shape[1]
    past = page_table.shape[1] * PAGE_SIZE
    k_past = cache_k_l[page_table].reshape(nb, past, A_HEADS, 2, A_QK).astype(k.dtype)
    v_past = cache_v_l[page_table].reshape(nb, past, A_HEADS, A_V).astype(v.dtype)
    k_all = jnp.concatenate([k_past, k], axis=1)
    v_all = jnp.concatenate([v_past, v], axis=1)
    q_pos = past + jnp.arange(tq)
    k_pos = jnp.arange(past + tq)
    return diff_attn_core(q, k_all, v_all, q_pos, k_pos, lam)


def causal_conv(xb, buf, w, bias):
    xp = jnp.concatenate([buf.astype(xb.dtype), xb], axis=1)
    t = xb.shape[1]
    y = bias + xp[:, 0:t] * w[0]
    for j in range(1, CONV_W):
        y = y + xp[:, j:j + t] * w[j]
    return y, xp[:, xp.shape[1] - (CONV_W - 1):]


def linear_scan(a, b, h0):
    b = b.at[:, 0].add(a[:, 0] * h0)

    def comb(l, r):
        return (l[0] * r[0], r[0] * l[1] + r[1])

    _, h = lax.associative_scan(comb, (a, b), axis=1)
    return h


def rglru(xc, h0, w_a, b_a, w_x, b_x, lam):
    bt, t, _ = xc.shape
    xf = xc.astype(jnp.float32)
    xblk = xf.reshape(bt, t, LRU_BLOCKS, LRU_BW)
    r = jax.nn.sigmoid(jnp.einsum('btnc,ncd->btnd', xblk, w_a).reshape(bt, t, W_BRANCH) + b_a)
    i = jax.nn.sigmoid(jnp.einsum('btnc,ncd->btnd', xblk, w_x).reshape(bt, t, W_BRANCH) + b_x)
    log_a = -LRU_C * r * jax.nn.softplus(-lam.astype(jnp.float32))
    a = jnp.exp(log_a)
    mult = jnp.sqrt(-jnp.expm1(2.0 * log_a))
    h = linear_scan(a, mult * (i * xf), h0.astype(jnp.float32))
    return h, h[:, -1]


def rwkv7_mix(pc, prev, S0, mu, w0, w1, w2, a0, a1, a2, k_k, k_a, r_k, gn_w, gn_b):
    bt, t, _ = pc.shape
    f32 = jnp.float32
    p = pc.astype(f32)
    p_prev = jnp.concatenate([prev.astype(f32)[:, None], p[:, :-1]], axis=1)
    xm = p + (p_prev - p) * mu.astype(f32)
    xr, xw, xk, xv, xa = jnp.split(xm, RW_NPROJ, axis=-1)
    w = -jax.nn.softplus(-(w0 + jnp.tanh(xw @ w1) @ w2)) - 0.5
    decay = jnp.exp(-jnp.exp(w))
    a = jax.nn.sigmoid(a0 + (xa @ a1) @ a2)

    def heads(z):
        return z.reshape(bt, t, RW_HEADS, RW_HS)

    kk = heads(xk * k_k)
    kk = kk / jnp.maximum(jnp.sqrt(jnp.sum(kk * kk, axis=-1, keepdims=True)), 1e-12)
    k = heads(xk * (1.0 + (a - 1.0) * k_a))
    r, v, a_h, dec_h = heads(xr), heads(xv), heads(a), heads(decay)

    def step(S, inp):
        r_t, w_t, k_t, v_t, kk_t, a_t = inp
        sa = jnp.einsum('bhij,bhj->bhi', S, -kk_t)
        S = (S * w_t[:, :, None, :] + sa[..., None] * (kk_t * a_t)[:, :, None, :]
             + v_t[..., None] * k_t[:, :, None, :])
        return S, jnp.einsum('bhij,bhj->bhi', S, r_t)

    seq = tuple(jnp.swapaxes(z, 0, 1) for z in (r, dec_h, k, v, kk, a_h))
    S, y = lax.scan(step, S0.astype(f32), seq)
    y = jnp.swapaxes(y, 0, 1)
    mean = jnp.mean(y, axis=-1, keepdims=True)
    var = jnp.mean(jnp.square(y - mean), axis=-1, keepdims=True)
    y = ((y - mean) * lax.rsqrt(var + RW_GN_EPS) * gn_w.reshape(RW_HEADS, RW_HS)
         + gn_b.reshape(RW_HEADS, RW_HS))
    y = y + jnp.sum(r * k * r_k, axis=-1, keepdims=True) * v
    return y.reshape(bt, t, W_BRANCH), pc[:, -1], S


def s5_mix(u, x0, lam_re, lam_im, log_dt, b_re, b_im, c_re, c_im, d, w_glu, b_glu):
    bt, t, _ = u.shape
    f32 = jnp.float32
    uf = u.astype(f32)
    ug = uf.reshape(bt, t, S5_GROUPS, S5_CH)
    lr, li = lam_re.astype(f32), lam_im.astype(f32)
    dt = jnp.exp(log_dt.astype(f32))[:, None]
    mag = jnp.exp(lr * dt)
    ab_re, ab_im = mag * jnp.cos(li * dt), mag * jnp.sin(li * dt)
    den = lr * lr + li * li
    pr = ab_re - 1.0
    f_re = (pr * lr + ab_im * li) / den
    f_im = (ab_im * lr - pr * li) / den
    bb_re = f_re[..., None] * b_re - f_im[..., None] * b_im
    bb_im = f_re[..., None] * b_im + f_im[..., None] * b_re
    bu_re = jnp.einsum('btgc,gnc->btgn', ug, bb_re)
    bu_im = jnp.einsum('btgc,gnc->btgn', ug, bb_im)
    h0_re, h0_im = x0[..., 0].astype(f32), x0[..., 1].astype(f32)
    bu_re = bu_re.at[:, 0].add(ab_re * h0_re - ab_im * h0_im)
    bu_im = bu_im.at[:, 0].add(ab_re * h0_im + ab_im * h0_re)
    a_re = jnp.broadcast_to(ab_re, bu_re.shape)
    a_im = jnp.broadcast_to(ab_im, bu_im.shape)

    def comb(l, r):
        lar, lai, lbr, lbi = l
        rar, rai, rbr, rbi = r
        return (rar * lar - rai * lai, rar * lai + rai * lar,
                rar * lbr - rai * lbi + rbr, rar * lbi + rai * lbr + rbi)

    _, _, h_re, h_im = lax.associative_scan(comb, (a_re, a_im, bu_re, bu_im), axis=1)
    y = jnp.einsum('btgn,gcn->btgc', h_re, c_re) - jnp.einsum('btgn,gcn->btgc', h_im, c_im)
    y = y.reshape(bt, t, W_BRANCH) + d * uf
    z = jax.nn.gelu(y)
    out = z * jax.nn.sigmoid(z @ w_glu + b_glu)
    return out, jnp.stack([h_re[:, -1], h_im[:, -1]], axis=-1)


def mixer_layer(x, attend, states, lam_init, norm_pre, norm_post, w_in, w_out,
                lam_q1, lam_k1, lam_q2, lam_k2, subln_w, conv_w, conv_b,
                lru_wa, lru_ba, lru_wx, lru_bx, lru_lam,
                rw_mu, rw_w0, rw_w1, rw_w2, rw_a0, rw_a1, rw_a2, rw_kk, rw_ka, rw_rk, rw_gnw, rw_gnb,
                s5_lre, s5_lim, s5_logdt, s5_bre, s5_bim, s5_cre, s5_cim, s5_d, s5_wglu, s5_bglu):
    conv_buf, lru_h, shift_prev, wkv_S, ssm_x = states
    b, t, _ = x.shape
    f32 = jnp.float32
    h = rms_norm(x, norm_pre)
    proj = jnp.einsum('btd,de->bte', h, w_in)
    q = proj[..., OFF_AQ:OFF_AK].reshape(b, t, A_HEADS, 2, A_QK)
    k = proj[..., OFF_AK:OFF_AV].reshape(b, t, A_HEADS, 2, A_QK)
    v = proj[..., OFF_AV:OFF_B].reshape(b, t, A_HEADS, A_V)
    xb = proj[..., OFF_B:OFF_C]
    pc = proj[..., OFF_C:OFF_D]
    u = proj[..., OFF_D:OFF_G]
    gate = proj[..., OFF_G:]
    lam = (jnp.exp(jnp.sum(lam_q1.astype(f32) * lam_k1.astype(f32)))
           - jnp.exp(jnp.sum(lam_q2.astype(f32) * lam_k2.astype(f32))) + lam_init)
    oa = attend(q, k, v, lam)
    oa = (rms_norm(oa, subln_w) * (1.0 - lam_init)).reshape(b, t, W_BRANCH)
    xc, new_conv = causal_conv(xb, conv_buf, conv_w, conv_b)
    ob, new_h = rglru(xc, lru_h, lru_wa, lru_ba, lru_wx, lru_bx, lru_lam)
    oc, new_shift, new_S = rwkv7_mix(pc, shift_prev, wkv_S, rw_mu, rw_w0, rw_w1, rw_w2,
                                     rw_a0, rw_a1, rw_a2, rw_kk, rw_ka, rw_rk, rw_gnw, rw_gnb)
    od, new_ssm = s5_mix(u, ssm_x, s5_lre, s5_lim, s5_logdt, s5_bre, s5_bim, s5_cre, s5_cim,
                         s5_d, s5_wglu, s5_bglu)
    o = jnp.concatenate([oa.astype(x.dtype), ob.astype(x.dtype), oc.astype(x.dtype),
                         od.astype(x.dtype)], axis=-1) * jax.nn.silu(gate)
    y = x + rms_norm(jnp.einsum('bte,ed->btd', o, w_out), norm_post)
    new_states = (k.reshape(b, t, A_HEADS, 2 * A_QK), v, new_conv, new_h, new_shift, new_S, new_ssm)
    return y, new_states


def setup_inputs(seed: int = 0) -> dict:
    key = jax.random.key(seed)
    keys = iter(jax.random.split(key, 64))
    f32 = jnp.float32

    def nrm(shape, scale):
        return jax.random.normal(next(keys), shape, f32) * scale

    def unif(shape, lo, hi):
        return jax.random.uniform(next(keys), shape, f32, lo, hi)

    W = W_BRANCH
    n_pages = PAST_LEN // PAGE_SIZE
    n_used = DEC_BATCH * n_pages
    n_pool = n_used + n_used // 4
    x_prompt = nrm((BATCH, SEQ, D_MODEL), 1.0)
    x_sample = nrm((DEC_BATCH, DEC_SEQ, D_MODEL), 1.0)
    cache_k = nrm((DEPTH, n_pool, PAGE_SIZE, A_HEADS, 2 * A_QK), 1.0)
    cache_v = nrm((DEPTH, n_pool, PAGE_SIZE, A_HEADS, A_V), 1.0)
    page_table = jax.random.permutation(next(keys), n_pool)[:n_used].reshape(DEC_BATCH, n_pages).astype(jnp.int32)
    state_conv = nrm((DEPTH, DEC_BATCH, CONV_W - 1, W), 1.0)
    state_lru = nrm((DEPTH, DEC_BATCH, W), 0.5)
    state_shift = nrm((DEPTH, DEC_BATCH, RW_NPROJ * W), 1.0)
    state_wkv = nrm((DEPTH, DEC_BATCH, RW_HEADS, RW_HS, RW_HS), 0.3)
    state_ssm = nrm((DEPTH, DEC_BATCH, S5_GROUPS, S5_STATE, 2), 0.1)
    norm_pre = 1.0 + nrm((DEPTH, D_MODEL), 0.05)
    norm_post = 1.0 + nrm((DEPTH, D_MODEL), 0.05)
    w_in = nrm((DEPTH, D_MODEL, D_IN), D_MODEL ** -0.5)
    w_out = nrm((DEPTH, D_MIX, D_MODEL), D_MIX ** -0.5)
    lam_q1 = nrm((DEPTH, A_QK), 0.1)
    lam_k1 = nrm((DEPTH, A_QK), 0.1)
    lam_q2 = nrm((DEPTH, A_QK), 0.1)
    lam_k2 = nrm((DEPTH, A_QK), 0.1)
    subln_w = 1.0 + nrm((DEPTH, A_V), 0.05)
    conv_w = nrm((DEPTH, CONV_W, W), CONV_W ** -0.5)
    conv_b = nrm((DEPTH, W), 0.01)
    lru_wa = nrm((DEPTH, LRU_BLOCKS, LRU_BW, LRU_BW), LRU_BW ** -0.5)
    lru_ba = nrm((DEPTH, W), 0.01)
    lru_wx = nrm((DEPTH, LRU_BLOCKS, LRU_BW, LRU_BW), LRU_BW ** -0.5)
    lru_bx = nrm((DEPTH, W), 0.01)
    a_init = unif((DEPTH, W), 0.9, 0.999)
    lru_lam = jnp.log(a_init) - jnp.log1p(-a_init)
    rw_mu = unif((DEPTH, RW_NPROJ * W), 0.0, 1.0)
    rw_w0 = unif((DEPTH, W), -5.0, 1.0)
    rw_w1 = nrm((DEPTH, W, RW_LORA_W), W ** -0.5)
    rw_w2 = nrm((DEPTH, RW_LORA_W, W), 0.1 * RW_LORA_W ** -0.5)
    rw_a0 = nrm((DEPTH, W), 0.1)
    rw_a1 = nrm((DEPTH, W, RW_LORA_A), W ** -0.5)
    rw_a2 = nrm((DEPTH, RW_LORA_A, W), 0.1 * RW_LORA_A ** -0.5)
    rw_kk = 0.85 + nrm((DEPTH, W), 0.05)
    rw_ka = 1.0 + nrm((DEPTH, W), 0.05)
    rw_rk = nrm((DEPTH, RW_HEADS, RW_HS), 0.1)
    rw_gnw = 1.0 + nrm((DEPTH, W), 0.05)
    rw_gnb = nrm((DEPTH, W), 0.01)
    s5_lre = -0.5 + nrm((DEPTH, S5_GROUPS, S5_STATE), 0.01)
    s5_lim = math.pi * jnp.arange(S5_STATE, dtype=f32) + nrm((DEPTH, S5_GROUPS, S5_STATE), 0.01)
    s5_logdt = unif((DEPTH, S5_GROUPS), math.log(0.001), math.log(0.1))
    s5_bre = nrm((DEPTH, S5_GROUPS, S5_STATE, S5_CH), (2 * S5_CH) ** -0.5)
    s5_bim = nrm((DEPTH, S5_GROUPS, S5_STATE, S5_CH), (2 * S5_CH) ** -0.5)
    s5_cre = nrm((DEPTH, S5_GROUPS, S5_CH, S5_STATE), (2 * S5_STATE) ** -0.5)
    s5_cim = nrm((DEPTH, S5_GROUPS, S5_CH, S5_STATE), (2 * S5_STATE) ** -0.5)
    s5_d = nrm((DEPTH, W), 0.5)
    s5_wglu = nrm((DEPTH, W, W), W ** -0.5)
    s5_bglu = nrm((DEPTH, W), 0.01)
    return {'x_prompt': x_prompt, 'x_sample': x_sample, 'cache_k': cache_k, 'cache_v': cache_v,
            'page_table': page_table, 'state_conv': state_conv, 'state_lru': state_lru,
            'state_shift': state_shift, 'state_wkv': state_wkv, 'state_ssm': state_ssm,
            'norm_pre': norm_pre, 'norm_post': norm_post, 'w_in': w_in, 'w_out': w_out,
            'lam_q1': lam_q1, 'lam_k1': lam_k1, 'lam_q2': lam_q2, 'lam_k2': lam_k2,
            'subln_w': subln_w, 'conv_w': conv_w, 'conv_b': conv_b,
            'lru_wa': lru_wa, 'lru_ba': lru_ba, 'lru_wx': lru_wx, 'lru_bx': lru_bx, 'lru_lam': lru_lam,
            'rw_mu': rw_mu, 'rw_w0': rw_w0, 'rw_w1': rw_w1, 'rw_w2': rw_w2, 'rw_a0': rw_a0,
            'rw_a1': rw_a1, 'rw_a2': rw_a2, 'rw_kk': rw_kk, 'rw_ka': rw_ka, 'rw_rk': rw_rk,
            'rw_gnw': rw_gnw, 'rw_gnb': rw_gnb,
            's5_lre': s5_lre, 's5_lim': s5_lim, 's5_logdt': s5_logdt, 's5_bre': s5_bre,
            's5_bim': s5_bim, 's5_cre': s5_cre, 's5_cim': s5_cim, 's5_d': s5_d,
            's5_wglu': s5_wglu, 's5_bglu': s5_bglu}


def reference(x_prompt, x_sample, cache_k, cache_v, page_table, state_conv, state_lru,
              state_shift, state_wkv, state_ssm, norm_pre, norm_post, w_in, w_out,
              lam_q1, lam_k1, lam_q2, lam_k2, subln_w, conv_w, conv_b,
              lru_wa, lru_ba, lru_wx, lru_bx, lru_lam,
              rw_mu, rw_w0, rw_w1, rw_w2, rw_a0, rw_a1, rw_a2, rw_kk, rw_ka, rw_rk, rw_gnw, rw_gnb,
              s5_lre, s5_lim, s5_logdt, s5_bre, s5_bim, s5_cre, s5_cim, s5_d, s5_wglu, s5_bglu):
    weights = (norm_pre, norm_post, w_in, w_out, lam_q1, lam_k1, lam_q2, lam_k2, subln_w,
               conv_w, conv_b, lru_wa, lru_ba, lru_wx, lru_bx, lru_lam,
               rw_mu, rw_w0, rw_w1, rw_w2, rw_a0, rw_a1, rw_a2, rw_kk, rw_ka, rw_rk, rw_gnw, rw_gnb,
               s5_lre, s5_lim, s5_logdt, s5_bre, s5_bim, s5_cre, s5_cim, s5_d, s5_wglu, s5_bglu)
    dt = x_prompt.dtype
    nbp = x_prompt.shape[0]
    zero_states = (jnp.zeros((nbp, CONV_W - 1, W_BRANCH), dt),
                   jnp.zeros((nbp, W_BRANCH), dt),
                   jnp.zeros((nbp, RW_NPROJ * W_BRANCH), dt),
                   jnp.zeros((nbp, RW_HEADS, RW_HS, RW_HS), dt),
                   jnp.zeros((nbp, S5_GROUPS, S5_STATE, 2), dt))
    xp, xs = x_prompt, x_sample
    outs_p = [[] for _ in range(7)]
    outs_s = [[] for _ in range(7)]
    for l in range(DEPTH):
        wl = tuple(w[l] for w in weights)
        lam_init = 0.8 - 0.6 * math.exp(-0.3 * l)
        xp, st_p = mixer_layer(xp, diff_attn_prompt, zero_states, lam_init, *wl)
        attend_s = functools.partial(diff_attn_sample, cache_k_l=cache_k[l],
                                     cache_v_l=cache_v[l], page_table=page_table)
        st_in = (state_conv[l], state_lru[l], state_shift[l], state_wkv[l], state_ssm[l])
        xs, st_s = mixer_layer(xs, attend_s, st_in, lam_init, *wl)
        for i in range(7):
            outs_p[i].append(st_p[i])
            outs_s[i].append(st_s[i])
    k_p, v_p, conv_p, lru_p, shift_p, wkv_p, ssm_p = [jnp.stack(z) for z in outs_p]
    k_s, v_s, conv_s, lru_s, shift_s, wkv_s, ssm_s = [jnp.stack(z) for z in outs_s]
    return (xp, xs, k_p, k_s, v_p, v_s, conv_p, conv_s, lru_p, lru_s,
            shift_p, shift_s, wkv_p, wkv_s, ssm_p, ssm_s)
```

```python
import functools
import math

import jax
import jax.numpy as jnp
from jax import lax
from jax.experimental import pallas as pl
from jax.experimental.pallas import tpu as pltpu

F32 = jnp.float32
BF16 = jnp.bfloat16

D_MODEL = 1024
W_BRANCH = 256
A_HEADS = 4
A_QK = 32
A_V = 64
PAGE_SIZE = 128
LRU_BLOCKS = 4
LRU_BW = 64
CONV_W = 4
LRU_C = 8.0
RW_HEADS = 4
RW_HS = 64
RW_NPROJ = 5
RW_GN_EPS = RW_HS * 1e-5
S5_CH = 16
S5_GROUPS = 16
S5_STATE = 64
S5_W = S5_GROUPS * S5_STATE
NORM_EPS = 1e-6
NEG_BIG = -1e30

OFF_AQ = 0
OFF_AK = OFF_AQ + W_BRANCH
OFF_AV = OFF_AK + W_BRANCH
OFF_B = OFF_AV + W_BRANCH
OFF_C = OFF_B + W_BRANCH
OFF_D = OFF_C + RW_NPROJ * W_BRANCH
OFF_G = OFF_D + W_BRANCH
D_IN = OFF_G + D_MODEL
_SEGS = ((OFF_AQ, OFF_AK), (OFF_AK, OFF_AV), (OFF_AV, OFF_B), (OFF_B, OFF_C),
         (OFF_C, OFF_D), (OFF_D, OFF_G), (OFF_G, D_IN))

SUBLANES = 8
LANES = 128
VMEM_LIMIT_MIB = 56


def _cparams(*sem):
    return pltpu.CompilerParams(dimension_semantics=sem,
                                vmem_limit_bytes=VMEM_LIMIT_MIB * 1024 * 1024)


def _nt(a, b):
    return lax.dot_general(a, b, (((1,), (1,)), ((), ())), preferred_element_type=F32)


def _mm(a, b):
    return jnp.dot(a, b, preferred_element_type=F32)


def _segsum(x, ones_blk):
    hi = x.astype(BF16)
    lo = (x - hi.astype(F32)).astype(BF16)
    return _mm(hi, ones_blk) + _mm(lo, ones_blk)


def _sigmoid(x):
    return jax.nn.sigmoid(x)


def _softplus(x):
    return jnp.maximum(x, 0.0) + jnp.log1p(jnp.exp(-jnp.abs(x)))


def _inproj_body(x_ref, g_ref, w_ref, *out_refs):
    x = x_ref[...]
    h = x * lax.rsqrt(jnp.mean(x * x, axis=-1, keepdims=True) + NORM_EPS) * g_ref[...]
    hb = h.astype(BF16)
    for (lo, hi), o_ref in zip(_SEGS, out_refs):
        o_ref[...] = _mm(hb, w_ref[:, lo:hi])


def _inproj(x2, g, w_bf, tm):
    n = x2.shape[0]
    row = lambda i: (i, 0)
    const = lambda i: (0, 0)
    return pl.pallas_call(
        _inproj_body,
        grid=(n // tm,),
        in_specs=[pl.BlockSpec((tm, D_MODEL), row),
                  pl.BlockSpec((1, D_MODEL), const),
                  pl.BlockSpec((D_MODEL, D_IN), const)],
        out_specs=[pl.BlockSpec((tm, hi - lo), row) for lo, hi in _SEGS],
        out_shape=[jax.ShapeDtypeStruct((n, hi - lo), F32) for lo, hi in _SEGS],
        compiler_params=_cparams("parallel"),
        name="inproj",
    )(x2, g, w_bf)


def _attn_prompt_body(lam_ref, q_ref, k_ref, v_ref, o_ref, qm_ref, acc_ref, m_ref, l_ref, *, tq):
    qi = pl.program_id(1)
    nhc = 2 * A_HEADS
    lane = lax.broadcasted_iota(jnp.int32, (1, W_BRANCH), 1)
    q = q_ref[0] * (A_QK ** -0.5)
    for hc in range(nhc):
        qm_ref[hc * tq:(hc + 1) * tq, :] = jnp.where(lane // A_QK == hc, q, 0.0).astype(BF16)
    m_ref[...] = jnp.full(m_ref.shape, NEG_BIG, F32)
    l_ref[...] = jnp.zeros(l_ref.shape, F32)
    acc_ref[...] = jnp.zeros(acc_ref.shape, F32)
    row = qi * tq + lax.broadcasted_iota(jnp.int32, (tq, 1), 0)

    def body(kb, carry):
        ks = pl.multiple_of(kb * tq, tq)
        kblk = k_ref[0, pl.ds(ks, tq), :].astype(BF16)
        vblk = v_ref[0, pl.ds(ks, tq), :]
        causal = (ks + lax.broadcasted_iota(jnp.int32, (1, tq), 1)) <= row
        s_all = _nt(qm_ref[...], kblk)
        for h in range(A_HEADS):
            vm = jnp.where(lane // A_V == h, vblk, 0.0).astype(BF16)
            for c in range(2):
                hc = 2 * h + c
                s = jnp.where(causal, s_all[hc * tq:(hc + 1) * tq], NEG_BIG)
                m_old = m_ref[hc]
                m_new = jnp.maximum(m_old, jnp.max(s, axis=-1, keepdims=True))
                alpha = jnp.exp(m_old - m_new)
                p = jnp.exp(s - m_new)
                l_ref[hc] = alpha * l_ref[hc] + jnp.sum(p, axis=-1, keepdims=True)
                acc_ref[hc] = alpha * acc_ref[hc] + _mm(p.astype(BF16), vm)
                m_ref[hc] = m_new
        return carry

    lax.fori_loop(0, qi + 1, body, 0)
    lam = lam_ref[0]
    out = jnp.zeros((tq, W_BRANCH), F32)
    for h in range(A_HEADS):
        out = out + (acc_ref[2 * h] * (1.0 / l_ref[2 * h])
                     - lam * (acc_ref[2 * h + 1] * (1.0 / l_ref[2 * h + 1])))
    o_ref[0] = out


def _attn_prompt(lam, q, k, v, tq):
    b, t, _ = q.shape
    nhc = 2 * A_HEADS
    return pl.pallas_call(
        functools.partial(_attn_prompt_body, tq=tq),
        grid=(b, t // tq),
        in_specs=[pl.BlockSpec(memory_space=pltpu.SMEM),
                  pl.BlockSpec((1, tq, W_BRANCH), lambda i, j: (i, j, 0)),
                  pl.BlockSpec((1, t, W_BRANCH), lambda i, j: (i, 0, 0)),
                  pl.BlockSpec((1, t, W_BRANCH), lambda i, j: (i, 0, 0))],
        out_specs=pl.BlockSpec((1, tq, W_BRANCH), lambda i, j: (i, j, 0)),
        out_shape=jax.ShapeDtypeStruct((b, t, W_BRANCH), F32),
        scratch_shapes=[pltpu.VMEM((nhc * tq, W_BRANCH), BF16),
                        pltpu.VMEM((nhc, tq, W_BRANCH), F32),
                        pltpu.VMEM((nhc, tq, 1), F32),
                        pltpu.VMEM((nhc, tq, 1), F32)],
        compiler_params=_cparams("parallel", "arbitrary"),
        name="attn_prompt",
    )(lam, q, k, v)


def _attn_sample_body(pt_ref, lam_ref, q_ref, kn_ref, vn_ref, *rest, n_pages, tq):
    del pt_ref
    kp_refs = rest[:n_pages]
    vp_refs = rest[n_pages:2 * n_pages]
    o_ref = rest[2 * n_pages]
    nhc = 2 * A_HEADS
    lane = lax.broadcasted_iota(jnp.int32, (1, W_BRANCH), 1)
    rowhc = lax.broadcasted_iota(jnp.int32, (nhc * tq, 1), 0) // tq
    q = q_ref[0] * (A_QK ** -0.5)
    qbd = jnp.where(lane // A_QK == rowhc, jnp.tile(q, (nhc, 1)), 0.0).astype(BF16)
    s_past = jnp.concatenate([_nt(qbd, kp_refs[j][0].astype(BF16)) for j in range(n_pages)], axis=1)
    s_new = _nt(qbd, kn_ref[0].astype(BF16))
    tcol = lax.broadcasted_iota(jnp.int32, (nhc * tq, tq), 1)
    trow = lax.broadcasted_iota(jnp.int32, (nhc * tq, tq), 0) % tq
    s_new = jnp.where(tcol <= trow, s_new, NEG_BIG)
    m = jnp.maximum(jnp.max(s_past, axis=-1, keepdims=True), jnp.max(s_new, axis=-1, keepdims=True))
    p_past = jnp.exp(s_past - m)
    p_new = jnp.exp(s_new - m)
    inv = 1.0 / (jnp.sum(p_past, axis=-1, keepdims=True) + jnp.sum(p_new, axis=-1, keepdims=True))
    p_past = p_past * inv
    p_new = p_new * inv
    lam = lam_ref[0]

    def diff(p):
        return jnp.concatenate(
            [p[(2 * h) * tq:(2 * h + 1) * tq] - lam * p[(2 * h + 1) * tq:(2 * h + 2) * tq]
             for h in range(A_HEADS)], axis=0)

    w_past = diff(p_past).astype(BF16)
    w_new = diff(p_new).astype(BF16)
    out_all = _mm(w_new, vn_ref[0].astype(BF16))
    for j in range(n_pages):
        out_all = out_all + _mm(w_past[:, j * PAGE_SIZE:(j + 1) * PAGE_SIZE], vp_refs[j][0].astype(BF16))
    out = jnp.zeros((tq, W_BRANCH), F32)
    for h in range(A_HEADS):
        out = out + jnp.where(lane // A_V == h, out_all[h * tq:(h + 1) * tq], 0.0)
    o_ref[0] = out


def _attn_sample(lam, q, k, v, cache_k_l, cache_v_l, page_table):
    b, tq, _ = q.shape
    n_pages = page_table.shape[1]
    pt = page_table.reshape(-1)
    ck = cache_k_l.reshape(cache_k_l.shape[0], PAGE_SIZE, W_BRANCH)
    cv = cache_v_l.reshape(cache_v_l.shape[0], PAGE_SIZE, W_BRANCH)
    seq = pl.BlockSpec((1, tq, W_BRANCH), lambda i, pt_r, lam_r: (i, 0, 0))

    def page_spec(j):
        return pl.BlockSpec((1, PAGE_SIZE, W_BRANCH),
                            lambda i, pt_r, lam_r: (pt_r[i * n_pages + j], 0, 0))

    grid_spec = pltpu.PrefetchScalarGridSpec(
        num_scalar_prefetch=2,
        grid=(b,),
        in_specs=[seq, seq, seq] + [page_spec(j) for j in range(n_pages)] * 2,
        out_specs=seq,
    )
    return pl.pallas_call(
        functools.partial(_attn_sample_body, n_pages=n_pages, tq=tq),
        grid_spec=grid_spec,
        out_shape=jax.ShapeDtypeStruct((b, tq, W_BRANCH), F32),
        compiler_params=_cparams("parallel"),
        name="attn_sample",
    )(pt, lam, q, k, v, *([ck] * n_pages), *([cv] * n_pages))


def _tile_rows(c8, groups):
    return jnp.tile(c8, (groups, 1)) if groups > 1 else c8


def _shift_in(prev8, x, s):
    tt = x.shape[1]
    ext = jnp.concatenate([prev8, x], axis=1)
    return pltpu.roll(ext, s, axis=1)[:, SUBLANES:SUBLANES + tt]


def _mixb_body(xb_ref, cst_ref, h0_ref, cw_ref, cb_ref, wg_ref, bg_ref, lam_ref,
               ob_ref, cout_ref, hout_ref, prev_ref, hcar_ref, *, bb, tt):
    @pl.when(pl.program_id(1) == 0)
    def _():
        prev_ref[...] = cst_ref[...]
        hcar_ref[...] = h0_ref[...]

    w = W_BRANCH
    x = xb_ref[...]
    prev8 = prev_ref[...]
    cw = cw_ref[...]
    xc = cb_ref[...] + _shift_in(prev8, x, 3) * cw[0:1]
    xc = xc + _shift_in(prev8, x, 2) * cw[1:2]
    xc = xc + _shift_in(prev8, x, 1) * cw[2:3]
    xc = xc + x * cw[3:4]
    tail = x[:, tt - SUBLANES:tt]
    prev_ref[...] = tail
    cout_ref[...] = tail

    xf = xc.reshape(bb * tt, w)
    gates = _mm(xf.astype(BF16), wg_ref[...]) + bg_ref[...]
    r = _sigmoid(gates[:, :w])
    i = _sigmoid(gates[:, w:])
    log_a = -LRU_C * r * _softplus(-lam_ref[...])
    a = jnp.exp(log_a).reshape(bb, tt, w)
    b = (jnp.sqrt(1.0 - jnp.exp(2.0 * log_a)) * (i * xf)).reshape(bb, tt, w)

    rmod = lax.broadcasted_iota(jnp.int32, (1, tt, 1), 1) % SUBLANES
    for s in (1, 2, 4):
        keep = rmod >= s
        b = b + a * jnp.where(keep, pltpu.roll(b, s, axis=1), 0.0)
        a = a * jnp.where(keep, pltpu.roll(a, s, axis=1), 1.0)
    h = hcar_ref[...]
    for g in range(tt // SUBLANES):
        sl = slice(g * SUBLANES, (g + 1) * SUBLANES)
        hg = b[:, sl] + a[:, sl] * h
        ob_ref[:, sl, :] = hg
        h = hg[:, SUBLANES - 1:SUBLANES]
    hcar_ref[...] = h
    hout_ref[...] = h


def _mixb(xb, cst8, h0, cw, cb, wg, bg, lru_lam, bb, tt):
    b, t, w = xb.shape
    seq = lambda i, j: (i, j, 0)
    per_b = lambda i, j: (i, 0, 0)
    const = lambda i, j: (0, 0)
    return pl.pallas_call(
        functools.partial(_mixb_body, bb=bb, tt=tt),
        grid=(b // bb, t // tt),
        in_specs=[pl.BlockSpec((bb, tt, w), seq),
                  pl.BlockSpec((bb, SUBLANES, w), per_b),
                  pl.BlockSpec((bb, 1, w), per_b),
                  pl.BlockSpec((CONV_W, w), const),
                  pl.BlockSpec((1, w), const),
                  pl.BlockSpec((w, 2 * w), const),
                  pl.BlockSpec((1, 2 * w), const),
                  pl.BlockSpec((1, w), const)],
        out_specs=[pl.BlockSpec((bb, tt, w), seq),
                   pl.BlockSpec((bb, SUBLANES, w), per_b),
                   pl.BlockSpec((bb, 1, w), per_b)],
        out_shape=[jax.ShapeDtypeStruct((b, t, w), F32),
                   jax.ShapeDtypeStruct((b, SUBLANES, w), F32),
                   jax.ShapeDtypeStruct((b, 1, w), F32)],
        scratch_shapes=[pltpu.VMEM((bb, SUBLANES, w), F32), pltpu.VMEM((bb, 1, w), F32)],
        compiler_params=_cparams("parallel", "arbitrary"),
        name="conv_rglru",
    )(xb, cst8, h0, cw, cb, wg, bg, lru_lam)


def _rwkv_pre_body(pc_ref, sh0_ref, mu_ref, w0_ref, w1_ref, w2_ref, a0_ref, a1_ref, a2_ref,
                   kkw_ref, ka_ref, rk_ref, ones_ref,
                   jv_ref, v_ref, bonus_ref, shout_ref, prev_ref, *, bb, tt):
    @pl.when(pl.program_id(1) == 0)
    def _():
        prev_ref[...] = jnp.broadcast_to(sh0_ref[...], prev_ref.shape)

    w = W_BRANCH
    p = pc_ref[...]
    p_prev = _shift_in(prev_ref[...], p, 1)
    tail = p[:, tt - SUBLANES:tt]
    prev_ref[...] = tail
    shout_ref[...] = p[:, tt - 1:tt]
    xm = (p + (p_prev - p) * mu_ref[...]).reshape(bb * tt, RW_NPROJ * w)
    xr, xw, xk, xv, xa = (xm[:, n * w:(n + 1) * w] for n in range(RW_NPROJ))
    lw = _mm(jnp.tanh(_mm(xw.astype(BF16), w1_ref[...])).astype(BF16), w2_ref[...])
    wlog = -_softplus(-(w0_ref[...] + lw)) - 0.5
    decay = jnp.exp(-jnp.exp(wlog))
    a = _sigmoid(a0_ref[...] + _mm(_mm(xa.astype(BF16), a1_ref[...]).astype(BF16), a2_ref[...]))
    kk = xk * kkw_ref[...]
    ones_blk = ones_ref[...]
    kk = kk / jnp.maximum(jnp.sqrt(_segsum(kk * kk, ones_blk)), 1e-12)
    k = xk * (1.0 + (a - 1.0) * ka_ref[...])
    bonus = _segsum(xr * k * rk_ref[...], ones_blk) * xv
    shp = (bb, tt, w)
    jv_ref[:, :, 0 * w:1 * w] = decay.reshape(shp)
    jv_ref[:, :, 1 * w:2 * w] = (-kk).reshape(shp)
    jv_ref[:, :, 2 * w:3 * w] = (kk * a).reshape(shp)
    jv_ref[:, :, 3 * w:4 * w] = k.reshape(shp)
    jv_ref[:, :, 4 * w:5 * w] = xr.reshape(shp)
    v_ref[...] = xv.reshape(shp)
    bonus_ref[...] = bonus.reshape(shp)


def _rwkv_pre(pc, sh0, mu, w0, w1, w2, a0, a1, a2, kkw, ka, rk, ones_blk, bb, tt):
    b, t, wp = pc.shape
    w = W_BRANCH
    seq = lambda i, j: (i, j, 0)
    per_b = lambda i, j: (i, 0, 0)
    const = lambda i, j: (0, 0)
    vec = pl.BlockSpec((1, w), const)
    return pl.pallas_call(
        functools.partial(_rwkv_pre_body, bb=bb, tt=tt),
        grid=(b // bb, t // tt),
        in_specs=[pl.BlockSpec((bb, tt, wp), seq),
                  pl.BlockSpec((bb, 1, wp), per_b),
                  pl.BlockSpec((1, wp), const),
                  vec, pl.BlockSpec(w1.shape, const), pl.BlockSpec(w2.shape, const),
                  vec, pl.BlockSpec(a1.shape, const), pl.BlockSpec(a2.shape, const),
                  vec, vec, vec, pl.BlockSpec((w, w), const)],
        out_specs=[pl.BlockSpec((bb, tt, wp), seq),
                   pl.BlockSpec((bb, tt, w), seq),
                   pl.BlockSpec((bb, tt, w), seq),
                   pl.BlockSpec((bb, 1, wp), per_b)],
        out_shape=[jax.ShapeDtypeStruct((b, t, wp), F32),
                   jax.ShapeDtypeStruct((b, t, w), F32),
                   jax.ShapeDtypeStruct((b, t, w), F32),
                   jax.ShapeDtypeStruct((b, 1, wp), F32)],
        scratch_shapes=[pltpu.VMEM((bb, SUBLANES, wp), F32)],
        compiler_params=_cparams("parallel", "arbitrary"),
        name="rwkv_pre",
    )(pc, sh0, mu, w0, w1, w2, a0, a1, a2, kkw, ka, rk, ones_blk)


_JV_DECAY, _JV_NKK, _JV_KKA, _JV_K, _JV_R = range(5)


def _tree_sum(parts):
    while len(parts) > 1:
        parts = [parts[n] + parts[n + 1] for n in range(0, len(parts), 2)]
    return parts[0]


def _rwkv_seq_body(jv_ref, v_ref, s0_ref, y_ref, s_ref, *, tt, n_acc):
    @pl.when(pl.program_id(1) == 0)
    def _():
        s_ref[...] = s0_ref[...]

    def step(t, carry):
        def jrow(which, j):
            return jv_ref[0, t, which, pl.ds(j, 1), :]

        parts = [None] * n_acc
        for j in range(RW_HS):
            term = s_ref[0, j] * jrow(_JV_NKK, j)
            parts[j % n_acc] = term if parts[j % n_acc] is None else parts[j % n_acc] + term
        sa = _tree_sum(parts)
        v = v_ref[0, t]
        parts = [None] * n_acc
        for j in range(RW_HS):
            s = s_ref[0, j] * jrow(_JV_DECAY, j) + sa * jrow(_JV_KKA, j) + v * jrow(_JV_K, j)
            s_ref[0, j] = s
            term = s * jrow(_JV_R, j)
            parts[j % n_acc] = term if parts[j % n_acc] is None else parts[j % n_acc] + term
        y_ref[0, t] = _tree_sum(parts)
        return carry

    lax.fori_loop(0, tt, step, 0)


def _rwkv_seq(jv, v, s0, tt):
    g, t, nq, hs, _ = jv.shape
    rows = v.shape[2]
    return pl.pallas_call(
        functools.partial(_rwkv_seq_body, tt=tt, n_acc=4),
        grid=(g, t // tt),
        in_specs=[pl.BlockSpec((1, tt, nq, hs, LANES), lambda i, j: (i, j, 0, 0, 0)),
                  pl.BlockSpec((1, tt, rows, LANES), lambda i, j: (i, j, 0, 0)),
                  pl.BlockSpec((1, hs, rows, LANES), lambda i, j: (i, 0, 0, 0))],
        out_specs=[pl.BlockSpec((1, tt, rows, LANES), lambda i, j: (i, j, 0, 0)),
                   pl.BlockSpec((1, hs, rows, LANES), lambda i, j: (i, 0, 0, 0))],
        out_shape=[jax.ShapeDtypeStruct((g, t, rows, LANES), F32),
                   jax.ShapeDtypeStruct((g, hs, rows, LANES), F32)],
        compiler_params=_cparams("parallel", "arbitrary"),
        name="rwkv_scan",
    )(jv, v, s0)


def _lane_split(b):
    seqs = min(b, LANES // RW_HEADS)
    return b // seqs, seqs, LANES // (RW_HEADS * seqs)


def _to_lanes_j(x, g, bl, il):
    _, t, wq = x.shape
    nq = wq // W_BRANCH
    x = x.reshape(g, bl, t, nq, RW_HEADS, RW_HS).transpose(0, 2, 3, 5, 1, 4)
    x = jnp.broadcast_to(x[..., None], x.shape + (il,))
    return x.reshape(g, t, nq, RW_HS, LANES)


def _to_lanes_i(x, g, bl, il):
    _, t, _ = x.shape
    rows = RW_HS // il
    x = x.reshape(g, bl, t, RW_HEADS, rows, il).transpose(0, 2, 4, 1, 3, 5)
    return x.reshape(g, t, rows, LANES)


def _from_lanes_i(y, bl, il):
    g, t, rows, _ = y.shape
    y = y.reshape(g, t, rows, bl, RW_HEADS, il).transpose(0, 3, 1, 4, 2, 5)
    return y.reshape(g * bl, t, W_BRANCH)


def _state_to_lanes(s, g, bl, il):
    rows = RW_HS // il
    s = s.reshape(g, bl, RW_HEADS, rows, il, RW_HS).transpose(0, 5, 3, 1, 2, 4)
    return s.reshape(g, RW_HS, rows, LANES)


def _state_from_lanes(s, bl, il):
    g, _, rows, _ = s.shape
    s = s.reshape(g, RW_HS, rows, bl, RW_HEADS, il).transpose(0, 3, 4, 2, 5, 1)
    return s.reshape(g * bl, RW_HEADS, RW_HS, RW_HS)


def _s5_body(u_ref, xre_ref, xim_ref, bmat_ref, cre_ref, cim_ref, pre_ref, pim_ref, cyr_ref, cyi_ref,
             d_ref, wglu_ref, bglu_ref, od_ref, hre_out, him_out, hre_s, him_s, car_re, car_im, *, bb, tt):
    @pl.when(pl.program_id(1) == 0)
    def _():
        car_re[...] = xre_ref[...]
        car_im[...] = xim_ref[...]

    w = W_BRANCH
    groups = tt // SUBLANES
    uf = u_ref[...].reshape(bb * tt, w)
    bu = _mm(uf.astype(BF16), bmat_ref[...])
    xr = bu[:, :S5_W].reshape(bb, tt, S5_W)
    xi = bu[:, S5_W:].reshape(bb, tt, S5_W)
    for n, s in enumerate((1, 2, 4)):
        cr = _tile_rows(pre_ref[n], groups)
        ci = _tile_rows(pim_ref[n], groups)
        sr = pltpu.roll(xr, s, axis=1)
        si = pltpu.roll(xi, s, axis=1)
        xr, xi = xr + (cr * sr - ci * si), xi + (cr * si + ci * sr)
    cyr = cyr_ref[...]
    cyi = cyi_ref[...]
    hr = car_re[...]
    hi = car_im[...]
    for g in range(groups):
        sl = slice(g * SUBLANES, (g + 1) * SUBLANES)
        gr = xr[:, sl] + (cyr * hr - cyi * hi)
        gi = xi[:, sl] + (cyr * hi + cyi * hr)
        hre_s[:, sl, :] = gr
        him_s[:, sl, :] = gi
        hr = gr[:, SUBLANES - 1:SUBLANES]
        hi = gi[:, SUBLANES - 1:SUBLANES]
    car_re[...] = hr
    car_im[...] = hi
    hre_out[...] = hr
    him_out[...] = hi
    h_re = hre_s[...].reshape(bb * tt, S5_W).astype(BF16)
    h_im = him_s[...].reshape(bb * tt, S5_W).astype(BF16)
    y = _mm(h_re, cre_ref[...]) - _mm(h_im, cim_ref[...])
    y = y + d_ref[...] * uf
    z = 0.5 * y * (1.0 + jnp.tanh(math.sqrt(2.0 / math.pi) * (y + 0.044715 * (y * y * y))))
    out = z * _sigmoid(_mm(z.astype(BF16), wglu_ref[...]) + bglu_ref[...])
    od_ref[...] = out.reshape(bb, tt, w)


def _s5(u, xre, xim, bmat, cre, cim, pre, pim, cyr, cyi, d, wglu, bglu, bb, tt):
    b, t, w = u.shape
    seq = lambda i, j: (i, j, 0)
    per_b = lambda i, j: (i, 0, 0)
    c2 = lambda i, j: (0, 0)
    c3 = lambda i, j: (0, 0, 0)
    return pl.pallas_call(
        functools.partial(_s5_body, bb=bb, tt=tt),
        grid=(b // bb, t // tt),
        in_specs=[pl.BlockSpec((bb, tt, w), seq),
                  pl.BlockSpec((bb, 1, S5_W), per_b),
                  pl.BlockSpec((bb, 1, S5_W), per_b),
                  pl.BlockSpec(bmat.shape, c2),
                  pl.BlockSpec(cre.shape, c2),
                  pl.BlockSpec(cim.shape, c2),
                  pl.BlockSpec(pre.shape, c3),
                  pl.BlockSpec(pim.shape, c3),
                  pl.BlockSpec(cyr.shape, c2),
                  pl.BlockSpec(cyi.shape, c2),
                  pl.BlockSpec((1, w), c2),
                  pl.BlockSpec((w, w), c2),
                  pl.BlockSpec((1, w), c2)],
        out_specs=[pl.BlockSpec((bb, tt, w), seq),
                   pl.BlockSpec((bb, 1, S5_W), per_b),
                   pl.BlockSpec((bb, 1, S5_W), per_b)],
        out_shape=[jax.ShapeDtypeStruct((b, t, w), F32),
                   jax.ShapeDtypeStruct((b, 1, S5_W), F32),
                   jax.ShapeDtypeStruct((b, 1, S5_W), F32)],
        scratch_shapes=[pltpu.VMEM((bb, tt, S5_W), F32), pltpu.VMEM((bb, tt, S5_W), F32),
                        pltpu.VMEM((bb, 1, S5_W), F32), pltpu.VMEM((bb, 1, S5_W), F32)],
        compiler_params=_cparams("parallel", "arbitrary"),
        name="s5",
    )(u, xre, xim, bmat, cre, cim, pre, pim, cyr, cyi, d, wglu, bglu)


def _s5_params(lre, lim, logdt, b_re, b_im, c_re, c_im):
    dt = jnp.exp(logdt)[:, None]
    mag = jnp.exp(lre * dt)
    ab_re, ab_im = mag * jnp.cos(lim * dt), mag * jnp.sin(lim * dt)
    den = lre * lre + lim * lim
    pr = ab_re - 1.0
    f_re = (pr * lre + ab_im * lim) / den
    f_im = (ab_im * lre - pr * lim) / den
    bb_re = f_re[..., None] * b_re - f_im[..., None] * b_im
    bb_im = f_re[..., None] * b_im + f_im[..., None] * b_re
    eye = jnp.eye(S5_GROUPS, dtype=F32)
    in_re = jnp.einsum('gnc,gh->gchn', bb_re, eye).reshape(W_BRANCH, S5_W)
    in_im = jnp.einsum('gnc,gh->gchn', bb_im, eye).reshape(W_BRANCH, S5_W)
    bmat = jnp.concatenate([in_re, in_im], axis=1).astype(BF16)
    cre = jnp.einsum('gcn,gh->gnhc', c_re, eye).reshape(S5_W, W_BRANCH).astype(BF16)
    cim = jnp.einsum('gcn,gh->gnhc', c_im, eye).reshape(S5_W, W_BRANCH).astype(BF16)
    ar, ai = ab_re.reshape(1, S5_W), ab_im.reshape(1, S5_W)
    pows = [(ar, ai)]
    for _ in range(SUBLANES - 1):
        qr, qi = pows[-1]
        pows.append((qr * ar - qi * ai, qr * ai + qi * ar))
    rows = jnp.arange(SUBLANES)[:, None]
    pre = jnp.stack([jnp.where(rows >= s, pows[s - 1][0], 0.0) for s in (1, 2, 4)])
    pim = jnp.stack([jnp.where(rows >= s, pows[s - 1][1], 0.0) for s in (1, 2, 4)])
    cyr = jnp.concatenate([q[0] for q in pows], axis=0)
    cyi = jnp.concatenate([q[1] for q in pows], axis=0)
    return bmat, cre, cim, pre, pim, cyr, cyi


def _outproj_body(x_ref, oa_ref, ob_ref, yc_ref, bonus_ref, od_ref, gate_ref,
                  subln_ref, gnw_ref, gnb_ref, ones_ref, wout_ref, npost_ref, y_ref, *, attn_scale):
    w = W_BRANCH
    ones_blk = ones_ref[...]
    oa = oa_ref[...]
    oa = oa * lax.rsqrt(_segsum(oa * oa, ones_blk) * (1.0 / A_V) + NORM_EPS) * subln_ref[...] * attn_scale
    yc = yc_ref[...]
    mean = _segsum(yc, ones_blk) * (1.0 / RW_HS)
    cen = yc - mean
    var = _segsum(cen * cen, ones_blk) * (1.0 / RW_HS)
    oc = cen * lax.rsqrt(var + RW_GN_EPS) * gnw_ref[...] + gnb_ref[...] + bonus_ref[...]
    gate = gate_ref[...]
    z = None
    for n, branch in enumerate((oa, ob_ref[...], oc, od_ref[...])):
        g = gate[:, n * w:(n + 1) * w]
        part = _mm((branch * (g * _sigmoid(g))).astype(BF16), wout_ref[n * w:(n + 1) * w, :])
        z = part if z is None else z + part
    zn = z * lax.rsqrt(jnp.mean(z * z, axis=-1, keepdims=True) + NORM_EPS) * npost_ref[...]
    y_ref[...] = x_ref[...] + zn


def _outproj(x2, oa, ob, yc, bonus, od, gate, subln, gnw, gnb, ones_blk, wout_bf, npost, attn_scale, tm):
    n = x2.shape[0]
    w = W_BRANCH
    row = lambda i: (i, 0)
    const = lambda i: (0, 0)
    rw = pl.BlockSpec((tm, w), row)
    rd = pl.BlockSpec((tm, D_MODEL), row)
    vec = pl.BlockSpec((1, w), const)
    return pl.pallas_call(
        functools.partial(_outproj_body, attn_scale=attn_scale),
        grid=(n // tm,),
        in_specs=[rd, rw, rw, rw, rw, rw, rd, vec, vec, vec,
                  pl.BlockSpec((w, w), const),
                  pl.BlockSpec((D_MODEL, D_MODEL), const),
                  pl.BlockSpec((1, D_MODEL), const)],
        out_specs=rd,
        out_shape=jax.ShapeDtypeStruct((n, D_MODEL), F32),
        compiler_params=_cparams("parallel"),
        name="outproj",
    )(x2, oa, ob, yc, bonus, od, gate, subln, gnw, gnb, ones_blk, wout_bf, npost)


def _block_diag(blocks):
    n, k, _ = blocks.shape
    eye = jnp.eye(n, dtype=blocks.dtype)
    return jnp.einsum('nij,nm->nimj', blocks, eye).reshape(n * k, n * k)


def _layer(x, attend, states, lam_init, wts, tiles):
    (norm_pre, norm_post, w_in, w_out, lam_q1, lam_k1, lam_q2, lam_k2, subln_w, conv_w, conv_b,
     lru_wa, lru_ba, lru_wx, lru_bx, lru_lam,
     rw_mu, rw_w0, rw_w1, rw_w2, rw_a0, rw_a1, rw_a2, rw_kk, rw_ka, rw_rk, rw_gnw, rw_gnb,
     s5_lre, s5_lim, s5_logdt, s5_bre, s5_bim, s5_cre, s5_cim, s5_d, s5_wglu, s5_bglu) = wts
    conv_buf, lru_h, shift_prev, wkv_s, ssm_x = states
    b, t, _ = x.shape
    n = b * t
    w = W_BRANCH
    tm, bb, tt, tt_scan = tiles
    row = lambda v: v.reshape(1, -1)

    x2 = x.reshape(n, D_MODEL)
    q, k, v, xb, pc, u, gate = _inproj(x2, row(norm_pre), w_in.astype(BF16), tm)

    lam = (jnp.exp(jnp.sum(lam_q1 * lam_k1)) - jnp.exp(jnp.sum(lam_q2 * lam_k2)) + lam_init).reshape(1)
    oa = attend(lam, q.reshape(b, t, w), k.reshape(b, t, w), v.reshape(b, t, w))

    cst8 = jnp.pad(conv_buf, ((0, 0), (SUBLANES - (CONV_W - 1), 0), (0, 0)))
    wg = jnp.concatenate([_block_diag(lru_wa), _block_diag(lru_wx)], axis=1).astype(BF16)
    bg = jnp.concatenate([lru_ba, lru_bx]).reshape(1, 2 * w)
    ob, conv8, new_h = _mixb(xb.reshape(b, t, w), cst8, lru_h.reshape(b, 1, w), conv_w, row(conv_b),
                             wg, bg, row(lru_lam), bb, tt)
    new_conv = conv8[:, SUBLANES - (CONV_W - 1):]

    ones_blk = _block_diag(jnp.ones((RW_HEADS, RW_HS, RW_HS), BF16))
    jv, rv, bonus, new_shift = _rwkv_pre(
        pc.reshape(b, t, RW_NPROJ * w), shift_prev.reshape(b, 1, RW_NPROJ * w), row(rw_mu), row(rw_w0),
        rw_w1.astype(BF16), rw_w2.astype(BF16), row(rw_a0), rw_a1.astype(BF16), rw_a2.astype(BF16),
        row(rw_kk), row(rw_ka), row(rw_rk), ones_blk, bb, tt)
    g, bl, il = _lane_split(b)
    yl, sl = _rwkv_seq(_to_lanes_j(jv, g, bl, il), _to_lanes_i(rv, g, bl, il),
                       _state_to_lanes(wkv_s, g, bl, il), tt_scan)
    yc = _from_lanes_i(yl, bl, il)
    new_s = _state_from_lanes(sl, bl, il)

    bmat, cre, cim, pre, pim, cyr, cyi = _s5_params(s5_lre, s5_lim, s5_logdt, s5_bre, s5_bim, s5_cre, s5_cim)
    od, hre, him = _s5(u.reshape(b, t, w), ssm_x[..., 0].reshape(b, 1, S5_W), ssm_x[..., 1].reshape(b, 1, S5_W),
                       bmat, cre, cim, pre, pim, cyr, cyi, row(s5_d), s5_wglu.astype(BF16), row(s5_bglu), bb, tt)
    new_ssm = jnp.stack([hre.reshape(b, S5_GROUPS, S5_STATE), him.reshape(b, S5_GROUPS, S5_STATE)], axis=-1)

    y = _outproj(x2, oa.reshape(n, w), ob.reshape(n, w), yc.reshape(n, w), bonus.reshape(n, w),
                 od.reshape(n, w), gate, row(jnp.tile(subln_w, A_HEADS)), row(rw_gnw), row(rw_gnb),
                 ones_blk, w_out.astype(BF16), row(norm_post), 1.0 - lam_init, tm)
    new_states = (k.reshape(b, t, A_HEADS, A_V), v.reshape(b, t, A_HEADS, A_V), new_conv,
                  new_h.reshape(b, w), new_shift.reshape(b, RW_NPROJ * w), new_s, new_ssm)
    return y.reshape(b, t, D_MODEL), new_states


def kernel(x_prompt, x_sample, cache_k, cache_v, page_table, state_conv, state_lru, state_shift, state_wkv, state_ssm, norm_pre, norm_post, w_in, w_out, lam_q1, lam_k1, lam_q2, lam_k2, subln_w, conv_w, conv_b, lru_wa, lru_ba, lru_wx, lru_bx, lru_lam, rw_mu, rw_w0, rw_w1, rw_w2, rw_a0, rw_a1, rw_a2, rw_kk, rw_ka, rw_rk, rw_gnw, rw_gnb, s5_lre, s5_lim, s5_logdt, s5_bre, s5_bim, s5_cre, s5_cim, s5_d, s5_wglu, s5_bglu):
    weights = (norm_pre, norm_post, w_in, w_out, lam_q1, lam_k1, lam_q2, lam_k2, subln_w,
               conv_w, conv_b, lru_wa, lru_ba, lru_wx, lru_bx, lru_lam,
               rw_mu, rw_w0, rw_w1, rw_w2, rw_a0, rw_a1, rw_a2, rw_kk, rw_ka, rw_rk, rw_gnw, rw_gnb,
               s5_lre, s5_lim, s5_logdt, s5_bre, s5_bim, s5_cre, s5_cim, s5_d, s5_wglu, s5_bglu)
    depth = w_in.shape[0]
    nbp, tp, _ = x_prompt.shape
    nbs, ts, _ = x_sample.shape
    dt = x_prompt.dtype
    w = W_BRANCH
    zero_states = (jnp.zeros((nbp, CONV_W - 1, w), dt), jnp.zeros((nbp, w), dt),
                   jnp.zeros((nbp, RW_NPROJ * w), dt), jnp.zeros((nbp, RW_HEADS, RW_HS, RW_HS), dt),
                   jnp.zeros((nbp, S5_GROUPS, S5_STATE, 2), dt))
    tiles_p = (min(512, nbp * tp), 1, min(256, tp), min(64, tp))
    bbs = min(16, nbs)
    tiles_s = (min(512, nbs * ts), bbs, ts, ts)
    tq = min(256, tp)
    xp, xs = x_prompt, x_sample
    outs_p = [[] for _ in range(7)]
    outs_s = [[] for _ in range(7)]
    for l in range(depth):
        wl = tuple(wt[l] for wt in weights)
        lam_init = 0.8 - 0.6 * math.exp(-0.3 * l)
        xp, st_p = _layer(xp, functools.partial(_attn_prompt, tq=tq), zero_states, lam_init, wl, tiles_p)
        attend_s = functools.partial(_attn_sample, cache_k_l=cache_k[l], cache_v_l=cache_v[l],
                                     page_table=page_table)
        st_in = (state_conv[l], state_lru[l], state_shift[l], state_wkv[l], state_ssm[l])
        xs, st_s = _layer(xs, attend_s, st_in, lam_init, wl, tiles_s)
        for n in range(7):
            outs_p[n].append(st_p[n])
            outs_s[n].append(st_s[n])
    k_p, v_p, conv_p, lru_p, shift_p, wkv_p, ssm_p = [jnp.stack(z) for z in outs_p]
    k_s, v_s, conv_s, lru_s, shift_s, wkv_s, ssm_s = [jnp.stack(z) for z in outs_s]
    return (xp, xs, k_p, k_s, v_p, v_s, conv_p, conv_s, lru_p, lru_s,
            shift_p, shift_s, wkv_p, wkv_s, ssm_p, ssm_s)
```

```python
import functools
import math

import jax
import jax.numpy as jnp
from jax import lax
from jax.experimental import pallas as pl
from jax.experimental.pallas import tpu as pltpu

F32 = jnp.float32
BF16 = jnp.bfloat16

D_MODEL = 1024
W_BRANCH = 256
A_HEADS = 4
A_QK = 32
A_V = 64
PAGE_SIZE = 128
LRU_BLOCKS = 4
LRU_BW = 64
CONV_W = 4
LRU_C = 8.0
RW_HEADS = 4
RW_HS = 64
RW_NPROJ = 5
RW_GN_EPS = RW_HS * 1e-5
S5_CH = 16
S5_GROUPS = 16
S5_STATE = 64
S5_W = S5_GROUPS * S5_STATE
NORM_EPS = 1e-6
NEG_BIG = -1e30

OFF_AQ = 0
OFF_AK = OFF_AQ + W_BRANCH
OFF_AV = OFF_AK + W_BRANCH
OFF_B = OFF_AV + W_BRANCH
OFF_C = OFF_B + W_BRANCH
OFF_D = OFF_C + RW_NPROJ * W_BRANCH
OFF_G = OFF_D + W_BRANCH
D_IN = OFF_G + D_MODEL
_SEGS = ((OFF_AQ, OFF_AK), (OFF_AK, OFF_AV), (OFF_AV, OFF_B), (OFF_B, OFF_C),
         (OFF_C, OFF_D), (OFF_D, OFF_G), (OFF_G, D_IN))

SUBLANES = 8
LANES = 128
VMEM_LIMIT_MIB = 56


def _cparams(*sem):
    return pltpu.CompilerParams(dimension_semantics=sem,
                                vmem_limit_bytes=VMEM_LIMIT_MIB * 1024 * 1024)


def _nt(a, b):
    return lax.dot_general(a, b, (((1,), (1,)), ((), ())), preferred_element_type=F32)


def _mm(a, b):
    return jnp.dot(a, b, preferred_element_type=F32)


def _segsum(x, ones_blk):
    hi = x.astype(BF16)
    lo = (x - hi.astype(F32)).astype(BF16)
    return _mm(hi, ones_blk) + _mm(lo, ones_blk)


def _sigmoid(x):
    return jax.nn.sigmoid(x)


def _softplus(x):
    return jnp.maximum(x, 0.0) + jnp.log1p(jnp.exp(-jnp.abs(x)))


def _inproj_body(x_ref, g_ref, w_ref, *out_refs):
    x = x_ref[...]
    h = x * lax.rsqrt(jnp.mean(x * x, axis=-1, keepdims=True) + NORM_EPS) * g_ref[...]
    hb = h.astype(BF16)
    for (lo, hi), o_ref in zip(_SEGS, out_refs):
        o_ref[...] = _mm(hb, w_ref[:, lo:hi])


def _inproj(x2, g, w_bf, tm):
    n = x2.shape[0]
    row = lambda i: (i, 0)
    const = lambda i: (0, 0)
    return pl.pallas_call(
        _inproj_body,
        grid=(n // tm,),
        in_specs=[pl.BlockSpec((tm, D_MODEL), row),
                  pl.BlockSpec((1, D_MODEL), const),
                  pl.BlockSpec((D_MODEL, D_IN), const)],
        out_specs=[pl.BlockSpec((tm, hi - lo), row) for lo, hi in _SEGS],
        out_shape=[jax.ShapeDtypeStruct((n, hi - lo), F32) for lo, hi in _SEGS],
        compiler_params=_cparams("parallel"),
        name="inproj",
    )(x2, g, w_bf)


def _attn_prompt_body(lam_ref, q_ref, k_ref, v_ref, o_ref, kb_ref, vm_ref, qm_ref, acc_ref, m_ref, l_ref, *, tq):
    qi = pl.program_id(1)
    nhc = 2 * A_HEADS
    lane = lax.broadcasted_iota(jnp.int32, (1, W_BRANCH), 1)

    @pl.when(qi == 0)
    def _():
        kb_ref[...] = k_ref[0].astype(BF16)
        v = v_ref[0]
        for h in range(A_HEADS):
            vm_ref[h] = jnp.where(lane // A_V == h, v, 0.0).astype(BF16)

    q = q_ref[0] * (A_QK ** -0.5)
    for hc in range(nhc):
        qm_ref[hc * tq:(hc + 1) * tq, :] = jnp.where(lane // A_QK == hc, q, 0.0).astype(BF16)
    m_ref[...] = jnp.full(m_ref.shape, NEG_BIG, F32)
    l_ref[...] = jnp.zeros(l_ref.shape, F32)
    acc_ref[...] = jnp.zeros(acc_ref.shape, F32)

    def wide(x):
        return jnp.concatenate([x, x], axis=1)

    def block(kb, diagonal):
        ks = pl.multiple_of(kb * tq, tq)
        s = _nt(qm_ref[...], kb_ref[pl.ds(ks, tq), :])
        if diagonal:
            r = lax.broadcasted_iota(jnp.int32, (nhc * tq, 1), 0) % tq
            c = lax.broadcasted_iota(jnp.int32, (1, tq), 1)
            s = jnp.where(c <= r, s, NEG_BIG)
        m_old = m_ref[...]
        m_new = jnp.maximum(m_old, jnp.max(s, axis=-1, keepdims=True))
        alpha = jnp.exp(m_old - m_new)
        p = jnp.exp(s - wide(m_new))
        l_ref[...] = alpha * l_ref[...] + jnp.sum(p, axis=-1, keepdims=True)
        m_ref[...] = m_new
        pb = p.astype(BF16)
        a2 = wide(alpha)
        for h in range(A_HEADS):
            rows = slice(2 * h * tq, (2 * h + 2) * tq)
            acc_ref[rows, :] = a2[rows] * acc_ref[rows, :] + _mm(pb[rows], vm_ref[h, pl.ds(ks, tq), :])

    def off_diagonal(kb, carry):
        block(kb, False)
        return carry

    lax.fori_loop(0, qi, off_diagonal, 0)
    block(qi, True)
    lam = lam_ref[0]
    inv_l = wide(1.0 / l_ref[...])
    out = jnp.zeros((tq, W_BRANCH), F32)
    for h in range(A_HEADS):
        r0 = slice(2 * h * tq, (2 * h + 1) * tq)
        r1 = slice((2 * h + 1) * tq, (2 * h + 2) * tq)
        out = out + (acc_ref[r0, :] * inv_l[r0] - lam * (acc_ref[r1, :] * inv_l[r1]))
    o_ref[0] = out


def _attn_prompt(lam, q, k, v, tq):
    b, t, _ = q.shape
    nhc = 2 * A_HEADS
    return pl.pallas_call(
        functools.partial(_attn_prompt_body, tq=tq),
        grid=(b, t // tq),
        in_specs=[pl.BlockSpec(memory_space=pltpu.SMEM),
                  pl.BlockSpec((1, tq, W_BRANCH), lambda i, j: (i, j, 0)),
                  pl.BlockSpec((1, t, W_BRANCH), lambda i, j: (i, 0, 0)),
                  pl.BlockSpec((1, t, W_BRANCH), lambda i, j: (i, 0, 0))],
        out_specs=pl.BlockSpec((1, tq, W_BRANCH), lambda i, j: (i, j, 0)),
        out_shape=jax.ShapeDtypeStruct((b, t, W_BRANCH), F32),
        scratch_shapes=[pltpu.VMEM((t, W_BRANCH), BF16),
                        pltpu.VMEM((A_HEADS, t, W_BRANCH), BF16),
                        pltpu.VMEM((nhc * tq, W_BRANCH), BF16),
                        pltpu.VMEM((nhc * tq, W_BRANCH), F32),
                        pltpu.VMEM((nhc * tq, LANES), F32),
                        pltpu.VMEM((nhc * tq, LANES), F32)],
        compiler_params=_cparams("parallel", "arbitrary"),
        name="attn_prompt",
    )(lam, q, k, v)


def _attn_sample_body(pt_ref, lam_ref, q_ref, kn_ref, vn_ref, *rest, n_pages, tq):
    del pt_ref
    kp_refs = rest[:n_pages]
    vp_refs = rest[n_pages:2 * n_pages]
    o_ref = rest[2 * n_pages]
    nhc = 2 * A_HEADS
    lane = lax.broadcasted_iota(jnp.int32, (1, W_BRANCH), 1)
    rowhc = lax.broadcasted_iota(jnp.int32, (nhc * tq, 1), 0) // tq
    q = q_ref[0] * (A_QK ** -0.5)
    qbd = jnp.where(lane // A_QK == rowhc, jnp.tile(q, (nhc, 1)), 0.0).astype(BF16)
    s_past = jnp.concatenate([_nt(qbd, kp_refs[j][0].astype(BF16)) for j in range(n_pages)], axis=1)
    s_new = _nt(qbd, kn_ref[0].astype(BF16))
    tcol = lax.broadcasted_iota(jnp.int32, (nhc * tq, tq), 1)
    trow = lax.broadcasted_iota(jnp.int32, (nhc * tq, tq), 0) % tq
    s_new = jnp.where(tcol <= trow, s_new, NEG_BIG)
    m = jnp.maximum(jnp.max(s_past, axis=-1, keepdims=True), jnp.max(s_new, axis=-1, keepdims=True))
    p_past = jnp.exp(s_past - m)
    p_new = jnp.exp(s_new - m)
    inv = 1.0 / (jnp.sum(p_past, axis=-1, keepdims=True) + jnp.sum(p_new, axis=-1, keepdims=True))
    p_past = p_past * inv
    p_new = p_new * inv
    lam = lam_ref[0]

    def diff(p):
        return jnp.concatenate(
            [p[(2 * h) * tq:(2 * h + 1) * tq] - lam * p[(2 * h + 1) * tq:(2 * h + 2) * tq]
             for h in range(A_HEADS)], axis=0)

    w_past = diff(p_past).astype(BF16)
    w_new = diff(p_new).astype(BF16)
    out_all = _mm(w_new, vn_ref[0].astype(BF16))
    for j in range(n_pages):
        out_all = out_all + _mm(w_past[:, j * PAGE_SIZE:(j + 1) * PAGE_SIZE], vp_refs[j][0].astype(BF16))
    out = jnp.zeros((tq, W_BRANCH), F32)
    for h in range(A_HEADS):
        out = out + jnp.where(lane // A_V == h, out_all[h * tq:(h + 1) * tq], 0.0)
    o_ref[0] = out


def _attn_sample(lam, q, k, v, cache_k_l, cache_v_l, page_table):
    b, tq, _ = q.shape
    n_pages = page_table.shape[1]
    pt = page_table.reshape(-1)
    ck = cache_k_l.reshape(cache_k_l.shape[0], PAGE_SIZE, W_BRANCH)
    cv = cache_v_l.reshape(cache_v_l.shape[0], PAGE_SIZE, W_BRANCH)
    seq = pl.BlockSpec((1, tq, W_BRANCH), lambda i, pt_r, lam_r: (i, 0, 0))

    def page_spec(j):
        return pl.BlockSpec((1, PAGE_SIZE, W_BRANCH),
                            lambda i, pt_r, lam_r: (pt_r[i * n_pages + j], 0, 0))

    grid_spec = pltpu.PrefetchScalarGridSpec(
        num_scalar_prefetch=2,
        grid=(b,),
        in_specs=[seq, seq, seq] + [page_spec(j) for j in range(n_pages)] * 2,
        out_specs=seq,
    )
    return pl.pallas_call(
        functools.partial(_attn_sample_body, n_pages=n_pages, tq=tq),
        grid_spec=grid_spec,
        out_shape=jax.ShapeDtypeStruct((b, tq, W_BRANCH), F32),
        compiler_params=_cparams("parallel"),
        name="attn_sample",
    )(pt, lam, q, k, v, *([ck] * n_pages), *([cv] * n_pages))


def _tile_rows(c8, groups):
    return jnp.tile(c8, (groups, 1)) if groups > 1 else c8


def _shift_in(prev8, x, s):
    tt = x.shape[1]
    ext = jnp.concatenate([prev8, x], axis=1)
    return pltpu.roll(ext, s, axis=1)[:, SUBLANES:SUBLANES + tt]


def _mixb_body(xb_ref, cst_ref, h0_ref, cw_ref, cb_ref, wg_ref, bg_ref, lam_ref,
               ob_ref, cout_ref, hout_ref, prev_ref, hcar_ref, *, bb, tt):
    @pl.when(pl.program_id(1) == 0)
    def _():
        prev_ref[...] = cst_ref[...]
        hcar_ref[...] = h0_ref[...]

    w = W_BRANCH
    x = xb_ref[...]
    prev8 = prev_ref[...]
    cw = cw_ref[...]
    xc = cb_ref[...] + _shift_in(prev8, x, 3) * cw[0:1]
    xc = xc + _shift_in(prev8, x, 2) * cw[1:2]
    xc = xc + _shift_in(prev8, x, 1) * cw[2:3]
    xc = xc + x * cw[3:4]
    tail = x[:, tt - SUBLANES:tt]
    prev_ref[...] = tail
    cout_ref[...] = tail

    xf = xc.reshape(bb * tt, w)
    gates = _mm(xf.astype(BF16), wg_ref[...]) + bg_ref[...]
    r = _sigmoid(gates[:, :w])
    i = _sigmoid(gates[:, w:])
    log_a = -LRU_C * r * _softplus(-lam_ref[...])
    a = jnp.exp(log_a).reshape(bb, tt, w)
    b = (jnp.sqrt(1.0 - jnp.exp(2.0 * log_a)) * (i * xf)).reshape(bb, tt, w)

    rmod = lax.broadcasted_iota(jnp.int32, (1, tt, 1), 1) % SUBLANES
    for s in (1, 2, 4):
        keep = rmod >= s
        b = b + a * jnp.where(keep, pltpu.roll(b, s, axis=1), 0.0)
        a = a * jnp.where(keep, pltpu.roll(a, s, axis=1), 1.0)
    h = hcar_ref[...]
    for g in range(tt // SUBLANES):
        sl = slice(g * SUBLANES, (g + 1) * SUBLANES)
        hg = b[:, sl] + a[:, sl] * h
        ob_ref[:, sl, :] = hg
        h = hg[:, SUBLANES - 1:SUBLANES]
    hcar_ref[...] = h
    hout_ref[...] = h


def _mixb(xb, cst8, h0, cw, cb, wg, bg, lru_lam, bb, tt):
    b, t, w = xb.shape
    seq = lambda i, j: (i, j, 0)
    per_b = lambda i, j: (i, 0, 0)
    const = lambda i, j: (0, 0)
    return pl.pallas_call(
        functools.partial(_mixb_body, bb=bb, tt=tt),
        grid=(b // bb, t // tt),
        in_specs=[pl.BlockSpec((bb, tt, w), seq),
                  pl.BlockSpec((bb, SUBLANES, w), per_b),
                  pl.BlockSpec((bb, 1, w), per_b),
                  pl.BlockSpec((CONV_W, w), const),
                  pl.BlockSpec((1, w), const),
                  pl.BlockSpec((w, 2 * w), const),
                  pl.BlockSpec((1, 2 * w), const),
                  pl.BlockSpec((1, w), const)],
        out_specs=[pl.BlockSpec((bb, tt, w), seq),
                   pl.BlockSpec((bb, SUBLANES, w), per_b),
                   pl.BlockSpec((bb, 1, w), per_b)],
        out_shape=[jax.ShapeDtypeStruct((b, t, w), F32),
                   jax.ShapeDtypeStruct((b, SUBLANES, w), F32),
                   jax.ShapeDtypeStruct((b, 1, w), F32)],
        scratch_shapes=[pltpu.VMEM((bb, SUBLANES, w), F32), pltpu.VMEM((bb, 1, w), F32)],
        compiler_params=_cparams("parallel", "arbitrary"),
        name="conv_rglru",
    )(xb, cst8, h0, cw, cb, wg, bg, lru_lam)


def _rwkv_pre_body(pc_ref, sh0_ref, mu_ref, w0_ref, w1_ref, w2_ref, a0_ref, a1_ref, a2_ref,
                   kkw_ref, ka_ref, rk_ref, ones_ref,
                   jv_ref, v_ref, bonus_ref, shout_ref, prev_ref, *, bb, tt):
    @pl.when(pl.program_id(1) == 0)
    def _():
        prev_ref[...] = jnp.broadcast_to(sh0_ref[...], prev_ref.shape)

    w = W_BRANCH
    p = pc_ref[...]
    p_prev = _shift_in(prev_ref[...], p, 1)
    tail = p[:, tt - SUBLANES:tt]
    prev_ref[...] = tail
    shout_ref[...] = p[:, tt - 1:tt]
    xm = (p + (p_prev - p) * mu_ref[...]).reshape(bb * tt, RW_NPROJ * w)
    xr, xw, xk, xv, xa = (xm[:, n * w:(n + 1) * w] for n in range(RW_NPROJ))
    lw = _mm(jnp.tanh(_mm(xw.astype(BF16), w1_ref[...])).astype(BF16), w2_ref[...])
    wlog = -_softplus(-(w0_ref[...] + lw)) - 0.5
    decay = jnp.exp(-jnp.exp(wlog))
    a = _sigmoid(a0_ref[...] + _mm(_mm(xa.astype(BF16), a1_ref[...]).astype(BF16), a2_ref[...]))
    kk = xk * kkw_ref[...]
    ones_blk = ones_ref[...]
    kk = kk / jnp.maximum(jnp.sqrt(_segsum(kk * kk, ones_blk)), 1e-12)
    k = xk * (1.0 + (a - 1.0) * ka_ref[...])
    bonus = _segsum(xr * k * rk_ref[...], ones_blk) * xv
    shp = (bb, tt, w)
    jv_ref[:, :, 0 * w:1 * w] = decay.reshape(shp)
    jv_ref[:, :, 1 * w:2 * w] = (-kk).reshape(shp)
    jv_ref[:, :, 2 * w:3 * w] = (kk * a).reshape(shp)
    jv_ref[:, :, 3 * w:4 * w] = k.reshape(shp)
    jv_ref[:, :, 4 * w:5 * w] = xr.reshape(shp)
    v_ref[...] = xv.reshape(shp)
    bonus_ref[...] = bonus.reshape(shp)


def _rwkv_pre(pc, sh0, mu, w0, w1, w2, a0, a1, a2, kkw, ka, rk, ones_blk, bb, tt):
    b, t, wp = pc.shape
    w = W_BRANCH
    seq = lambda i, j: (i, j, 0)
    per_b = lambda i, j: (i, 0, 0)
    const = lambda i, j: (0, 0)
    vec = pl.BlockSpec((1, w), const)
    return pl.pallas_call(
        functools.partial(_rwkv_pre_body, bb=bb, tt=tt),
        grid=(b // bb, t // tt),
        in_specs=[pl.BlockSpec((bb, tt, wp), seq),
                  pl.BlockSpec((bb, 1, wp), per_b),
                  pl.BlockSpec((1, wp), const),
                  vec, pl.BlockSpec(w1.shape, const), pl.BlockSpec(w2.shape, const),
                  vec, pl.BlockSpec(a1.shape, const), pl.BlockSpec(a2.shape, const),
                  vec, vec, vec, pl.BlockSpec((w, w), const)],
        out_specs=[pl.BlockSpec((bb, tt, wp), seq),
                   pl.BlockSpec((bb, tt, w), seq),
                   pl.BlockSpec((bb, tt, w), seq),
                   pl.BlockSpec((bb, 1, wp), per_b)],
        out_shape=[jax.ShapeDtypeStruct((b, t, wp), F32),
                   jax.ShapeDtypeStruct((b, t, w), F32),
                   jax.ShapeDtypeStruct((b, t, w), F32),
                   jax.ShapeDtypeStruct((b, 1, wp), F32)],
        scratch_shapes=[pltpu.VMEM((bb, SUBLANES, wp), F32)],
        compiler_params=_cparams("parallel", "arbitrary"),
        name="rwkv_pre",
    )(pc, sh0, mu, w0, w1, w2, a0, a1, a2, kkw, ka, rk, ones_blk)


_JV_DECAY, _JV_NKK, _JV_KKA, _JV_K, _JV_R = range(5)
RW_NJV = 5
XT = LANES
RELAY_UNROLL = 8


def _tree_sum(parts):
    while len(parts) > 1:
        parts = [parts[n] + parts[n + 1] for n in range(0, len(parts), 2)]
    return parts[0]


def _rwkv_steps(zrow, zv_ref, zy_ref, s_ref, nsteps, rows, n_acc=4):
    def accumulate(parts, j, term):
        parts[j % n_acc] = term if parts[j % n_acc] is None else parts[j % n_acc] + term

    sa_parts = [None] * n_acc
    for j in range(RW_HS):
        accumulate(sa_parts, j, s_ref[j] * zrow(_JV_NKK, j, 0))

    def step(t, sa):
        r0 = pl.multiple_of(t * rows, rows)
        v = zv_ref[pl.ds(r0, rows), :]
        t_next = jnp.minimum(t + 1, nsteps - 1)
        y_parts = [None] * n_acc
        sa_parts = [None] * n_acc
        for j in range(RW_HS):
            s = s_ref[j] * zrow(_JV_DECAY, j, t) + sa * zrow(_JV_KKA, j, t) + v * zrow(_JV_K, j, t)
            s_ref[j] = s
            accumulate(y_parts, j, s * zrow(_JV_R, j, t))
            accumulate(sa_parts, j, s * zrow(_JV_NKK, j, t_next))
        zy_ref[pl.ds(r0, rows), :] = _tree_sum(y_parts)
        return _tree_sum(sa_parts)

    lax.fori_loop(0, nsteps, step, _tree_sum(sa_parts))


def _rwkv_scan_nat_body(jv_ref, v_ref, s0_ref, y_ref, s_ref, e_ref, z_ref, zv_ref, zy_ref, *, nb):
    reps = LANES // (nb * RW_HEADS)
    rows = RW_HS // reps
    nbh = nb * RW_HEADS

    @pl.when(pl.program_id(0) == 0)
    def _():
        s_ref[...] = s0_ref[...]

    halves = [(b, h0) for b in range(nb) for h0 in range(0, W_BRANCH, LANES)]

    def fill_e(src_ref, lane0):
        for b, h0 in halves:
            e_ref[b * W_BRANCH + h0:b * W_BRANCH + h0 + LANES, :] = src_ref[b, :, lane0 + h0:lane0 + h0 + LANES].T

    for q in range(RW_NJV):
        fill_e(jv_ref, q * W_BRANCH)

        def relay(jg, carry, q=q):
            for jj in range(RELAY_UNROLL):
                j = jg * RELAY_UNROLL + jj
                r = e_ref[pl.ds(j, nbh, stride=RW_HS), :]
                z_ref[q, j] = jnp.concatenate([r] * reps, axis=0).T
            return carry

        lax.fori_loop(0, RW_HS // RELAY_UNROLL, relay, 0)
    fill_e(v_ref, 0)
    for ih in range(rows):
        tile = jnp.concatenate([e_ref[pl.ds(rep * rows + ih, nbh, stride=RW_HS), :] for rep in range(reps)], axis=0)
        zv_ref[pl.ds(ih, XT, stride=rows), :] = tile.T

    _rwkv_steps(lambda which, j, t: z_ref[which, j, pl.ds(t, 1), :], zv_ref, zy_ref, s_ref, XT, rows)

    for ih in range(rows):
        tile = zy_ref[pl.ds(ih, XT, stride=rows), :].T
        for rep in range(reps):
            e_ref[pl.ds(rep * rows + ih, nbh, stride=RW_HS), :] = tile[rep * nbh:(rep + 1) * nbh]
    for b, h0 in halves:
        y_ref[b, :, h0:h0 + LANES] = e_ref[b * W_BRANCH + h0:b * W_BRANCH + h0 + LANES, :].T


def _rwkv_scan_nat(jv, v, s0):
    nb, t, wq = jv.shape
    rows = s0.shape[1]
    seq = lambda i: (0, i, 0)
    st = lambda i: (0, 0, 0)
    return pl.pallas_call(
        functools.partial(_rwkv_scan_nat_body, nb=nb),
        grid=(t // XT,),
        in_specs=[pl.BlockSpec((nb, XT, wq), seq),
                  pl.BlockSpec((nb, XT, W_BRANCH), seq),
                  pl.BlockSpec((RW_HS, rows, LANES), st)],
        out_specs=[pl.BlockSpec((nb, XT, W_BRANCH), seq),
                   pl.BlockSpec((RW_HS, rows, LANES), st)],
        out_shape=[jax.ShapeDtypeStruct((nb, t, W_BRANCH), F32),
                   jax.ShapeDtypeStruct((RW_HS, rows, LANES), F32)],
        scratch_shapes=[pltpu.VMEM((nb * W_BRANCH, XT), F32),
                        pltpu.VMEM((RW_NJV, RW_HS, XT, LANES), F32),
                        pltpu.VMEM((XT * rows, LANES), F32),
                        pltpu.VMEM((XT * rows, LANES), F32)],
        compiler_params=_cparams("arbitrary"),
        name="rwkv_scan",
    )(jv, v, s0)


def _rwkv_scan_lanes_body(z_ref, zv_ref, s0_ref, zy_ref, s_ref, *, nsteps, rows):
    s_ref[...] = s0_ref[...]
    _rwkv_steps(lambda which, j, t: z_ref[0, which, j, pl.ds(t, 1), :], zv_ref.at[0], zy_ref.at[0],
                s_ref.at[0], nsteps, rows)


def _rwkv_scan_lanes(z, zv, s0):
    g, nq, hs, t, _ = z.shape
    rows = s0.shape[2]
    return pl.pallas_call(
        functools.partial(_rwkv_scan_lanes_body, nsteps=t, rows=rows),
        grid=(g,),
        in_specs=[pl.BlockSpec((1, nq, hs, t, LANES), lambda i: (i, 0, 0, 0, 0)),
                  pl.BlockSpec((1, t * rows, LANES), lambda i: (i, 0, 0)),
                  pl.BlockSpec((1, hs, rows, LANES), lambda i: (i, 0, 0, 0))],
        out_specs=[pl.BlockSpec((1, t * rows, LANES), lambda i: (i, 0, 0)),
                   pl.BlockSpec((1, hs, rows, LANES), lambda i: (i, 0, 0, 0))],
        out_shape=[jax.ShapeDtypeStruct((g, t * rows, LANES), F32),
                   jax.ShapeDtypeStruct((g, hs, rows, LANES), F32)],
        compiler_params=_cparams("parallel"),
        name="rwkv_scan_lanes",
    )(z, zv, s0)


def _lane_split(b):
    seqs = min(b, LANES // RW_HEADS)
    return b // seqs, seqs, LANES // (RW_HEADS * seqs)


def _state_to_lanes(s, g, bl, reps):
    rows = RW_HS // reps
    s = s.reshape(g, bl, RW_HEADS, reps, rows, RW_HS).transpose(0, 5, 4, 3, 1, 2)
    return s.reshape(g, RW_HS, rows, LANES)


def _state_from_lanes(s, bl, reps):
    g, _, rows, _ = s.shape
    s = s.reshape(g, RW_HS, rows, reps, bl, RW_HEADS).transpose(0, 4, 5, 3, 2, 1)
    return s.reshape(g * bl, RW_HEADS, RW_HS, RW_HS)


def _rwkv_scan(jv, v, wkv_s):
    b, t, _ = jv.shape
    g, bl, reps = _lane_split(b)
    s0 = _state_to_lanes(wkv_s, g, bl, reps)
    if g == 1 and t % XT == 0:
        y, s = _rwkv_scan_nat(jv, v, s0[0])
        return y, _state_from_lanes(s[None], bl, reps)
    rows = RW_HS // reps
    z = jv.reshape(g, bl, t, RW_NJV, RW_HEADS, RW_HS).transpose(0, 3, 5, 2, 1, 4)
    z = jnp.broadcast_to(z[:, :, :, :, None], (g, RW_NJV, RW_HS, t, reps, bl, RW_HEADS)).reshape(g, RW_NJV, RW_HS, t, LANES)
    zv = v.reshape(g, bl, t, RW_HEADS, reps, rows).transpose(0, 2, 5, 4, 1, 3).reshape(g, t * rows, LANES)
    zy, s = _rwkv_scan_lanes(z, zv, s0)
    y = zy.reshape(g, t, rows, reps, bl, RW_HEADS).transpose(0, 4, 1, 5, 3, 2).reshape(b, t, W_BRANCH)
    return y, _state_from_lanes(s, bl, reps)


def _s5_body(u_ref, xre_ref, xim_ref, bmat_ref, cre_ref, cim_ref, pre_ref, pim_ref, cyr_ref, cyi_ref,
             d_ref, wglu_ref, bglu_ref, od_ref, hre_out, him_out, hre_s, him_s, car_re, car_im, *, bb, tt):
    @pl.when(pl.program_id(1) == 0)
    def _():
        car_re[...] = xre_ref[...]
        car_im[...] = xim_ref[...]

    w = W_BRANCH
    groups = tt // SUBLANES
    uf = u_ref[...].reshape(bb * tt, w)
    bu = _mm(uf.astype(BF16), bmat_ref[...])
    xr = bu[:, :S5_W].reshape(bb, tt, S5_W)
    xi = bu[:, S5_W:].reshape(bb, tt, S5_W)
    for n, s in enumerate((1, 2, 4)):
        cr = _tile_rows(pre_ref[n], groups)
        ci = _tile_rows(pim_ref[n], groups)
        sr = pltpu.roll(xr, s, axis=1)
        si = pltpu.roll(xi, s, axis=1)
        xr, xi = xr + (cr * sr - ci * si), xi + (cr * si + ci * sr)
    cyr = cyr_ref[...]
    cyi = cyi_ref[...]
    hr = car_re[...]
    hi = car_im[...]
    for g in range(groups):
        sl = slice(g * SUBLANES, (g + 1) * SUBLANES)
        gr = xr[:, sl] + (cyr * hr - cyi * hi)
        gi = xi[:, sl] + (cyr * hi + cyi * hr)
        hre_s[:, sl, :] = gr
        him_s[:, sl, :] = gi
        hr = gr[:, SUBLANES - 1:SUBLANES]
        hi = gi[:, SUBLANES - 1:SUBLANES]
    car_re[...] = hr
    car_im[...] = hi
    hre_out[...] = hr
    him_out[...] = hi
    h_re = hre_s[...].reshape(bb * tt, S5_W).astype(BF16)
    h_im = him_s[...].reshape(bb * tt, S5_W).astype(BF16)
    y = _mm(h_re, cre_ref[...]) - _mm(h_im, cim_ref[...])
    y = y + d_ref[...] * uf
    z = 0.5 * y * (1.0 + jnp.tanh(math.sqrt(2.0 / math.pi) * (y + 0.044715 * (y * y * y))))
    out = z * _sigmoid(_mm(z.astype(BF16), wglu_ref[...]) + bglu_ref[...])
    od_ref[...] = out.reshape(bb, tt, w)


def _s5(u, xre, xim, bmat, cre, cim, pre, pim, cyr, cyi, d, wglu, bglu, bb, tt):
    b, t, w = u.shape
    seq = lambda i, j: (i, j, 0)
    per_b = lambda i, j: (i, 0, 0)
    c2 = lambda i, j: (0, 0)
    c3 = lambda i, j: (0, 0, 0)
    return pl.pallas_call(
        functools.partial(_s5_body, bb=bb, tt=tt),
        grid=(b // bb, t // tt),
        in_specs=[pl.BlockSpec((bb, tt, w), seq),
                  pl.BlockSpec((bb, 1, S5_W), per_b),
                  pl.BlockSpec((bb, 1, S5_W), per_b),
                  pl.BlockSpec(bmat.shape, c2),
                  pl.BlockSpec(cre.shape, c2),
                  pl.BlockSpec(cim.shape, c2),
                  pl.BlockSpec(pre.shape, c3),
                  pl.BlockSpec(pim.shape, c3),
                  pl.BlockSpec(cyr.shape, c2),
                  pl.BlockSpec(cyi.shape, c2),
                  pl.BlockSpec((1, w), c2),
                  pl.BlockSpec((w, w), c2),
                  pl.BlockSpec((1, w), c2)],
        out_specs=[pl.BlockSpec((bb, tt, w), seq),
                   pl.BlockSpec((bb, 1, S5_W), per_b),
                   pl.BlockSpec((bb, 1, S5_W), per_b)],
        out_shape=[jax.ShapeDtypeStruct((b, t, w), F32),
                   jax.ShapeDtypeStruct((b, 1, S5_W), F32),
                   jax.ShapeDtypeStruct((b, 1, S5_W), F32)],
        scratch_shapes=[pltpu.VMEM((bb, tt, S5_W), F32), pltpu.VMEM((bb, tt, S5_W), F32),
                        pltpu.VMEM((bb, 1, S5_W), F32), pltpu.VMEM((bb, 1, S5_W), F32)],
        compiler_params=_cparams("parallel", "arbitrary"),
        name="s5",
    )(u, xre, xim, bmat, cre, cim, pre, pim, cyr, cyi, d, wglu, bglu)


def _s5_params(lre, lim, logdt, b_re, b_im, c_re, c_im):
    dt = jnp.exp(logdt)[:, None]
    mag = jnp.exp(lre * dt)
    ab_re, ab_im = mag * jnp.cos(lim * dt), mag * jnp.sin(lim * dt)
    den = lre * lre + lim * lim
    pr = ab_re - 1.0
    f_re = (pr * lre + ab_im * lim) / den
    f_im = (ab_im * lre - pr * lim) / den
    bb_re = f_re[..., None] * b_re - f_im[..., None] * b_im
    bb_im = f_re[..., None] * b_im + f_im[..., None] * b_re
    eye = jnp.eye(S5_GROUPS, dtype=F32)
    in_re = jnp.einsum('gnc,gh->gchn', bb_re, eye).reshape(W_BRANCH, S5_W)
    in_im = jnp.einsum('gnc,gh->gchn', bb_im, eye).reshape(W_BRANCH, S5_W)
    bmat = jnp.concatenate([in_re, in_im], axis=1).astype(BF16)
    cre = jnp.einsum('gcn,gh->gnhc', c_re, eye).reshape(S5_W, W_BRANCH).astype(BF16)
    cim = jnp.einsum('gcn,gh->gnhc', c_im, eye).reshape(S5_W, W_BRANCH).astype(BF16)
    ar, ai = ab_re.reshape(1, S5_W), ab_im.reshape(1, S5_W)
    pows = [(ar, ai)]
    for _ in range(SUBLANES - 1):
        qr, qi = pows[-1]
        pows.append((qr * ar - qi * ai, qr * ai + qi * ar))
    rows = jnp.arange(SUBLANES)[:, None]
    pre = jnp.stack([jnp.where(rows >= s, pows[s - 1][0], 0.0) for s in (1, 2, 4)])
    pim = jnp.stack([jnp.where(rows >= s, pows[s - 1][1], 0.0) for s in (1, 2, 4)])
    cyr = jnp.concatenate([q[0] for q in pows], axis=0)
    cyi = jnp.concatenate([q[1] for q in pows], axis=0)
    return bmat, cre, cim, pre, pim, cyr, cyi


def _outproj_body(x_ref, oa_ref, ob_ref, yc_ref, bonus_ref, od_ref, gate_ref,
                  subln_ref, gnw_ref, gnb_ref, ones_ref, wout_ref, npost_ref, y_ref, *, attn_scale):
    w = W_BRANCH
    ones_blk = ones_ref[...]
    oa = oa_ref[...]
    oa = oa * lax.rsqrt(_segsum(oa * oa, ones_blk) * (1.0 / A_V) + NORM_EPS) * subln_ref[...] * attn_scale
    yc = yc_ref[...]
    mean = _segsum(yc, ones_blk) * (1.0 / RW_HS)
    cen = yc - mean
    var = _segsum(cen * cen, ones_blk) * (1.0 / RW_HS)
    oc = cen * lax.rsqrt(var + RW_GN_EPS) * gnw_ref[...] + gnb_ref[...] + bonus_ref[...]
    gate = gate_ref[...]
    z = None
    for n, branch in enumerate((oa, ob_ref[...], oc, od_ref[...])):
        g = gate[:, n * w:(n + 1) * w]
        part = _mm((branch * (g * _sigmoid(g))).astype(BF16), wout_ref[n * w:(n + 1) * w, :])
        z = part if z is None else z + part
    zn = z * lax.rsqrt(jnp.mean(z * z, axis=-1, keepdims=True) + NORM_EPS) * npost_ref[...]
    y_ref[...] = x_ref[...] + zn


def _outproj(x2, oa, ob, yc, bonus, od, gate, subln, gnw, gnb, ones_blk, wout_bf, npost, attn_scale, tm):
    n = x2.shape[0]
    w = W_BRANCH
    row = lambda i: (i, 0)
    const = lambda i: (0, 0)
    rw = pl.BlockSpec((tm, w), row)
    rd = pl.BlockSpec((tm, D_MODEL), row)
    vec = pl.BlockSpec((1, w), const)
    return pl.pallas_call(
        functools.partial(_outproj_body, attn_scale=attn_scale),
        grid=(n // tm,),
        in_specs=[rd, rw, rw, rw, rw, rw, rd, vec, vec, vec,
                  pl.BlockSpec((w, w), const),
                  pl.BlockSpec((D_MODEL, D_MODEL), const),
                  pl.BlockSpec((1, D_MODEL), const)],
        out_specs=rd,
        out_shape=jax.ShapeDtypeStruct((n, D_MODEL), F32),
        compiler_params=_cparams("parallel"),
        name="outproj",
    )(x2, oa, ob, yc, bonus, od, gate, subln, gnw, gnb, ones_blk, wout_bf, npost)


def _block_diag(blocks):
    n, k, _ = blocks.shape
    eye = jnp.eye(n, dtype=blocks.dtype)
    return jnp.einsum('nij,nm->nimj', blocks, eye).reshape(n * k, n * k)


def _prep_layer(wts, lam_init):
    (norm_pre, norm_post, w_in, w_out, lam_q1, lam_k1, lam_q2, lam_k2, subln_w, conv_w, conv_b,
     lru_wa, lru_ba, lru_wx, lru_bx, lru_lam,
     rw_mu, rw_w0, rw_w1, rw_w2, rw_a0, rw_a1, rw_a2, rw_kk, rw_ka, rw_rk, rw_gnw, rw_gnb,
     s5_lre, s5_lim, s5_logdt, s5_bre, s5_bim, s5_cre, s5_cim, s5_d, s5_wglu, s5_bglu) = wts
    w = W_BRANCH
    row = lambda v: v.reshape(1, -1)
    return dict(
        norm_pre=row(norm_pre), w_in=w_in.astype(BF16), w_out=w_out.astype(BF16), norm_post=row(norm_post),
        lam=(jnp.exp(jnp.sum(lam_q1 * lam_k1)) - jnp.exp(jnp.sum(lam_q2 * lam_k2)) + lam_init).reshape(1),
        attn_scale=1.0 - lam_init, subln=row(jnp.tile(subln_w, A_HEADS)),
        conv_w=conv_w, conv_b=row(conv_b),
        lru_w=jnp.concatenate([_block_diag(lru_wa), _block_diag(lru_wx)], axis=1).astype(BF16),
        lru_b=jnp.concatenate([lru_ba, lru_bx]).reshape(1, 2 * w), lru_lam=row(lru_lam),
        rwkv=(row(rw_mu), row(rw_w0), rw_w1.astype(BF16), rw_w2.astype(BF16), row(rw_a0),
              rw_a1.astype(BF16), rw_a2.astype(BF16), row(rw_kk), row(rw_ka), row(rw_rk)),
        gnw=row(rw_gnw), gnb=row(rw_gnb),
        ones_blk=_block_diag(jnp.ones((RW_HEADS, RW_HS, RW_HS), BF16)),
        s5=_s5_params(s5_lre, s5_lim, s5_logdt, s5_bre, s5_bim, s5_cre, s5_cim)
        + (row(s5_d), s5_wglu.astype(BF16), row(s5_bglu)),
    )


def _layer(x, attend, states, p, tiles):
    conv_buf, lru_h, shift_prev, wkv_s, ssm_x = states
    b, t, _ = x.shape
    n = b * t
    w = W_BRANCH
    tm, bb, tt = tiles

    x2 = x.reshape(n, D_MODEL)
    q, k, v, xb, pc, u, gate = _inproj(x2, p["norm_pre"], p["w_in"], tm)

    oa = attend(p["lam"], q.reshape(b, t, w), k.reshape(b, t, w), v.reshape(b, t, w))

    cst8 = jnp.pad(conv_buf, ((0, 0), (SUBLANES - (CONV_W - 1), 0), (0, 0)))
    ob, conv8, new_h = _mixb(xb.reshape(b, t, w), cst8, lru_h.reshape(b, 1, w), p["conv_w"], p["conv_b"],
                             p["lru_w"], p["lru_b"], p["lru_lam"], bb, tt)
    new_conv = conv8[:, SUBLANES - (CONV_W - 1):]

    jv, rv, bonus, new_shift = _rwkv_pre(pc.reshape(b, t, RW_NPROJ * w), shift_prev.reshape(b, 1, RW_NPROJ * w),
                                         *p["rwkv"], p["ones_blk"], bb, tt)
    yc, new_s = _rwkv_scan(jv, rv, wkv_s)

    od, hre, him = _s5(u.reshape(b, t, w), ssm_x[..., 0].reshape(b, 1, S5_W), ssm_x[..., 1].reshape(b, 1, S5_W),
                       *p["s5"], bb, tt)
    new_ssm = jnp.stack([hre.reshape(b, S5_GROUPS, S5_STATE), him.reshape(b, S5_GROUPS, S5_STATE)], axis=-1)

    y = _outproj(x2, oa.reshape(n, w), ob.reshape(n, w), yc.reshape(n, w), bonus.reshape(n, w),
                 od.reshape(n, w), gate, p["subln"], p["gnw"], p["gnb"],
                 p["ones_blk"], p["w_out"], p["norm_post"], p["attn_scale"], tm)
    new_states = (k.reshape(b, t, A_HEADS, A_V), v.reshape(b, t, A_HEADS, A_V), new_conv,
                  new_h.reshape(b, w), new_shift.reshape(b, RW_NPROJ * w), new_s, new_ssm)
    return y.reshape(b, t, D_MODEL), new_states


def kernel(x_prompt, x_sample, cache_k, cache_v, page_table, state_conv, state_lru, state_shift, state_wkv, state_ssm, norm_pre, norm_post, w_in, w_out, lam_q1, lam_k1, lam_q2, lam_k2, subln_w, conv_w, conv_b, lru_wa, lru_ba, lru_wx, lru_bx, lru_lam, rw_mu, rw_w0, rw_w1, rw_w2, rw_a0, rw_a1, rw_a2, rw_kk, rw_ka, rw_rk, rw_gnw, rw_gnb, s5_lre, s5_lim, s5_logdt, s5_bre, s5_bim, s5_cre, s5_cim, s5_d, s5_wglu, s5_bglu):
    weights = (norm_pre, norm_post, w_in, w_out, lam_q1, lam_k1, lam_q2, lam_k2, subln_w,
               conv_w, conv_b, lru_wa, lru_ba, lru_wx, lru_bx, lru_lam,
               rw_mu, rw_w0, rw_w1, rw_w2, rw_a0, rw_a1, rw_a2, rw_kk, rw_ka, rw_rk, rw_gnw, rw_gnb,
               s5_lre, s5_lim, s5_logdt, s5_bre, s5_bim, s5_cre, s5_cim, s5_d, s5_wglu, s5_bglu)
    depth = w_in.shape[0]
    nbp, tp, _ = x_prompt.shape
    nbs, ts, _ = x_sample.shape
    dt = x_prompt.dtype
    w = W_BRANCH
    zero_states = (jnp.zeros((nbp, CONV_W - 1, w), dt), jnp.zeros((nbp, w), dt),
                   jnp.zeros((nbp, RW_NPROJ * w), dt), jnp.zeros((nbp, RW_HEADS, RW_HS, RW_HS), dt),
                   jnp.zeros((nbp, S5_GROUPS, S5_STATE, 2), dt))
    tiles_p = (min(512, nbp * tp), 1, min(256, tp))
    tiles_s = (min(512, nbs * ts), min(16, nbs), ts)
    tq = min(256, tp)
    xp, xs = x_prompt, x_sample
    outs_p = [[] for _ in range(7)]
    outs_s = [[] for _ in range(7)]
    for l in range(depth):
        prep = _prep_layer(tuple(wt[l] for wt in weights), 0.8 - 0.6 * math.exp(-0.3 * l))
        xp, st_p = _layer(xp, functools.partial(_attn_prompt, tq=tq), zero_states, prep, tiles_p)
        attend_s = functools.partial(_attn_sample, cache_k_l=cache_k[l], cache_v_l=cache_v[l],
                                     page_table=page_table)
        st_in = (state_conv[l], state_lru[l], state_shift[l], state_wkv[l], state_ssm[l])
        xs, st_s = _layer(xs, attend_s, st_in, prep, tiles_s)
        for n in range(7):
            outs_p[n].append(st_p[n])
            outs_s[n].append(st_s[n])
    k_p, v_p, conv_p, lru_p, shift_p, wkv_p, ssm_p = [jnp.stack(z) for z in outs_p]
    k_s, v_s, conv_s, lru_s, shift_s, wkv_s, ssm_s = [jnp.stack(z) for z in outs_s]
    return (xp, xs, k_p, k_s, v_p, v_s, conv_p, conv_s, lru_p, lru_s,
            shift_p, shift_s, wkv_p, wkv_s, ssm_p, ssm_s)
```

```python
import functools
import math

import jax
import jax.numpy as jnp
from jax import lax
from jax.experimental import pallas as pl
from jax.experimental.pallas import tpu as pltpu

F32 = jnp.float32
BF16 = jnp.bfloat16

D_MODEL = 1024
W_BRANCH = 256
A_HEADS = 4
A_QK = 32
A_V = 64
PAGE_SIZE = 128
LRU_BLOCKS = 4
LRU_BW = 64
CONV_W = 4
LRU_C = 8.0
RW_HEADS = 4
RW_HS = 64
RW_NPROJ = 5
RW_GN_EPS = RW_HS * 1e-5
S5_CH = 16
S5_GROUPS = 16
S5_STATE = 64
S5_W = S5_GROUPS * S5_STATE
NORM_EPS = 1e-6
NEG_BIG = -1e30

OFF_AQ = 0
OFF_AK = OFF_AQ + W_BRANCH
OFF_AV = OFF_AK + W_BRANCH
OFF_B = OFF_AV + W_BRANCH
OFF_C = OFF_B + W_BRANCH
OFF_D = OFF_C + RW_NPROJ * W_BRANCH
OFF_G = OFF_D + W_BRANCH
D_IN = OFF_G + D_MODEL
_SEGS = ((OFF_AQ, OFF_AK), (OFF_AK, OFF_AV), (OFF_AV, OFF_B), (OFF_B, OFF_C),
         (OFF_C, OFF_D), (OFF_D, OFF_G), (OFF_G, D_IN))

SUBLANES = 8
LANES = 128
VMEM_LIMIT_MIB = 56


def _cparams(*sem):
    return pltpu.CompilerParams(dimension_semantics=sem,
                                vmem_limit_bytes=VMEM_LIMIT_MIB * 1024 * 1024)


def _nt(a, b):
    return lax.dot_general(a, b, (((1,), (1,)), ((), ())), preferred_element_type=F32)


def _mm(a, b):
    return jnp.dot(a, b, preferred_element_type=F32)


def _segsum(x, ones_blk):
    hi = x.astype(BF16)
    lo = (x - hi.astype(F32)).astype(BF16)
    return _mm(hi, ones_blk) + _mm(lo, ones_blk)


def _sigmoid(x):
    return jax.nn.sigmoid(x)


def _softplus(x):
    return jnp.maximum(x, 0.0) + jnp.log1p(jnp.exp(-jnp.abs(x)))


_SEG_K, _SEG_V = 1, 2


def _inproj_body(x_ref, g_ref, w_ref, wkvt_ref, *out_refs, kv_t):
    bb, tt, _ = x_ref.shape
    x = x_ref[...].reshape(bb * tt, D_MODEL)
    h = x * lax.rsqrt(jnp.mean(x * x, axis=-1, keepdims=True) + NORM_EPS) * g_ref[...]
    hb = h.astype(BF16)
    if kv_t:
        kvt = _nt(wkvt_ref[...], hb)
        out_refs[_SEG_K][0] = kvt[:W_BRANCH]
        out_refs[_SEG_V][0] = kvt[W_BRANCH:]
    for n, ((lo, hi), o_ref) in enumerate(zip(_SEGS, out_refs)):
        if not (kv_t and n in (_SEG_K, _SEG_V)):
            o_ref[...] = _mm(hb, w_ref[:, lo:hi]).reshape(bb, tt, hi - lo)


def _inproj(x, g, w_bf, wkvt_bf, bb, tt, kv_t):
    b, t, _ = x.shape
    assert bb == 1 or not kv_t
    seq = lambda i, j: (i, j, 0)
    const = lambda i, j: (0, 0)
    out_specs, out_shape = [], []
    for n, (lo, hi) in enumerate(_SEGS):
        if kv_t and n in (_SEG_K, _SEG_V):
            out_specs.append(pl.BlockSpec((1, hi - lo, tt), lambda i, j: (i, 0, j)))
            out_shape.append(jax.ShapeDtypeStruct((b, hi - lo, t), F32))
        else:
            out_specs.append(pl.BlockSpec((bb, tt, hi - lo), seq))
            out_shape.append(jax.ShapeDtypeStruct((b, t, hi - lo), F32))
    return pl.pallas_call(
        functools.partial(_inproj_body, kv_t=kv_t),
        grid=(b // bb, t // tt),
        in_specs=[pl.BlockSpec((bb, tt, D_MODEL), seq),
                  pl.BlockSpec((1, D_MODEL), const),
                  pl.BlockSpec((D_MODEL, D_IN), const),
                  pl.BlockSpec((2 * W_BRANCH, D_MODEL), const)],
        out_specs=out_specs,
        out_shape=out_shape,
        compiler_params=_cparams("parallel", "parallel"),
        name="inproj",
    )(x, g, w_bf, wkvt_bf)


def _attn_prompt_body(lam_ref, q_ref, k_ref, v_ref, o_ref, kb_ref, vm_ref, qm_ref, acc_ref, m_ref, l_ref, *, tq):
    qi = pl.program_id(1)
    nhc = 2 * A_HEADS
    lane = lax.broadcasted_iota(jnp.int32, (1, W_BRANCH), 1)

    @pl.when(qi == 0)
    def _():
        kb_ref[...] = k_ref[0].astype(BF16)
        v = v_ref[0]
        feat = lax.broadcasted_iota(jnp.int32, (W_BRANCH, 1), 0)
        for h in range(A_HEADS):
            vm_ref[h] = jnp.where(feat // A_V == h, v, 0.0).astype(BF16)

    q = q_ref[0] * (A_QK ** -0.5)
    for hc in range(nhc):
        qm_ref[hc * tq:(hc + 1) * tq, :] = jnp.where(lane // A_QK == hc, q, 0.0).astype(BF16)
    m_ref[...] = jnp.full(m_ref.shape, NEG_BIG, F32)
    l_ref[...] = jnp.zeros(l_ref.shape, F32)
    acc_ref[...] = jnp.zeros(acc_ref.shape, F32)

    def wide(x):
        return jnp.concatenate([x, x], axis=1)

    def block(kb, diagonal):
        ks = pl.multiple_of(kb * tq, tq)
        s = _mm(qm_ref[...], kb_ref[:, pl.ds(ks, tq)])
        if diagonal:
            r = lax.broadcasted_iota(jnp.int32, (nhc * tq, 1), 0) % tq
            c = lax.broadcasted_iota(jnp.int32, (1, tq), 1)
            s = jnp.where(c <= r, s, NEG_BIG)
        m_old = m_ref[...]
        m_new = jnp.maximum(m_old, jnp.max(s, axis=-1, keepdims=True))
        alpha = jnp.exp(m_old - m_new)
        p = jnp.exp(s - wide(m_new))
        l_ref[...] = alpha * l_ref[...] + jnp.sum(p, axis=-1, keepdims=True)
        m_ref[...] = m_new
        pb = p.astype(BF16)
        a2 = wide(alpha)
        for h in range(A_HEADS):
            rows = slice(2 * h * tq, (2 * h + 2) * tq)
            acc_ref[rows, :] = a2[rows] * acc_ref[rows, :] + _nt(pb[rows], vm_ref[h, :, pl.ds(ks, tq)])

    def off_diagonal(kb, carry):
        block(kb, False)
        return carry

    lax.fori_loop(0, qi, off_diagonal, 0)
    block(qi, True)
    lam = lam_ref[0]
    inv_l = wide(1.0 / l_ref[...])
    out = jnp.zeros((tq, W_BRANCH), F32)
    for h in range(A_HEADS):
        r0 = slice(2 * h * tq, (2 * h + 1) * tq)
        r1 = slice((2 * h + 1) * tq, (2 * h + 2) * tq)
        out = out + (acc_ref[r0, :] * inv_l[r0] - lam * (acc_ref[r1, :] * inv_l[r1]))
    o_ref[0] = out


def _attn_prompt(lam, q, kt, vt, tq):
    b, t, _ = q.shape
    nhc = 2 * A_HEADS
    return pl.pallas_call(
        functools.partial(_attn_prompt_body, tq=tq),
        grid=(b, t // tq),
        in_specs=[pl.BlockSpec(memory_space=pltpu.SMEM),
                  pl.BlockSpec((1, tq, W_BRANCH), lambda i, j: (i, j, 0)),
                  pl.BlockSpec((1, W_BRANCH, t), lambda i, j: (i, 0, 0)),
                  pl.BlockSpec((1, W_BRANCH, t), lambda i, j: (i, 0, 0))],
        out_specs=pl.BlockSpec((1, tq, W_BRANCH), lambda i, j: (i, j, 0)),
        out_shape=jax.ShapeDtypeStruct((b, t, W_BRANCH), F32),
        scratch_shapes=[pltpu.VMEM((W_BRANCH, t), BF16),
                        pltpu.VMEM((A_HEADS, W_BRANCH, t), BF16),
                        pltpu.VMEM((nhc * tq, W_BRANCH), BF16),
                        pltpu.VMEM((nhc * tq, W_BRANCH), F32),
                        pltpu.VMEM((nhc * tq, LANES), F32),
                        pltpu.VMEM((nhc * tq, LANES), F32)],
        compiler_params=_cparams("parallel", "arbitrary"),
        name="attn_prompt",
    )(lam, q, kt, vt)


def _attn_sample_body(pt_ref, lam_ref, q_ref, kn_ref, vn_ref, *rest, n_pages, tq, nseq):
    del pt_ref
    kp_refs = rest[:nseq * n_pages]
    vp_refs = rest[nseq * n_pages:2 * nseq * n_pages]
    o_ref = rest[2 * nseq * n_pages]
    nhc = 2 * A_HEADS
    lane = lax.broadcasted_iota(jnp.int32, (1, W_BRANCH), 1)
    rowhc = lax.broadcasted_iota(jnp.int32, (nhc * tq, 1), 0) // tq
    tcol = lax.broadcasted_iota(jnp.int32, (nhc * tq, tq), 1)
    trow = lax.broadcasted_iota(jnp.int32, (nhc * tq, tq), 0) % tq
    lam = lam_ref[0]

    def diff(p):
        return jnp.concatenate(
            [p[(2 * h) * tq:(2 * h + 1) * tq] - lam * p[(2 * h + 1) * tq:(2 * h + 2) * tq]
             for h in range(A_HEADS)], axis=0)

    for n in range(nseq):
        kp = kp_refs[n * n_pages:(n + 1) * n_pages]
        vp = vp_refs[n * n_pages:(n + 1) * n_pages]
        q = q_ref[n] * (A_QK ** -0.5)
        qbd = jnp.where(lane // A_QK == rowhc, jnp.tile(q, (nhc, 1)), 0.0).astype(BF16)
        s_past = jnp.concatenate([_mm(qbd, kp[j][0, 0].astype(BF16)) for j in range(n_pages)], axis=1)
        s_new = _nt(qbd, kn_ref[n].astype(BF16))
        s_new = jnp.where(tcol <= trow, s_new, NEG_BIG)
        m = jnp.maximum(jnp.max(s_past, axis=-1, keepdims=True), jnp.max(s_new, axis=-1, keepdims=True))
        p_past = jnp.exp(s_past - m)
        p_new = jnp.exp(s_new - m)
        inv = 1.0 / (jnp.sum(p_past, axis=-1, keepdims=True) + jnp.sum(p_new, axis=-1, keepdims=True))
        w_past = diff(p_past * inv).astype(BF16)
        w_new = diff(p_new * inv).astype(BF16)
        out_all = _mm(w_new, vn_ref[n].astype(BF16))
        for j in range(n_pages):
            out_all = out_all + _nt(w_past[:, j * PAGE_SIZE:(j + 1) * PAGE_SIZE], vp[j][0, 0].astype(BF16))
        out = jnp.zeros((tq, W_BRANCH), F32)
        for h in range(A_HEADS):
            out = out + jnp.where(lane // A_V == h, out_all[h * tq:(h + 1) * tq], 0.0)
        o_ref[n] = out


def _attn_sample(lam, q, k, v, cache_kt, cache_vt, layer, page_table, nseq):
    b, tq, _ = q.shape
    n_pages = page_table.shape[1]
    pt = page_table.reshape(-1)
    seq = pl.BlockSpec((nseq, tq, W_BRANCH), lambda i, pt_r, lam_r: (i, 0, 0))

    def page_spec(n, j):
        return pl.BlockSpec((1, 1, W_BRANCH, PAGE_SIZE),
                            lambda i, pt_r, lam_r: (layer, pt_r[(i * nseq + n) * n_pages + j], 0, 0))

    pages = [page_spec(n, j) for n in range(nseq) for j in range(n_pages)]
    grid_spec = pltpu.PrefetchScalarGridSpec(
        num_scalar_prefetch=2,
        grid=(b // nseq,),
        in_specs=[seq, seq, seq] + pages * 2,
        out_specs=seq,
    )
    return pl.pallas_call(
        functools.partial(_attn_sample_body, n_pages=n_pages, tq=tq, nseq=nseq),
        grid_spec=grid_spec,
        out_shape=jax.ShapeDtypeStruct((b, tq, W_BRANCH), F32),
        compiler_params=_cparams("parallel"),
        name="attn_sample",
    )(pt, lam, q, k, v, *([cache_kt] * (nseq * n_pages)), *([cache_vt] * (nseq * n_pages)))


def _tile_rows(c8, groups):
    return jnp.tile(c8, (groups, 1)) if groups > 1 else c8


def _shift_in(prev8, x, s):
    tt = x.shape[1]
    ext = jnp.concatenate([prev8, x], axis=1)
    return pltpu.roll(ext, s, axis=1)[:, SUBLANES:SUBLANES + tt]


def _mixb_body(xb_ref, cst_ref, h0_ref, cw_ref, cb_ref, wg_ref, bg_ref, lam_ref,
               ob_ref, cout_ref, hout_ref, prev_ref, hcar_ref, *, bb, tt):
    @pl.when(pl.program_id(1) == 0)
    def _():
        prev_ref[...] = cst_ref[...]
        hcar_ref[...] = h0_ref[...]

    w = W_BRANCH
    x = xb_ref[...]
    prev8 = prev_ref[...]
    cw = cw_ref[...]
    xc = cb_ref[...] + _shift_in(prev8, x, 3) * cw[0:1]
    xc = xc + _shift_in(prev8, x, 2) * cw[1:2]
    xc = xc + _shift_in(prev8, x, 1) * cw[2:3]
    xc = xc + x * cw[3:4]
    tail = x[:, tt - SUBLANES:tt]
    prev_ref[...] = tail
    cout_ref[...] = tail

    xf = xc.reshape(bb * tt, w)
    gates = _mm(xf.astype(BF16), wg_ref[...]) + bg_ref[...]
    r = _sigmoid(gates[:, :w])
    i = _sigmoid(gates[:, w:])
    log_a = -LRU_C * r * _softplus(-lam_ref[...])
    a = jnp.exp(log_a).reshape(bb, tt, w)
    b = (jnp.sqrt(1.0 - jnp.exp(2.0 * log_a)) * (i * xf)).reshape(bb, tt, w)

    rmod = lax.broadcasted_iota(jnp.int32, (1, tt, 1), 1) % SUBLANES
    for s in (1, 2, 4):
        keep = rmod >= s
        b = b + a * jnp.where(keep, pltpu.roll(b, s, axis=1), 0.0)
        a = a * jnp.where(keep, pltpu.roll(a, s, axis=1), 1.0)
    h = hcar_ref[...]
    for g in range(tt // SUBLANES):
        sl = slice(g * SUBLANES, (g + 1) * SUBLANES)
        hg = b[:, sl] + a[:, sl] * h
        ob_ref[:, sl, :] = hg
        h = hg[:, SUBLANES - 1:SUBLANES]
    hcar_ref[...] = h
    hout_ref[...] = h


def _mixb(xb, cst8, h0, cw, cb, wg, bg, lru_lam, bb, tt):
    b, t, w = xb.shape
    seq = lambda i, j: (i, j, 0)
    per_b = lambda i, j: (i, 0, 0)
    const = lambda i, j: (0, 0)
    return pl.pallas_call(
        functools.partial(_mixb_body, bb=bb, tt=tt),
        grid=(b // bb, t // tt),
        in_specs=[pl.BlockSpec((bb, tt, w), seq),
                  pl.BlockSpec((bb, SUBLANES, w), per_b),
                  pl.BlockSpec((bb, 1, w), per_b),
                  pl.BlockSpec((CONV_W, w), const),
                  pl.BlockSpec((1, w), const),
                  pl.BlockSpec((w, 2 * w), const),
                  pl.BlockSpec((1, 2 * w), const),
                  pl.BlockSpec((1, w), const)],
        out_specs=[pl.BlockSpec((bb, tt, w), seq),
                   pl.BlockSpec((bb, SUBLANES, w), per_b),
                   pl.BlockSpec((bb, 1, w), per_b)],
        out_shape=[jax.ShapeDtypeStruct((b, t, w), F32),
                   jax.ShapeDtypeStruct((b, SUBLANES, w), F32),
                   jax.ShapeDtypeStruct((b, 1, w), F32)],
        scratch_shapes=[pltpu.VMEM((bb, SUBLANES, w), F32), pltpu.VMEM((bb, 1, w), F32)],
        compiler_params=_cparams("parallel", "arbitrary"),
        name="conv_rglru",
    )(xb, cst8, h0, cw, cb, wg, bg, lru_lam)


def _rwkv_pre_body(pc_ref, sh0_ref, mu_ref, w0_ref, w1_ref, w2_ref, a0_ref, a1_ref, a2_ref,
                   kkw_ref, ka_ref, rk_ref, ones_ref,
                   jv_ref, v_ref, bonus_ref, shout_ref, prev_ref, *, bb, tt):
    @pl.when(pl.program_id(1) == 0)
    def _():
        prev_ref[...] = jnp.broadcast_to(sh0_ref[...], prev_ref.shape)

    w = W_BRANCH
    p = pc_ref[...]
    p_prev = _shift_in(prev_ref[...], p, 1)
    tail = p[:, tt - SUBLANES:tt]
    prev_ref[...] = tail
    shout_ref[...] = p[:, tt - 1:tt]
    xm = (p + (p_prev - p) * mu_ref[...]).reshape(bb * tt, RW_NPROJ * w)
    xr, xw, xk, xv, xa = (xm[:, n * w:(n + 1) * w] for n in range(RW_NPROJ))
    lw = _mm(jnp.tanh(_mm(xw.astype(BF16), w1_ref[...])).astype(BF16), w2_ref[...])
    wlog = -_softplus(-(w0_ref[...] + lw)) - 0.5
    decay = jnp.exp(-jnp.exp(wlog))
    a = _sigmoid(a0_ref[...] + _mm(_mm(xa.astype(BF16), a1_ref[...]).astype(BF16), a2_ref[...]))
    kk = xk * kkw_ref[...]
    ones_blk = ones_ref[...]
    kk = kk / jnp.maximum(jnp.sqrt(_segsum(kk * kk, ones_blk)), 1e-12)
    k = xk * (1.0 + (a - 1.0) * ka_ref[...])
    bonus = _segsum(xr * k * rk_ref[...], ones_blk) * xv
    shp = (bb, tt, w)
    jv_ref[:, :, 0 * w:1 * w] = decay.reshape(shp)
    jv_ref[:, :, 1 * w:2 * w] = (-kk).reshape(shp)
    jv_ref[:, :, 2 * w:3 * w] = (kk * a).reshape(shp)
    jv_ref[:, :, 3 * w:4 * w] = k.reshape(shp)
    jv_ref[:, :, 4 * w:5 * w] = xr.reshape(shp)
    v_ref[...] = xv.reshape(shp)
    bonus_ref[...] = bonus.reshape(shp)


def _rwkv_pre(pc, sh0, mu, w0, w1, w2, a0, a1, a2, kkw, ka, rk, ones_blk, bb, tt):
    b, t, wp = pc.shape
    w = W_BRANCH
    seq = lambda i, j: (i, j, 0)
    per_b = lambda i, j: (i, 0, 0)
    const = lambda i, j: (0, 0)
    vec = pl.BlockSpec((1, w), const)
    return pl.pallas_call(
        functools.partial(_rwkv_pre_body, bb=bb, tt=tt),
        grid=(b // bb, t // tt),
        in_specs=[pl.BlockSpec((bb, tt, wp), seq),
                  pl.BlockSpec((bb, 1, wp), per_b),
                  pl.BlockSpec((1, wp), const),
                  vec, pl.BlockSpec(w1.shape, const), pl.BlockSpec(w2.shape, const),
                  vec, pl.BlockSpec(a1.shape, const), pl.BlockSpec(a2.shape, const),
                  vec, vec, vec, pl.BlockSpec((w, w), const)],
        out_specs=[pl.BlockSpec((bb, tt, wp), seq),
                   pl.BlockSpec((bb, tt, w), seq),
                   pl.BlockSpec((bb, tt, w), seq),
                   pl.BlockSpec((bb, 1, wp), per_b)],
        out_shape=[jax.ShapeDtypeStruct((b, t, wp), F32),
                   jax.ShapeDtypeStruct((b, t, w), F32),
                   jax.ShapeDtypeStruct((b, t, w), F32),
                   jax.ShapeDtypeStruct((b, 1, wp), F32)],
        scratch_shapes=[pltpu.VMEM((bb, SUBLANES, wp), F32)],
        compiler_params=_cparams("parallel", "arbitrary"),
        name="rwkv_pre",
    )(pc, sh0, mu, w0, w1, w2, a0, a1, a2, kkw, ka, rk, ones_blk)


_JV_DECAY, _JV_NKK, _JV_KKA, _JV_K, _JV_R = range(5)
RW_NJV = 5
XT = LANES
RELAY_UNROLL = 8
Z_PAD_ROWS = SUBLANES


def _tree_sum(parts):
    while len(parts) > 1:
        parts = [parts[n] + parts[n + 1] for n in range(0, len(parts), 2)]
    return parts[0]


def _rwkv_steps(zrow, zv_ref, zy_ref, s_ref, nsteps, rows, n_acc=4):
    def accumulate(parts, j, term):
        parts[j % n_acc] = term if parts[j % n_acc] is None else parts[j % n_acc] + term

    sa_parts = [None] * n_acc
    for j in range(RW_HS):
        accumulate(sa_parts, j, s_ref[j] * zrow(_JV_NKK, j, 0))

    def step(t, sa):
        r0 = pl.multiple_of(t * rows, rows)
        v = zv_ref[pl.ds(r0, rows), :]
        t_next = jnp.minimum(t + 1, nsteps - 1)
        y_parts = [None] * n_acc
        sa_parts = [None] * n_acc
        for j in range(RW_HS):
            s = s_ref[j] * zrow(_JV_DECAY, j, t) + sa * zrow(_JV_KKA, j, t) + v * zrow(_JV_K, j, t)
            s_ref[j] = s
            accumulate(y_parts, j, s * zrow(_JV_R, j, t))
            accumulate(sa_parts, j, s * zrow(_JV_NKK, j, t_next))
        zy_ref[pl.ds(r0, rows), :] = _tree_sum(y_parts)
        return _tree_sum(sa_parts)

    lax.fori_loop(0, nsteps, step, _tree_sum(sa_parts))


def _rwkv_scan_nat_body(jv_ref, v_ref, s0_ref, y_ref, s_ref, e_ref, z_ref, zv_ref, zy_ref, *, nb):
    reps = LANES // (nb * RW_HEADS)
    rows = RW_HS // reps
    nbh = nb * RW_HEADS

    @pl.when(pl.program_id(0) == 0)
    def _():
        s_ref[...] = s0_ref[...]

    halves = [(b, h0) for b in range(nb) for h0 in range(0, W_BRANCH, LANES)]

    def fill_e(src_ref, lane0):
        for b, h0 in halves:
            e_ref[b * W_BRANCH + h0:b * W_BRANCH + h0 + LANES, :] = src_ref[b, :, lane0 + h0:lane0 + h0 + LANES].T

    for q in range(RW_NJV):
        fill_e(jv_ref, q * W_BRANCH)

        def relay(jg, carry, q=q):
            for jj in range(RELAY_UNROLL):
                j = jg * RELAY_UNROLL + jj
                r = e_ref[pl.ds(j, nbh, stride=RW_HS), :]
                z_ref[q, j, 0:XT, :] = jnp.concatenate([r] * reps, axis=0).T
            return carry

        lax.fori_loop(0, RW_HS // RELAY_UNROLL, relay, 0)
    fill_e(v_ref, 0)
    for ih in range(rows):
        tile = jnp.concatenate([e_ref[pl.ds(rep * rows + ih, nbh, stride=RW_HS), :] for rep in range(reps)], axis=0)
        zv_ref[pl.ds(ih, XT, stride=rows), :] = tile.T

    _rwkv_steps(lambda which, j, t: z_ref[which, j, pl.ds(t, 1), :], zv_ref, zy_ref, s_ref, XT, rows)

    for ih in range(rows):
        tile = zy_ref[pl.ds(ih, XT, stride=rows), :].T
        for rep in range(reps):
            e_ref[pl.ds(rep * rows + ih, nbh, stride=RW_HS), :] = tile[rep * nbh:(rep + 1) * nbh]
    for b, h0 in halves:
        y_ref[b, :, h0:h0 + LANES] = e_ref[b * W_BRANCH + h0:b * W_BRANCH + h0 + LANES, :].T


def _rwkv_scan_nat(jv, v, s0):
    nb, t, wq = jv.shape
    rows = s0.shape[1]
    seq = lambda i: (0, i, 0)
    st = lambda i: (0, 0, 0)
    return pl.pallas_call(
        functools.partial(_rwkv_scan_nat_body, nb=nb),
        grid=(t // XT,),
        in_specs=[pl.BlockSpec((nb, XT, wq), seq),
                  pl.BlockSpec((nb, XT, W_BRANCH), seq),
                  pl.BlockSpec((RW_HS, rows, LANES), st)],
        out_specs=[pl.BlockSpec((nb, XT, W_BRANCH), seq),
                   pl.BlockSpec((RW_HS, rows, LANES), st)],
        out_shape=[jax.ShapeDtypeStruct((nb, t, W_BRANCH), F32),
                   jax.ShapeDtypeStruct((RW_HS, rows, LANES), F32)],
        scratch_shapes=[pltpu.VMEM((nb * W_BRANCH, XT), F32),
                        pltpu.VMEM((RW_NJV, RW_HS, XT + Z_PAD_ROWS, LANES), F32),
                        pltpu.VMEM((XT * rows, LANES), F32),
                        pltpu.VMEM((XT * rows, LANES), F32)],
        compiler_params=_cparams("arbitrary"),
        name="rwkv_scan",
    )(jv, v, s0)


def _rwkv_scan_lanes_body(z_ref, zv_ref, s0_ref, zy_ref, s_ref, *, nsteps, rows):
    s_ref[...] = s0_ref[...]
    _rwkv_steps(lambda which, j, t: z_ref[0, which, j, pl.ds(t, 1), :], zv_ref.at[0], zy_ref.at[0],
                s_ref.at[0], nsteps, rows)


def _rwkv_scan_lanes(z, zv, s0):
    g, nq, hs, t, _ = z.shape
    rows = s0.shape[2]
    return pl.pallas_call(
        functools.partial(_rwkv_scan_lanes_body, nsteps=t, rows=rows),
        grid=(g,),
        in_specs=[pl.BlockSpec((1, nq, hs, t, LANES), lambda i: (i, 0, 0, 0, 0)),
                  pl.BlockSpec((1, t * rows, LANES), lambda i: (i, 0, 0)),
                  pl.BlockSpec((1, hs, rows, LANES), lambda i: (i, 0, 0, 0))],
        out_specs=[pl.BlockSpec((1, t * rows, LANES), lambda i: (i, 0, 0)),
                   pl.BlockSpec((1, hs, rows, LANES), lambda i: (i, 0, 0, 0))],
        out_shape=[jax.ShapeDtypeStruct((g, t * rows, LANES), F32),
                   jax.ShapeDtypeStruct((g, hs, rows, LANES), F32)],
        compiler_params=_cparams("parallel"),
        name="rwkv_scan_lanes",
    )(z, zv, s0)


def _lane_split(b):
    seqs = min(b, LANES // RW_HEADS)
    return b // seqs, seqs, LANES // (RW_HEADS * seqs)


def _state_to_lanes(s, g, bl, reps):
    rows = RW_HS // reps
    s = s.reshape(g, bl, RW_HEADS, reps, rows, RW_HS).transpose(0, 5, 4, 3, 1, 2)
    return s.reshape(g, RW_HS, rows, LANES)


def _state_from_lanes(s, bl, reps):
    g, _, rows, _ = s.shape
    s = s.reshape(g, RW_HS, rows, reps, bl, RW_HEADS).transpose(0, 4, 5, 3, 2, 1)
    return s.reshape(g * bl, RW_HEADS, RW_HS, RW_HS)


def _rwkv_scan(jv, v, wkv_s):
    b, t, _ = jv.shape
    g, bl, reps = _lane_split(b)
    s0 = _state_to_lanes(wkv_s, g, bl, reps)
    if g == 1 and t % XT == 0:
        y, s = _rwkv_scan_nat(jv, v, s0[0])
        return y, _state_from_lanes(s[None], bl, reps)
    rows = RW_HS // reps
    z = jv.reshape(g, bl, t, RW_NJV, RW_HEADS, RW_HS).transpose(0, 3, 5, 2, 1, 4)
    z = jnp.broadcast_to(z[:, :, :, :, None], (g, RW_NJV, RW_HS, t, reps, bl, RW_HEADS)).reshape(g, RW_NJV, RW_HS, t, LANES)
    zv = v.reshape(g, bl, t, RW_HEADS, reps, rows).transpose(0, 2, 5, 4, 1, 3).reshape(g, t * rows, LANES)
    zy, s = _rwkv_scan_lanes(z, zv, s0)
    y = zy.reshape(g, t, rows, reps, bl, RW_HEADS).transpose(0, 4, 1, 5, 3, 2).reshape(b, t, W_BRANCH)
    return y, _state_from_lanes(s, bl, reps)


def _s5_body(u_ref, xre_ref, xim_ref, bmat_ref, cre_ref, cim_ref, pre_ref, pim_ref, cyr_ref, cyi_ref,
             d_ref, wglu_ref, bglu_ref, od_ref, hre_out, him_out, hre_s, him_s, car_re, car_im, *, bb, tt):
    @pl.when(pl.program_id(1) == 0)
    def _():
        car_re[...] = xre_ref[...]
        car_im[...] = xim_ref[...]

    w = W_BRANCH
    groups = tt // SUBLANES
    uf = u_ref[...].reshape(bb * tt, w)
    bu = _mm(uf.astype(BF16), bmat_ref[...])
    xr = bu[:, :S5_W].reshape(bb, tt, S5_W)
    xi = bu[:, S5_W:].reshape(bb, tt, S5_W)
    for n, s in enumerate((1, 2, 4)):
        cr = _tile_rows(pre_ref[n], groups)
        ci = _tile_rows(pim_ref[n], groups)
        sr = pltpu.roll(xr, s, axis=1)
        si = pltpu.roll(xi, s, axis=1)
        xr, xi = xr + (cr * sr - ci * si), xi + (cr * si + ci * sr)
    cyr = cyr_ref[...]
    cyi = cyi_ref[...]
    hr = car_re[...]
    hi = car_im[...]
    for g in range(groups):
        sl = slice(g * SUBLANES, (g + 1) * SUBLANES)
        gr = xr[:, sl] + (cyr * hr - cyi * hi)
        gi = xi[:, sl] + (cyr * hi + cyi * hr)
        hre_s[:, sl, :] = gr
        him_s[:, sl, :] = gi
        hr = gr[:, SUBLANES - 1:SUBLANES]
        hi = gi[:, SUBLANES - 1:SUBLANES]
    car_re[...] = hr
    car_im[...] = hi
    hre_out[...] = hr
    him_out[...] = hi
    h_re = hre_s[...].reshape(bb * tt, S5_W).astype(BF16)
    h_im = him_s[...].reshape(bb * tt, S5_W).astype(BF16)
    y = _mm(h_re, cre_ref[...]) - _mm(h_im, cim_ref[...])
    y = y + d_ref[...] * uf
    z = 0.5 * y * (1.0 + jnp.tanh(math.sqrt(2.0 / math.pi) * (y + 0.044715 * (y * y * y))))
    out = z * _sigmoid(_mm(z.astype(BF16), wglu_ref[...]) + bglu_ref[...])
    od_ref[...] = out.reshape(bb, tt, w)


def _s5(u, xre, xim, bmat, cre, cim, pre, pim, cyr, cyi, d, wglu, bglu, bb, tt):
    b, t, w = u.shape
    seq = lambda i, j: (i, j, 0)
    per_b = lambda i, j: (i, 0, 0)
    c2 = lambda i, j: (0, 0)
    c3 = lambda i, j: (0, 0, 0)
    return pl.pallas_call(
        functools.partial(_s5_body, bb=bb, tt=tt),
        grid=(b // bb, t // tt),
        in_specs=[pl.BlockSpec((bb, tt, w), seq),
                  pl.BlockSpec((bb, 1, S5_W), per_b),
                  pl.BlockSpec((bb, 1, S5_W), per_b),
                  pl.BlockSpec(bmat.shape, c2),
                  pl.BlockSpec(cre.shape, c2),
                  pl.BlockSpec(cim.shape, c2),
                  pl.BlockSpec(pre.shape, c3),
                  pl.BlockSpec(pim.shape, c3),
                  pl.BlockSpec(cyr.shape, c2),
                  pl.BlockSpec(cyi.shape, c2),
                  pl.BlockSpec((1, w), c2),
                  pl.BlockSpec((w, w), c2),
                  pl.BlockSpec((1, w), c2)],
        out_specs=[pl.BlockSpec((bb, tt, w), seq),
                   pl.BlockSpec((bb, 1, S5_W), per_b),
                   pl.BlockSpec((bb, 1, S5_W), per_b)],
        out_shape=[jax.ShapeDtypeStruct((b, t, w), F32),
                   jax.ShapeDtypeStruct((b, 1, S5_W), F32),
                   jax.ShapeDtypeStruct((b, 1, S5_W), F32)],
        scratch_shapes=[pltpu.VMEM((bb, tt, S5_W), F32), pltpu.VMEM((bb, tt, S5_W), F32),
                        pltpu.VMEM((bb, 1, S5_W), F32), pltpu.VMEM((bb, 1, S5_W), F32)],
        compiler_params=_cparams("parallel", "arbitrary"),
        name="s5",
    )(u, xre, xim, bmat, cre, cim, pre, pim, cyr, cyi, d, wglu, bglu)


def _s5_params(lre, lim, logdt, b_re, b_im, c_re, c_im):
    dt = jnp.exp(logdt)[:, None]
    mag = jnp.exp(lre * dt)
    ab_re, ab_im = mag * jnp.cos(lim * dt), mag * jnp.sin(lim * dt)
    den = lre * lre + lim * lim
    pr = ab_re - 1.0
    f_re = (pr * lre + ab_im * lim) / den
    f_im = (ab_im * lre - pr * lim) / den
    bb_re = f_re[..., None] * b_re - f_im[..., None] * b_im
    bb_im = f_re[..., None] * b_im + f_im[..., None] * b_re
    eye = jnp.eye(S5_GROUPS, dtype=F32)
    in_re = jnp.einsum('gnc,gh->gchn', bb_re, eye).reshape(W_BRANCH, S5_W)
    in_im = jnp.einsum('gnc,gh->gchn', bb_im, eye).reshape(W_BRANCH, S5_W)
    bmat = jnp.concatenate([in_re, in_im], axis=1).astype(BF16)
    cre = jnp.einsum('gcn,gh->gnhc', c_re, eye).reshape(S5_W, W_BRANCH).astype(BF16)
    cim = jnp.einsum('gcn,gh->gnhc', c_im, eye).reshape(S5_W, W_BRANCH).astype(BF16)
    ar, ai = ab_re.reshape(1, S5_W), ab_im.reshape(1, S5_W)
    pows = [(ar, ai)]
    for _ in range(SUBLANES - 1):
        qr, qi = pows[-1]
        pows.append((qr * ar - qi * ai, qr * ai + qi * ar))
    rows = jnp.arange(SUBLANES)[:, None]
    pre = jnp.stack([jnp.where(rows >= s, pows[s - 1][0], 0.0) for s in (1, 2, 4)])
    pim = jnp.stack([jnp.where(rows >= s, pows[s - 1][1], 0.0) for s in (1, 2, 4)])
    cyr = jnp.concatenate([q[0] for q in pows], axis=0)
    cyi = jnp.concatenate([q[1] for q in pows], axis=0)
    return bmat, cre, cim, pre, pim, cyr, cyi


def _outproj_body(x_ref, oa_ref, ob_ref, yc_ref, bonus_ref, od_ref, gate_ref,
                  subln_ref, gnw_ref, gnb_ref, ones_ref, wout_ref, npost_ref, y_ref, *, attn_scale):
    w = W_BRANCH
    ones_blk = ones_ref[...]
    oa = oa_ref[...]
    oa = oa * lax.rsqrt(_segsum(oa * oa, ones_blk) * (1.0 / A_V) + NORM_EPS) * subln_ref[...] * attn_scale
    yc = yc_ref[...]
    mean = _segsum(yc, ones_blk) * (1.0 / RW_HS)
    cen = yc - mean
    var = _segsum(cen * cen, ones_blk) * (1.0 / RW_HS)
    oc = cen * lax.rsqrt(var + RW_GN_EPS) * gnw_ref[...] + gnb_ref[...] + bonus_ref[...]
    gate = gate_ref[...]
    z = None
    for n, branch in enumerate((oa, ob_ref[...], oc, od_ref[...])):
        g = gate[:, n * w:(n + 1) * w]
        part = _mm((branch * (g * _sigmoid(g))).astype(BF16), wout_ref[n * w:(n + 1) * w, :])
        z = part if z is None else z + part
    zn = z * lax.rsqrt(jnp.mean(z * z, axis=-1, keepdims=True) + NORM_EPS) * npost_ref[...]
    y_ref[...] = x_ref[...] + zn


def _outproj(x2, oa, ob, yc, bonus, od, gate, subln, gnw, gnb, ones_blk, wout_bf, npost, attn_scale, tm):
    n = x2.shape[0]
    w = W_BRANCH
    row = lambda i: (i, 0)
    const = lambda i: (0, 0)
    rw = pl.BlockSpec((tm, w), row)
    rd = pl.BlockSpec((tm, D_MODEL), row)
    vec = pl.BlockSpec((1, w), const)
    return pl.pallas_call(
        functools.partial(_outproj_body, attn_scale=attn_scale),
        grid=(n // tm,),
        in_specs=[rd, rw, rw, rw, rw, rw, rd, vec, vec, vec,
                  pl.BlockSpec((w, w), const),
                  pl.BlockSpec((D_MODEL, D_MODEL), const),
                  pl.BlockSpec((1, D_MODEL), const)],
        out_specs=rd,
        out_shape=jax.ShapeDtypeStruct((n, D_MODEL), F32),
        compiler_params=_cparams("parallel"),
        name="outproj",
    )(x2, oa, ob, yc, bonus, od, gate, subln, gnw, gnb, ones_blk, wout_bf, npost)


def _block_diag(blocks):
    n, k, _ = blocks.shape
    eye = jnp.eye(n, dtype=blocks.dtype)
    return jnp.einsum('nij,nm->nimj', blocks, eye).reshape(n * k, n * k)


def _prep_layer(wts, lam_init):
    (norm_pre, norm_post, w_in, w_out, lam_q1, lam_k1, lam_q2, lam_k2, subln_w, conv_w, conv_b,
     lru_wa, lru_ba, lru_wx, lru_bx, lru_lam,
     rw_mu, rw_w0, rw_w1, rw_w2, rw_a0, rw_a1, rw_a2, rw_kk, rw_ka, rw_rk, rw_gnw, rw_gnb,
     s5_lre, s5_lim, s5_logdt, s5_bre, s5_bim, s5_cre, s5_cim, s5_d, s5_wglu, s5_bglu) = wts
    w = W_BRANCH
    row = lambda v: v.reshape(1, -1)
    return dict(
        norm_pre=row(norm_pre), w_in=w_in.astype(BF16), w_out=w_out.astype(BF16), norm_post=row(norm_post),
        w_kvt=w_in[:, OFF_AK:OFF_B].T.astype(BF16),
        lam=(jnp.exp(jnp.sum(lam_q1 * lam_k1)) - jnp.exp(jnp.sum(lam_q2 * lam_k2)) + lam_init).reshape(1),
        attn_scale=1.0 - lam_init, subln=row(jnp.tile(subln_w, A_HEADS)),
        conv_w=conv_w, conv_b=row(conv_b),
        lru_w=jnp.concatenate([_block_diag(lru_wa), _block_diag(lru_wx)], axis=1).astype(BF16),
        lru_b=jnp.concatenate([lru_ba, lru_bx]).reshape(1, 2 * w), lru_lam=row(lru_lam),
        rwkv=(row(rw_mu), row(rw_w0), rw_w1.astype(BF16), rw_w2.astype(BF16), row(rw_a0),
              rw_a1.astype(BF16), rw_a2.astype(BF16), row(rw_kk), row(rw_ka), row(rw_rk)),
        gnw=row(rw_gnw), gnb=row(rw_gnb),
        ones_blk=_block_diag(jnp.ones((RW_HEADS, RW_HS, RW_HS), BF16)),
        s5=_s5_params(s5_lre, s5_lim, s5_logdt, s5_bre, s5_bim, s5_cre, s5_cim)
        + (row(s5_d), s5_wglu.astype(BF16), row(s5_bglu)),
    )


def _layer(x, attend, states, p, tiles):
    conv_buf, lru_h, shift_prev, wkv_s, ssm_x = states
    b, t, _ = x.shape
    n = b * t
    w = W_BRANCH
    tm, bb_proj, tt_proj, bb, tt, kv_t = tiles

    q, k, v, xb, pc, u, gate = _inproj(x, p["norm_pre"], p["w_in"], p["w_kvt"], bb_proj, tt_proj, kv_t)

    oa = attend(p["lam"], q, k, v)
    if kv_t:
        new_k = k.reshape(b, A_HEADS, A_V, t).transpose(0, 3, 1, 2)
        new_v = v.reshape(b, A_HEADS, A_V, t).transpose(0, 3, 1, 2)
    else:
        new_k, new_v = k.reshape(b, t, A_HEADS, A_V), v.reshape(b, t, A_HEADS, A_V)

    cst8 = jnp.pad(conv_buf, ((0, 0), (SUBLANES - (CONV_W - 1), 0), (0, 0)))
    ob, conv8, new_h = _mixb(xb, cst8, lru_h.reshape(b, 1, w), p["conv_w"], p["conv_b"],
                             p["lru_w"], p["lru_b"], p["lru_lam"], bb, tt)
    new_conv = conv8[:, SUBLANES - (CONV_W - 1):]

    jv, rv, bonus, new_shift = _rwkv_pre(pc, shift_prev.reshape(b, 1, RW_NPROJ * w),
                                         *p["rwkv"], p["ones_blk"], bb, tt)
    yc, new_s = _rwkv_scan(jv, rv, wkv_s)

    od, hre, him = _s5(u, ssm_x[..., 0].reshape(b, 1, S5_W), ssm_x[..., 1].reshape(b, 1, S5_W),
                       *p["s5"], bb, tt)
    new_ssm = jnp.stack([hre.reshape(b, S5_GROUPS, S5_STATE), him.reshape(b, S5_GROUPS, S5_STATE)], axis=-1)

    y = _outproj(x.reshape(n, D_MODEL), oa.reshape(n, w), ob.reshape(n, w), yc.reshape(n, w),
                 bonus.reshape(n, w), od.reshape(n, w), gate.reshape(n, D_MODEL), p["subln"], p["gnw"], p["gnb"],
                 p["ones_blk"], p["w_out"], p["norm_post"], p["attn_scale"], tm)
    new_states = (new_k, new_v, new_conv,
                  new_h.reshape(b, w), new_shift.reshape(b, RW_NPROJ * w), new_s, new_ssm)
    return y.reshape(b, t, D_MODEL), new_states


def kernel(x_prompt, x_sample, cache_k, cache_v, page_table, state_conv, state_lru, state_shift, state_wkv, state_ssm, norm_pre, norm_post, w_in, w_out, lam_q1, lam_k1, lam_q2, lam_k2, subln_w, conv_w, conv_b, lru_wa, lru_ba, lru_wx, lru_bx, lru_lam, rw_mu, rw_w0, rw_w1, rw_w2, rw_a0, rw_a1, rw_a2, rw_kk, rw_ka, rw_rk, rw_gnw, rw_gnb, s5_lre, s5_lim, s5_logdt, s5_bre, s5_bim, s5_cre, s5_cim, s5_d, s5_wglu, s5_bglu):
    weights = (norm_pre, norm_post, w_in, w_out, lam_q1, lam_k1, lam_q2, lam_k2, subln_w,
               conv_w, conv_b, lru_wa, lru_ba, lru_wx, lru_bx, lru_lam,
               rw_mu, rw_w0, rw_w1, rw_w2, rw_a0, rw_a1, rw_a2, rw_kk, rw_ka, rw_rk, rw_gnw, rw_gnb,
               s5_lre, s5_lim, s5_logdt, s5_bre, s5_bim, s5_cre, s5_cim, s5_d, s5_wglu, s5_bglu)
    depth = w_in.shape[0]
    nbp, tp, _ = x_prompt.shape
    nbs, ts, _ = x_sample.shape
    dt = x_prompt.dtype
    w = W_BRANCH
    zero_states = (jnp.zeros((nbp, CONV_W - 1, w), dt), jnp.zeros((nbp, w), dt),
                   jnp.zeros((nbp, RW_NPROJ * w), dt), jnp.zeros((nbp, RW_HEADS, RW_HS, RW_HS), dt),
                   jnp.zeros((nbp, S5_GROUPS, S5_STATE, 2), dt))
    tiles_p = (min(512, nbp * tp), 1, min(512, tp), 1, min(256, tp), True)
    tiles_s = (min(512, nbs * ts), min(64, nbs), ts, min(16, nbs), ts, False)
    tq = min(256, tp)
    n_pool = cache_k.shape[1]
    cache_kt = cache_k.transpose(0, 1, 3, 4, 2).reshape(depth, n_pool, W_BRANCH, PAGE_SIZE)
    cache_vt = cache_v.transpose(0, 1, 3, 4, 2).reshape(depth, n_pool, W_BRANCH, PAGE_SIZE)
    xp, xs = x_prompt, x_sample
    outs_p = [[] for _ in range(7)]
    outs_s = [[] for _ in range(7)]
    for l in range(depth):
        prep = _prep_layer(tuple(wt[l] for wt in weights), 0.8 - 0.6 * math.exp(-0.3 * l))
        xp, st_p = _layer(xp, functools.partial(_attn_prompt, tq=tq), zero_states, prep, tiles_p)
        attend_s = functools.partial(_attn_sample, cache_kt=cache_kt, cache_vt=cache_vt, layer=l,
                                     page_table=page_table, nseq=min(2, nbs))
        st_in = (state_conv[l], state_lru[l], state_shift[l], state_wkv[l], state_ssm[l])
        xs, st_s = _layer(xs, attend_s, st_in, prep, tiles_s)
        for n in range(7):
            outs_p[n].append(st_p[n])
            outs_s[n].append(st_s[n])
    k_p, v_p, conv_p, lru_p, shift_p, wkv_p, ssm_p = [jnp.stack(z) for z in outs_p]
    k_s, v_s, conv_s, lru_s, shift_s, wkv_s, ssm_s = [jnp.stack(z) for z in outs_s]
    return (xp, xs, k_p, k_s, v_p, v_s, conv_p, conv_s, lru_p, lru_s,
            shift_p, shift_s, wkv_p, wkv_s, ssm_p, ssm_s)
```

```python
import functools
import math

import jax
import jax.numpy as jnp
from jax import lax
from jax.experimental import pallas as pl
from jax.experimental.pallas import tpu as pltpu

F32 = jnp.float32
BF16 = jnp.bfloat16

D_MODEL = 1024
W_BRANCH = 256
A_HEADS = 4
A_QK = 32
A_V = 64
PAGE_SIZE = 128
LRU_BLOCKS = 4
LRU_BW = 64
CONV_W = 4
LRU_C = 8.0
RW_HEADS = 4
RW_HS = 64
RW_NPROJ = 5
RW_GN_EPS = RW_HS * 1e-5
S5_CH = 16
S5_GROUPS = 16
S5_STATE = 64
S5_W = S5_GROUPS * S5_STATE
NORM_EPS = 1e-6
NEG_BIG = -1e30

OFF_AQ = 0
OFF_AK = OFF_AQ + W_BRANCH
OFF_AV = OFF_AK + W_BRANCH
OFF_B = OFF_AV + W_BRANCH
OFF_C = OFF_B + W_BRANCH
OFF_D = OFF_C + RW_NPROJ * W_BRANCH
OFF_G = OFF_D + W_BRANCH
D_IN = OFF_G + D_MODEL
_SEGS = ((OFF_AQ, OFF_AK), (OFF_AK, OFF_AV), (OFF_AV, OFF_B), (OFF_B, OFF_C),
         (OFF_C, OFF_D), (OFF_D, OFF_G), (OFF_G, D_IN))

SUBLANES = 8
LANES = 128
VMEM_LIMIT_MIB = 56


def _cparams(*sem):
    return pltpu.CompilerParams(dimension_semantics=sem,
                                vmem_limit_bytes=VMEM_LIMIT_MIB * 1024 * 1024)


def _nt(a, b):
    return lax.dot_general(a, b, (((1,), (1,)), ((), ())), preferred_element_type=F32)


def _mm(a, b):
    return jnp.dot(a, b, preferred_element_type=F32)


def _segsum(x, ones_blk):
    hi = x.astype(BF16)
    lo = (x - hi.astype(F32)).astype(BF16)
    return _mm(hi, ones_blk) + _mm(lo, ones_blk)


def _sigmoid(x):
    return jax.nn.sigmoid(x)


def _softplus(x):
    return jnp.maximum(x, 0.0) + jnp.log1p(jnp.exp(-jnp.abs(x)))


_SEG_K, _SEG_V = 1, 2


def _inproj_body(x_ref, g_ref, w_ref, wkvt_ref, *out_refs, kv_t):
    bb, tt, _ = x_ref.shape
    x = x_ref[...].reshape(bb * tt, D_MODEL)
    h = x * lax.rsqrt(jnp.mean(x * x, axis=-1, keepdims=True) + NORM_EPS) * g_ref[...]
    hb = h.astype(BF16)
    if kv_t:
        kvt = _nt(wkvt_ref[...], hb)
        out_refs[_SEG_K][0] = kvt[:W_BRANCH]
        out_refs[_SEG_V][0] = kvt[W_BRANCH:]
    for n, ((lo, hi), o_ref) in enumerate(zip(_SEGS, out_refs)):
        if not (kv_t and n in (_SEG_K, _SEG_V)):
            o_ref[...] = _mm(hb, w_ref[:, lo:hi]).reshape(bb, tt, hi - lo)


def _inproj(x, g, w_bf, wkvt_bf, bb, tt, kv_t):
    b, t, _ = x.shape
    assert bb == 1 or not kv_t
    seq = lambda i, j: (i, j, 0)
    const = lambda i, j: (0, 0)
    out_specs, out_shape = [], []
    for n, (lo, hi) in enumerate(_SEGS):
        if kv_t and n in (_SEG_K, _SEG_V):
            out_specs.append(pl.BlockSpec((1, hi - lo, tt), lambda i, j: (i, 0, j)))
            out_shape.append(jax.ShapeDtypeStruct((b, hi - lo, t), F32))
        else:
            out_specs.append(pl.BlockSpec((bb, tt, hi - lo), seq))
            out_shape.append(jax.ShapeDtypeStruct((b, t, hi - lo), F32))
    return pl.pallas_call(
        functools.partial(_inproj_body, kv_t=kv_t),
        grid=(b // bb, t // tt),
        in_specs=[pl.BlockSpec((bb, tt, D_MODEL), seq),
                  pl.BlockSpec((1, D_MODEL), const),
                  pl.BlockSpec((D_MODEL, D_IN), const),
                  pl.BlockSpec((2 * W_BRANCH, D_MODEL), const)],
        out_specs=out_specs,
        out_shape=out_shape,
        compiler_params=_cparams("parallel", "parallel"),
        name="inproj",
    )(x, g, w_bf, wkvt_bf)


def _attn_prompt_body(lam_ref, q_ref, k_ref, v_ref, o_ref, kb_ref, vb_ref, qm_ref, acc_ref, m_ref, l_ref, *, tq):
    qi = pl.program_id(1)
    nhc = 2 * A_HEADS
    lane = lax.broadcasted_iota(jnp.int32, (1, W_BRANCH), 1)

    @pl.when(qi == 0)
    def _():
        kb_ref[...] = k_ref[0].astype(BF16)
        vb_ref[...] = v_ref[0].astype(BF16)

    q = q_ref[0] * (A_QK ** -0.5 * math.log2(math.e))
    for hc in range(nhc):
        qm_ref[hc * tq:(hc + 1) * tq, :] = jnp.where(lane // A_QK == hc, q, 0.0).astype(BF16)
    m_ref[...] = jnp.full(m_ref.shape, NEG_BIG, F32)
    l_ref[...] = jnp.zeros(l_ref.shape, F32)
    acc_ref[...] = jnp.zeros(acc_ref.shape, F32)

    def wide(x):
        return jnp.concatenate([x, x], axis=1)

    def block(kb, diagonal):
        ks = pl.multiple_of(kb * tq, tq)
        s = _mm(qm_ref[...], kb_ref[:, pl.ds(ks, tq)])
        if diagonal:
            r = lax.broadcasted_iota(jnp.int32, (nhc * tq, 1), 0) % tq
            c = lax.broadcasted_iota(jnp.int32, (1, tq), 1)
            s = jnp.where(c <= r, s, NEG_BIG)
        m_old = m_ref[...]
        m_new = jnp.maximum(m_old, jnp.max(s, axis=-1, keepdims=True))
        alpha = jnp.exp2(m_old - m_new)
        p = jnp.exp2(s - wide(m_new))
        l_ref[...] = alpha * l_ref[...] + jnp.sum(p, axis=-1, keepdims=True)
        m_ref[...] = m_new
        pb = p.astype(BF16)
        a_v = alpha[:, :A_V]
        for h in range(A_HEADS):
            rows = slice(2 * h * tq, (2 * h + 2) * tq)
            pv = _nt(pb[rows], vb_ref[h * A_V:(h + 1) * A_V, pl.ds(ks, tq)])
            acc_ref[rows, :] = a_v[rows] * acc_ref[rows, :] + pv

    def off_diagonal(kb, carry):
        block(kb, False)
        return carry

    lax.fori_loop(0, qi, off_diagonal, 0)
    block(qi, True)
    lam = lam_ref[0]
    inv_l = (1.0 / l_ref[...])[:, :A_V]
    heads = []
    for h in range(A_HEADS):
        r0 = slice(2 * h * tq, (2 * h + 1) * tq)
        r1 = slice((2 * h + 1) * tq, (2 * h + 2) * tq)
        heads.append(acc_ref[r0, :] * inv_l[r0] - lam * (acc_ref[r1, :] * inv_l[r1]))
    o_ref[0] = jnp.concatenate(heads, axis=1)


def _attn_prompt(lam, q, kt, vt, tq):
    b, t, _ = q.shape
    nhc = 2 * A_HEADS
    return pl.pallas_call(
        functools.partial(_attn_prompt_body, tq=tq),
        grid=(b, t // tq),
        in_specs=[pl.BlockSpec(memory_space=pltpu.SMEM),
                  pl.BlockSpec((1, tq, W_BRANCH), lambda i, j: (i, j, 0)),
                  pl.BlockSpec((1, W_BRANCH, t), lambda i, j: (i, 0, 0)),
                  pl.BlockSpec((1, W_BRANCH, t), lambda i, j: (i, 0, 0))],
        out_specs=pl.BlockSpec((1, tq, W_BRANCH), lambda i, j: (i, j, 0)),
        out_shape=jax.ShapeDtypeStruct((b, t, W_BRANCH), F32),
        scratch_shapes=[pltpu.VMEM((W_BRANCH, t), BF16),
                        pltpu.VMEM((W_BRANCH, t), BF16),
                        pltpu.VMEM((nhc * tq, W_BRANCH), BF16),
                        pltpu.VMEM((nhc * tq, A_V), F32),
                        pltpu.VMEM((nhc * tq, LANES), F32),
                        pltpu.VMEM((nhc * tq, LANES), F32)],
        compiler_params=_cparams("parallel", "arbitrary"),
        name="attn_prompt",
    )(lam, q, kt, vt)


def _attn_sample_body(pt_ref, lam_ref, q_ref, kn_ref, vn_ref, *rest, n_pages, tq, nseq):
    del pt_ref
    kp_refs = rest[:nseq * n_pages]
    vp_refs = rest[nseq * n_pages:2 * nseq * n_pages]
    o_ref = rest[2 * nseq * n_pages]
    nhc = 2 * A_HEADS
    lane = lax.broadcasted_iota(jnp.int32, (1, W_BRANCH), 1)
    rowhc = lax.broadcasted_iota(jnp.int32, (nhc * tq, 1), 0) // tq
    tcol = lax.broadcasted_iota(jnp.int32, (nhc * tq, tq), 1)
    trow = lax.broadcasted_iota(jnp.int32, (nhc * tq, tq), 0) % tq
    lam = lam_ref[0]

    def diff(p):
        return jnp.concatenate(
            [p[(2 * h) * tq:(2 * h + 1) * tq] - lam * p[(2 * h + 1) * tq:(2 * h + 2) * tq]
             for h in range(A_HEADS)], axis=0)

    for n in range(nseq):
        kp = kp_refs[n * n_pages:(n + 1) * n_pages]
        vp = vp_refs[n * n_pages:(n + 1) * n_pages]
        q = q_ref[n] * (A_QK ** -0.5)
        qbd = jnp.where(lane // A_QK == rowhc, jnp.tile(q, (nhc, 1)), 0.0).astype(BF16)
        s_past = jnp.concatenate([_mm(qbd, kp[j][0, 0].astype(BF16)) for j in range(n_pages)], axis=1)
        s_new = _nt(qbd, kn_ref[n].astype(BF16))
        s_new = jnp.where(tcol <= trow, s_new, NEG_BIG)
        m = jnp.maximum(jnp.max(s_past, axis=-1, keepdims=True), jnp.max(s_new, axis=-1, keepdims=True))
        p_past = jnp.exp(s_past - m)
        p_new = jnp.exp(s_new - m)
        inv = 1.0 / (jnp.sum(p_past, axis=-1, keepdims=True) + jnp.sum(p_new, axis=-1, keepdims=True))
        w_past = diff(p_past * inv).astype(BF16)
        w_new = diff(p_new * inv).astype(BF16)
        out_all = _mm(w_new, vn_ref[n].astype(BF16))
        for j in range(n_pages):
            out_all = out_all + _nt(w_past[:, j * PAGE_SIZE:(j + 1) * PAGE_SIZE], vp[j][0, 0].astype(BF16))
        out = jnp.zeros((tq, W_BRANCH), F32)
        for h in range(A_HEADS):
            out = out + jnp.where(lane // A_V == h, out_all[h * tq:(h + 1) * tq], 0.0)
        o_ref[n] = out


def _attn_sample(lam, q, k, v, cache_kt, cache_vt, layer, page_table, nseq):
    b, tq, _ = q.shape
    n_pages = page_table.shape[1]
    pt = page_table.reshape(-1)
    seq = pl.BlockSpec((nseq, tq, W_BRANCH), lambda i, pt_r, lam_r: (i, 0, 0))

    def page_spec(n, j):
        return pl.BlockSpec((1, 1, W_BRANCH, PAGE_SIZE),
                            lambda i, pt_r, lam_r: (layer, pt_r[(i * nseq + n) * n_pages + j], 0, 0))

    pages = [page_spec(n, j) for n in range(nseq) for j in range(n_pages)]
    grid_spec = pltpu.PrefetchScalarGridSpec(
        num_scalar_prefetch=2,
        grid=(b // nseq,),
        in_specs=[seq, seq, seq] + pages * 2,
        out_specs=seq,
    )
    return pl.pallas_call(
        functools.partial(_attn_sample_body, n_pages=n_pages, tq=tq, nseq=nseq),
        grid_spec=grid_spec,
        out_shape=jax.ShapeDtypeStruct((b, tq, W_BRANCH), F32),
        compiler_params=_cparams("parallel"),
        name="attn_sample",
    )(pt, lam, q, k, v, *([cache_kt] * (nseq * n_pages)), *([cache_vt] * (nseq * n_pages)))


def _shift_in(prev8, x, s):
    tt = x.shape[1]
    ext = jnp.concatenate([prev8, x], axis=1)
    return pltpu.roll(ext, s, axis=1)[:, SUBLANES:SUBLANES + tt]


def _to_time_major(src_ref, dst_ref, row0, nb, tt):
    for b in range(nb):
        for c in range(dst_ref.shape[0]):
            dst_ref[c, pl.ds(row0 + b, tt, stride=nb), :] = src_ref[b, :, c * LANES:(c + 1) * LANES]


def _from_time_major(src_ref, dst_ref, nb, tt):
    for b in range(nb):
        for c in range(src_ref.shape[0]):
            dst_ref[b, :, c * LANES:(c + 1) * LANES] = src_ref[c, pl.ds(b, tt, stride=nb), :]


def _planes_get(ref, r0, r1):
    return jnp.concatenate([ref[c, r0:r1, :] for c in range(ref.shape[0])], axis=1)


def _planes_set(ref, r0, r1, val):
    for c in range(ref.shape[0]):
        ref[c, r0:r1, :] = val[:, c * LANES:(c + 1) * LANES]


def _mixb_body(xb_ref, cst_ref, h0_ref, cw_ref, cb_ref, wg_ref, bg_ref, lam_ref,
               ob_ref, cout_ref, hout_ref, xt_ref, ht_ref, hcar_ref, *, nb, tt):
    w = W_BRANCH
    rows = tt * nb
    hist = (CONV_W - 1) * nb

    @pl.when(pl.program_id(1) == 0)
    def _():
        _planes_set(xt_ref, 0, hist, cst_ref[...].reshape(hist, w))
        hcar_ref[...] = h0_ref[...]

    _to_time_major(xb_ref, xt_ref, hist, nb, tt)
    cw = cw_ref[...]
    xc = cb_ref[...] + _planes_get(xt_ref, 0, rows) * cw[0:1]
    for s in range(1, CONV_W):
        xc = xc + _planes_get(xt_ref, s * nb, s * nb + rows) * cw[s:s + 1]
    tail = _planes_get(xt_ref, rows, rows + hist)
    cout_ref[...] = tail.reshape(CONV_W - 1, nb, w)
    _planes_set(xt_ref, 0, hist, tail)

    gates = _mm(xc.astype(BF16), wg_ref[...]) + bg_ref[...]
    r = _sigmoid(gates[:, :w])
    i = _sigmoid(gates[:, w:])
    log_a = -LRU_C * r * _softplus(-lam_ref[...])
    a = jnp.exp(log_a)
    b = jnp.sqrt(1.0 - jnp.exp(2.0 * log_a)) * (i * xc)
    h = hcar_ref[...]
    for t in range(tt):
        sl = slice(t * nb, (t + 1) * nb)
        h = a[sl] * h + b[sl]
        _planes_set(ht_ref, t * nb, (t + 1) * nb, h)
    hcar_ref[...] = h
    hout_ref[...] = h
    _from_time_major(ht_ref, ob_ref, nb, tt)


def _mixb(xb, cst, h0, cw, cb, wg, bg, lru_lam, nb, tt):
    b, t, w = xb.shape
    const = lambda i, j: (0, 0)
    return pl.pallas_call(
        functools.partial(_mixb_body, nb=nb, tt=tt),
        grid=(b // nb, t // tt),
        in_specs=[pl.BlockSpec((nb, tt, w), lambda i, j: (i, j, 0)),
                  pl.BlockSpec((CONV_W - 1, nb, w), lambda i, j: (0, i, 0)),
                  pl.BlockSpec((nb, w), lambda i, j: (i, 0)),
                  pl.BlockSpec((CONV_W, w), const),
                  pl.BlockSpec((1, w), const),
                  pl.BlockSpec((w, 2 * w), const),
                  pl.BlockSpec((1, 2 * w), const),
                  pl.BlockSpec((1, w), const)],
        out_specs=[pl.BlockSpec((nb, tt, w), lambda i, j: (i, j, 0)),
                   pl.BlockSpec((CONV_W - 1, nb, w), lambda i, j: (0, i, 0)),
                   pl.BlockSpec((nb, w), lambda i, j: (i, 0))],
        out_shape=[jax.ShapeDtypeStruct((b, t, w), F32),
                   jax.ShapeDtypeStruct((CONV_W - 1, b, w), F32),
                   jax.ShapeDtypeStruct((b, w), F32)],
        scratch_shapes=[pltpu.VMEM((w // LANES, (tt + CONV_W - 1) * nb, LANES), F32),
                        pltpu.VMEM((w // LANES, tt * nb, LANES), F32),
                        pltpu.VMEM((nb, w), F32)],
        compiler_params=_cparams("parallel", "arbitrary"),
        name="conv_rglru",
    )(xb, cst, h0, cw, cb, wg, bg, lru_lam)


def _rwkv_pre_body(pc_ref, sh0_ref, mu_ref, w0_ref, w1_ref, w2_ref, a0_ref, a1_ref, a2_ref,
                   kkw_ref, ka_ref, rk_ref, ones_ref,
                   jv_ref, v_ref, bonus_ref, shout_ref, prev_ref, *, bb, tt):
    @pl.when(pl.program_id(1) == 0)
    def _():
        prev_ref[...] = jnp.broadcast_to(sh0_ref[...], prev_ref.shape)

    w = W_BRANCH
    p = pc_ref[...]
    p_prev = _shift_in(prev_ref[...], p, 1)
    tail = p[:, tt - SUBLANES:tt]
    prev_ref[...] = tail
    shout_ref[...] = p[:, tt - 1:tt]
    xm = (p + (p_prev - p) * mu_ref[...]).reshape(bb * tt, RW_NPROJ * w)
    xr, xw, xk, xv, xa = (xm[:, n * w:(n + 1) * w] for n in range(RW_NPROJ))
    lw = _mm(jnp.tanh(_mm(xw.astype(BF16), w1_ref[...])).astype(BF16), w2_ref[...])
    wlog = -_softplus(-(w0_ref[...] + lw)) - 0.5
    decay = jnp.exp(-jnp.exp(wlog))
    a = _sigmoid(a0_ref[...] + _mm(_mm(xa.astype(BF16), a1_ref[...]).astype(BF16), a2_ref[...]))
    kk = xk * kkw_ref[...]
    ones_blk = ones_ref[...]
    kk = kk / jnp.maximum(jnp.sqrt(_segsum(kk * kk, ones_blk)), 1e-12)
    k = xk * (1.0 + (a - 1.0) * ka_ref[...])
    bonus = _segsum(xr * k * rk_ref[...], ones_blk) * xv
    shp = (bb, tt, w)
    jv_ref[:, :, 0 * w:1 * w] = decay.reshape(shp)
    jv_ref[:, :, 1 * w:2 * w] = (-kk).reshape(shp)
    jv_ref[:, :, 2 * w:3 * w] = (kk * a).reshape(shp)
    jv_ref[:, :, 3 * w:4 * w] = k.reshape(shp)
    jv_ref[:, :, 4 * w:5 * w] = xr.reshape(shp)
    v_ref[...] = xv.reshape(shp)
    bonus_ref[...] = bonus.reshape(shp)


def _rwkv_pre(pc, sh0, mu, w0, w1, w2, a0, a1, a2, kkw, ka, rk, ones_blk, bb, tt):
    b, t, wp = pc.shape
    w = W_BRANCH
    seq = lambda i, j: (i, j, 0)
    per_b = lambda i, j: (i, 0, 0)
    const = lambda i, j: (0, 0)
    vec = pl.BlockSpec((1, w), const)
    return pl.pallas_call(
        functools.partial(_rwkv_pre_body, bb=bb, tt=tt),
        grid=(b // bb, t // tt),
        in_specs=[pl.BlockSpec((bb, tt, wp), seq),
                  pl.BlockSpec((bb, 1, wp), per_b),
                  pl.BlockSpec((1, wp), const),
                  vec, pl.BlockSpec(w1.shape, const), pl.BlockSpec(w2.shape, const),
                  vec, pl.BlockSpec(a1.shape, const), pl.BlockSpec(a2.shape, const),
                  vec, vec, vec, pl.BlockSpec((w, w), const)],
        out_specs=[pl.BlockSpec((bb, tt, wp), seq),
                   pl.BlockSpec((bb, tt, w), seq),
                   pl.BlockSpec((bb, tt, w), seq),
                   pl.BlockSpec((bb, 1, wp), per_b)],
        out_shape=[jax.ShapeDtypeStruct((b, t, wp), F32),
                   jax.ShapeDtypeStruct((b, t, w), F32),
                   jax.ShapeDtypeStruct((b, t, w), F32),
                   jax.ShapeDtypeStruct((b, 1, wp), F32)],
        scratch_shapes=[pltpu.VMEM((bb, SUBLANES, wp), F32)],
        compiler_params=_cparams("parallel", "arbitrary"),
        name="rwkv_pre",
    )(pc, sh0, mu, w0, w1, w2, a0, a1, a2, kkw, ka, rk, ones_blk)


_JV_DECAY, _JV_NKK, _JV_KKA, _JV_K, _JV_R = range(5)
RW_NJV = 5
XT = LANES
RELAY_UNROLL = 8
Z_PAD_ROWS = SUBLANES


def _tree_sum(parts):
    while len(parts) > 1:
        parts = [parts[n] + parts[n + 1] for n in range(0, len(parts), 2)]
    return parts[0]


def _rwkv_steps(zrow, zv_ref, zy_ref, s_ref, nsteps, rows, n_acc=4):
    def accumulate(parts, j, term):
        parts[j % n_acc] = term if parts[j % n_acc] is None else parts[j % n_acc] + term

    sa_parts = [None] * n_acc
    for j in range(RW_HS):
        accumulate(sa_parts, j, s_ref[j] * zrow(_JV_NKK, j, 0))

    def step(t, sa):
        r0 = pl.multiple_of(t * rows, rows)
        v = zv_ref[pl.ds(r0, rows), :]
        t_next = jnp.minimum(t + 1, nsteps - 1)
        y_parts = [None] * n_acc
        sa_parts = [None] * n_acc
        for j in range(RW_HS):
            s = s_ref[j] * zrow(_JV_DECAY, j, t) + sa * zrow(_JV_KKA, j, t) + v * zrow(_JV_K, j, t)
            s_ref[j] = s
            accumulate(y_parts, j, s * zrow(_JV_R, j, t))
            accumulate(sa_parts, j, s * zrow(_JV_NKK, j, t_next))
        zy_ref[pl.ds(r0, rows), :] = _tree_sum(y_parts)
        return _tree_sum(sa_parts)

    lax.fori_loop(0, nsteps, step, _tree_sum(sa_parts))


def _rwkv_scan_nat_body(jv_ref, v_ref, s0_ref, y_ref, s_ref, e_ref, z_ref, zv_ref, zy_ref, *, nb):
    reps = LANES // (nb * RW_HEADS)
    rows = RW_HS // reps
    nbh = nb * RW_HEADS

    @pl.when(pl.program_id(0) == 0)
    def _():
        s_ref[...] = s0_ref[...]

    halves = [(b, h0) for b in range(nb) for h0 in range(0, W_BRANCH, LANES)]

    def fill_e(src_ref, lane0):
        for b, h0 in halves:
            e_ref[b * W_BRANCH + h0:b * W_BRANCH + h0 + LANES, :] = src_ref[b, :, lane0 + h0:lane0 + h0 + LANES].T

    for q in range(RW_NJV):
        fill_e(jv_ref, q * W_BRANCH)

        def relay(jg, carry, q=q):
            for jj in range(RELAY_UNROLL):
                j = jg * RELAY_UNROLL + jj
                r = e_ref[pl.ds(j, nbh, stride=RW_HS), :]
                z_ref[q, j, 0:XT, :] = jnp.concatenate([r] * reps, axis=0).T
            return carry

        lax.fori_loop(0, RW_HS // RELAY_UNROLL, relay, 0)
    fill_e(v_ref, 0)
    for ih in range(rows):
        tile = jnp.concatenate([e_ref[pl.ds(rep * rows + ih, nbh, stride=RW_HS), :] for rep in range(reps)], axis=0)
        zv_ref[pl.ds(ih, XT, stride=rows), :] = tile.T

    _rwkv_steps(lambda which, j, t: z_ref[which, j, pl.ds(t, 1), :], zv_ref, zy_ref, s_ref, XT, rows)

    for ih in range(rows):
        tile = zy_ref[pl.ds(ih, XT, stride=rows), :].T
        for rep in range(reps):
            e_ref[pl.ds(rep * rows + ih, nbh, stride=RW_HS), :] = tile[rep * nbh:(rep + 1) * nbh]
    for b, h0 in halves:
        y_ref[b, :, h0:h0 + LANES] = e_ref[b * W_BRANCH + h0:b * W_BRANCH + h0 + LANES, :].T


def _rwkv_scan_nat(jv, v, s0):
    nb, t, wq = jv.shape
    rows = s0.shape[1]
    seq = lambda i: (0, i, 0)
    st = lambda i: (0, 0, 0)
    return pl.pallas_call(
        functools.partial(_rwkv_scan_nat_body, nb=nb),
        grid=(t // XT,),
        in_specs=[pl.BlockSpec((nb, XT, wq), seq),
                  pl.BlockSpec((nb, XT, W_BRANCH), seq),
                  pl.BlockSpec((RW_HS, rows, LANES), st)],
        out_specs=[pl.BlockSpec((nb, XT, W_BRANCH), seq),
                   pl.BlockSpec((RW_HS, rows, LANES), st)],
        out_shape=[jax.ShapeDtypeStruct((nb, t, W_BRANCH), F32),
                   jax.ShapeDtypeStruct((RW_HS, rows, LANES), F32)],
        scratch_shapes=[pltpu.VMEM((nb * W_BRANCH, XT), F32),
                        pltpu.VMEM((RW_NJV, RW_HS, XT + Z_PAD_ROWS, LANES), F32),
                        pltpu.VMEM((XT * rows, LANES), F32),
                        pltpu.VMEM((XT * rows, LANES), F32)],
        compiler_params=_cparams("arbitrary"),
        name="rwkv_scan",
    )(jv, v, s0)


def _rwkv_scan_lanes_body(z_ref, zv_ref, s0_ref, zy_ref, s_ref, *, nsteps, rows):
    s_ref[...] = s0_ref[...]
    _rwkv_steps(lambda which, j, t: z_ref[0, which, j, pl.ds(t, 1), :], zv_ref.at[0], zy_ref.at[0],
                s_ref.at[0], nsteps, rows)


def _rwkv_scan_lanes(z, zv, s0):
    g, nq, hs, t, _ = z.shape
    rows = s0.shape[2]
    return pl.pallas_call(
        functools.partial(_rwkv_scan_lanes_body, nsteps=t, rows=rows),
        grid=(g,),
        in_specs=[pl.BlockSpec((1, nq, hs, t, LANES), lambda i: (i, 0, 0, 0, 0)),
                  pl.BlockSpec((1, t * rows, LANES), lambda i: (i, 0, 0)),
                  pl.BlockSpec((1, hs, rows, LANES), lambda i: (i, 0, 0, 0))],
        out_specs=[pl.BlockSpec((1, t * rows, LANES), lambda i: (i, 0, 0)),
                   pl.BlockSpec((1, hs, rows, LANES), lambda i: (i, 0, 0, 0))],
        out_shape=[jax.ShapeDtypeStruct((g, t * rows, LANES), F32),
                   jax.ShapeDtypeStruct((g, hs, rows, LANES), F32)],
        compiler_params=_cparams("parallel"),
        name="rwkv_scan_lanes",
    )(z, zv, s0)


def _lane_split(b):
    seqs = min(b, LANES // RW_HEADS)
    return b // seqs, seqs, LANES // (RW_HEADS * seqs)


def _state_to_lanes(s, g, bl, reps):
    rows = RW_HS // reps
    s = s.reshape(g, bl, RW_HEADS, reps, rows, RW_HS).transpose(0, 5, 4, 3, 1, 2)
    return s.reshape(g, RW_HS, rows, LANES)


def _state_from_lanes(s, bl, reps):
    g, _, rows, _ = s.shape
    s = s.reshape(g, RW_HS, rows, reps, bl, RW_HEADS).transpose(0, 4, 5, 3, 2, 1)
    return s.reshape(g * bl, RW_HEADS, RW_HS, RW_HS)


def _rwkv_scan(jv, v, wkv_s):
    b, t, _ = jv.shape
    g, bl, reps = _lane_split(b)
    s0 = _state_to_lanes(wkv_s, g, bl, reps)
    if g == 1 and t % XT == 0:
        y, s = _rwkv_scan_nat(jv, v, s0[0])
        return y, _state_from_lanes(s[None], bl, reps)
    rows = RW_HS // reps
    z = jv.reshape(g, bl, t, RW_NJV, RW_HEADS, RW_HS).transpose(0, 3, 5, 2, 1, 4)
    z = jnp.broadcast_to(z[:, :, :, :, None], (g, RW_NJV, RW_HS, t, reps, bl, RW_HEADS)).reshape(g, RW_NJV, RW_HS, t, LANES)
    zv = v.reshape(g, bl, t, RW_HEADS, reps, rows).transpose(0, 2, 5, 4, 1, 3).reshape(g, t * rows, LANES)
    zy, s = _rwkv_scan_lanes(z, zv, s0)
    y = zy.reshape(g, t, rows, reps, bl, RW_HEADS).transpose(0, 4, 1, 5, 3, 2).reshape(b, t, W_BRANCH)
    return y, _state_from_lanes(s, bl, reps)


S5_CHUNK_STEPS = 32


def _s5_body(u_ref, xre_ref, xim_ref, bmat_ref, cre_ref, cim_ref, ar_ref, ai_ref,
             d_ref, wglu_ref, bglu_ref, od_ref, hre_out, him_out,
             ut_ref, yt_ref, hre_s, him_s, car_re, car_im, *, nb, tt):
    @pl.when(pl.program_id(1) == 0)
    def _():
        car_re[...] = xre_ref[...]
        car_im[...] = xim_ref[...]

    _to_time_major(u_ref, ut_ref, 0, nb, tt)
    ar = jnp.broadcast_to(ar_ref[...], (nb, S5_W))
    ai = jnp.broadcast_to(ai_ref[...], (nb, S5_W))
    hr = car_re[...]
    hi = car_im[...]
    chunk = min(tt, S5_CHUNK_STEPS)
    for c0 in range(0, tt, chunk):
        bu = _mm(_planes_get(ut_ref, c0 * nb, (c0 + chunk) * nb).astype(BF16), bmat_ref[...])
        for s in range(chunk):
            src = slice(s * nb, (s + 1) * nb)
            hr, hi = (ar * hr - ai * hi + bu[src, :S5_W]), (ar * hi + ai * hr + bu[src, S5_W:])
            dst = slice((c0 + s) * nb, (c0 + s + 1) * nb)
            hre_s[dst, :] = hr
            him_s[dst, :] = hi
    car_re[...] = hr
    car_im[...] = hi
    hre_out[...] = hr
    him_out[...] = hi
    y = _mm(hre_s[...].astype(BF16), cre_ref[...]) - _mm(him_s[...].astype(BF16), cim_ref[...])
    y = y + d_ref[...] * _planes_get(ut_ref, 0, tt * nb)
    z = 0.5 * y * (1.0 + jnp.tanh(math.sqrt(2.0 / math.pi) * (y + 0.044715 * (y * y * y))))
    _planes_set(yt_ref, 0, tt * nb, z * _sigmoid(_mm(z.astype(BF16), wglu_ref[...]) + bglu_ref[...]))
    _from_time_major(yt_ref, od_ref, nb, tt)


def _s5(u, xre, xim, bmat, cre, cim, ar, ai, d, wglu, bglu, nb, tt):
    b, t, w = u.shape
    seq = lambda i, j: (i, j, 0)
    per_b = lambda i, j: (i, 0)
    c2 = lambda i, j: (0, 0)
    rows = tt * nb
    return pl.pallas_call(
        functools.partial(_s5_body, nb=nb, tt=tt),
        grid=(b // nb, t // tt),
        in_specs=[pl.BlockSpec((nb, tt, w), seq),
                  pl.BlockSpec((nb, S5_W), per_b),
                  pl.BlockSpec((nb, S5_W), per_b),
                  pl.BlockSpec(bmat.shape, c2),
                  pl.BlockSpec(cre.shape, c2),
                  pl.BlockSpec(cim.shape, c2),
                  pl.BlockSpec((1, S5_W), c2),
                  pl.BlockSpec((1, S5_W), c2),
                  pl.BlockSpec((1, w), c2),
                  pl.BlockSpec((w, w), c2),
                  pl.BlockSpec((1, w), c2)],
        out_specs=[pl.BlockSpec((nb, tt, w), seq),
                   pl.BlockSpec((nb, S5_W), per_b),
                   pl.BlockSpec((nb, S5_W), per_b)],
        out_shape=[jax.ShapeDtypeStruct((b, t, w), F32),
                   jax.ShapeDtypeStruct((b, S5_W), F32),
                   jax.ShapeDtypeStruct((b, S5_W), F32)],
        scratch_shapes=[pltpu.VMEM((w // LANES, rows, LANES), F32), pltpu.VMEM((w // LANES, rows, LANES), F32),
                        pltpu.VMEM((rows, S5_W), F32), pltpu.VMEM((rows, S5_W), F32),
                        pltpu.VMEM((nb, S5_W), F32), pltpu.VMEM((nb, S5_W), F32)],
        compiler_params=_cparams("parallel", "arbitrary"),
        name="s5",
    )(u, xre, xim, bmat, cre, cim, ar, ai, d, wglu, bglu)


def _s5_params(lre, lim, logdt, b_re, b_im, c_re, c_im):
    dt = jnp.exp(logdt)[:, None]
    mag = jnp.exp(lre * dt)
    ab_re, ab_im = mag * jnp.cos(lim * dt), mag * jnp.sin(lim * dt)
    den = lre * lre + lim * lim
    pr = ab_re - 1.0
    f_re = (pr * lre + ab_im * lim) / den
    f_im = (ab_im * lre - pr * lim) / den
    bb_re = f_re[..., None] * b_re - f_im[..., None] * b_im
    bb_im = f_re[..., None] * b_im + f_im[..., None] * b_re
    eye = jnp.eye(S5_GROUPS, dtype=F32)
    in_re = jnp.einsum('gnc,gh->gchn', bb_re, eye).reshape(W_BRANCH, S5_W)
    in_im = jnp.einsum('gnc,gh->gchn', bb_im, eye).reshape(W_BRANCH, S5_W)
    bmat = jnp.concatenate([in_re, in_im], axis=1).astype(BF16)
    cre = jnp.einsum('gcn,gh->gnhc', c_re, eye).reshape(S5_W, W_BRANCH).astype(BF16)
    cim = jnp.einsum('gcn,gh->gnhc', c_im, eye).reshape(S5_W, W_BRANCH).astype(BF16)
    return bmat, cre, cim, ab_re.reshape(1, S5_W), ab_im.reshape(1, S5_W)


def _outproj_body(x_ref, oa_ref, ob_ref, yc_ref, bonus_ref, od_ref, gate_ref,
                  subln_ref, gnw_ref, gnb_ref, ones_ref, wout_ref, npost_ref, y_ref, *, attn_scale):
    w = W_BRANCH
    ones_blk = ones_ref[...]
    oa = oa_ref[...]
    oa = oa * lax.rsqrt(_segsum(oa * oa, ones_blk) * (1.0 / A_V) + NORM_EPS) * subln_ref[...] * attn_scale
    yc = yc_ref[...]
    mean = _segsum(yc, ones_blk) * (1.0 / RW_HS)
    cen = yc - mean
    var = _segsum(cen * cen, ones_blk) * (1.0 / RW_HS)
    oc = cen * lax.rsqrt(var + RW_GN_EPS) * gnw_ref[...] + gnb_ref[...] + bonus_ref[...]
    gate = gate_ref[...]
    z = None
    for n, branch in enumerate((oa, ob_ref[...], oc, od_ref[...])):
        g = gate[:, n * w:(n + 1) * w]
        part = _mm((branch * (g * _sigmoid(g))).astype(BF16), wout_ref[n * w:(n + 1) * w, :])
        z = part if z is None else z + part
    zn = z * lax.rsqrt(jnp.mean(z * z, axis=-1, keepdims=True) + NORM_EPS) * npost_ref[...]
    y_ref[...] = x_ref[...] + zn


def _outproj(x2, oa, ob, yc, bonus, od, gate, subln, gnw, gnb, ones_blk, wout_bf, npost, attn_scale, tm):
    n = x2.shape[0]
    w = W_BRANCH
    row = lambda i: (i, 0)
    const = lambda i: (0, 0)
    rw = pl.BlockSpec((tm, w), row)
    rd = pl.BlockSpec((tm, D_MODEL), row)
    vec = pl.BlockSpec((1, w), const)
    return pl.pallas_call(
        functools.partial(_outproj_body, attn_scale=attn_scale),
        grid=(n // tm,),
        in_specs=[rd, rw, rw, rw, rw, rw, rd, vec, vec, vec,
                  pl.BlockSpec((w, w), const),
                  pl.BlockSpec((D_MODEL, D_MODEL), const),
                  pl.BlockSpec((1, D_MODEL), const)],
        out_specs=rd,
        out_shape=jax.ShapeDtypeStruct((n, D_MODEL), F32),
        compiler_params=_cparams("parallel"),
        name="outproj",
    )(x2, oa, ob, yc, bonus, od, gate, subln, gnw, gnb, ones_blk, wout_bf, npost)


def _block_diag(blocks):
    n, k, _ = blocks.shape
    eye = jnp.eye(n, dtype=blocks.dtype)
    return jnp.einsum('nij,nm->nimj', blocks, eye).reshape(n * k, n * k)


def _prep_layer(wts, lam_init):
    (norm_pre, norm_post, w_in, w_out, lam_q1, lam_k1, lam_q2, lam_k2, subln_w, conv_w, conv_b,
     lru_wa, lru_ba, lru_wx, lru_bx, lru_lam,
     rw_mu, rw_w0, rw_w1, rw_w2, rw_a0, rw_a1, rw_a2, rw_kk, rw_ka, rw_rk, rw_gnw, rw_gnb,
     s5_lre, s5_lim, s5_logdt, s5_bre, s5_bim, s5_cre, s5_cim, s5_d, s5_wglu, s5_bglu) = wts
    w = W_BRANCH
    row = lambda v: v.reshape(1, -1)
    return dict(
        norm_pre=row(norm_pre), w_in=w_in.astype(BF16), w_out=w_out.astype(BF16), norm_post=row(norm_post),
        w_kvt=w_in[:, OFF_AK:OFF_B].T.astype(BF16),
        lam=(jnp.exp(jnp.sum(lam_q1 * lam_k1)) - jnp.exp(jnp.sum(lam_q2 * lam_k2)) + lam_init).reshape(1),
        attn_scale=1.0 - lam_init, subln=row(jnp.tile(subln_w, A_HEADS)),
        conv_w=conv_w, conv_b=row(conv_b),
        lru_w=jnp.concatenate([_block_diag(lru_wa), _block_diag(lru_wx)], axis=1).astype(BF16),
        lru_b=jnp.concatenate([lru_ba, lru_bx]).reshape(1, 2 * w), lru_lam=row(lru_lam),
        rwkv=(row(rw_mu), row(rw_w0), rw_w1.astype(BF16), rw_w2.astype(BF16), row(rw_a0),
              rw_a1.astype(BF16), rw_a2.astype(BF16), row(rw_kk), row(rw_ka), row(rw_rk)),
        gnw=row(rw_gnw), gnb=row(rw_gnb),
        ones_blk=_block_diag(jnp.ones((RW_HEADS, RW_HS, RW_HS), BF16)),
        s5=_s5_params(s5_lre, s5_lim, s5_logdt, s5_bre, s5_bim, s5_cre, s5_cim)
        + (row(s5_d), s5_wglu.astype(BF16), row(s5_bglu)),
    )


def _layer(x, attend, states, p, tiles):
    conv_buf, lru_h, shift_prev, wkv_s, ssm_x = states
    b, t, _ = x.shape
    n = b * t
    w = W_BRANCH
    tm, bb_proj, tt_proj, bb, tt, nb, tt_b, tt_d, kv_t = tiles

    q, k, v, xb, pc, u, gate = _inproj(x, p["norm_pre"], p["w_in"], p["w_kvt"], bb_proj, tt_proj, kv_t)

    oa = attend(p["lam"], q, k, v)
    if kv_t:
        new_k = k.reshape(b, A_HEADS, A_V, t).transpose(0, 3, 1, 2)
        new_v = v.reshape(b, A_HEADS, A_V, t).transpose(0, 3, 1, 2)
    else:
        new_k, new_v = k.reshape(b, t, A_HEADS, A_V), v.reshape(b, t, A_HEADS, A_V)

    ob, conv_t, new_h = _mixb(xb, conv_buf.transpose(1, 0, 2), lru_h, p["conv_w"], p["conv_b"],
                              p["lru_w"], p["lru_b"], p["lru_lam"], nb, tt_b)
    new_conv = conv_t.transpose(1, 0, 2)

    jv, rv, bonus, new_shift = _rwkv_pre(pc, shift_prev.reshape(b, 1, RW_NPROJ * w),
                                         *p["rwkv"], p["ones_blk"], bb, tt)
    yc, new_s = _rwkv_scan(jv, rv, wkv_s)

    od, hre, him = _s5(u, ssm_x[..., 0].reshape(b, S5_W), ssm_x[..., 1].reshape(b, S5_W),
                       *p["s5"], nb, tt_d)
    new_ssm = jnp.stack([hre.reshape(b, S5_GROUPS, S5_STATE), him.reshape(b, S5_GROUPS, S5_STATE)], axis=-1)

    y = _outproj(x.reshape(n, D_MODEL), oa.reshape(n, w), ob.reshape(n, w), yc.reshape(n, w),
                 bonus.reshape(n, w), od.reshape(n, w), gate.reshape(n, D_MODEL), p["subln"], p["gnw"], p["gnb"],
                 p["ones_blk"], p["w_out"], p["norm_post"], p["attn_scale"], tm)
    new_states = (new_k, new_v, new_conv,
                  new_h.reshape(b, w), new_shift.reshape(b, RW_NPROJ * w), new_s, new_ssm)
    return y.reshape(b, t, D_MODEL), new_states


def kernel(x_prompt, x_sample, cache_k, cache_v, page_table, state_conv, state_lru, state_shift, state_wkv, state_ssm, norm_pre, norm_post, w_in, w_out, lam_q1, lam_k1, lam_q2, lam_k2, subln_w, conv_w, conv_b, lru_wa, lru_ba, lru_wx, lru_bx, lru_lam, rw_mu, rw_w0, rw_w1, rw_w2, rw_a0, rw_a1, rw_a2, rw_kk, rw_ka, rw_rk, rw_gnw, rw_gnb, s5_lre, s5_lim, s5_logdt, s5_bre, s5_bim, s5_cre, s5_cim, s5_d, s5_wglu, s5_bglu):
    weights = (norm_pre, norm_post, w_in, w_out, lam_q1, lam_k1, lam_q2, lam_k2, subln_w,
               conv_w, conv_b, lru_wa, lru_ba, lru_wx, lru_bx, lru_lam,
               rw_mu, rw_w0, rw_w1, rw_w2, rw_a0, rw_a1, rw_a2, rw_kk, rw_ka, rw_rk, rw_gnw, rw_gnb,
               s5_lre, s5_lim, s5_logdt, s5_bre, s5_bim, s5_cre, s5_cim, s5_d, s5_wglu, s5_bglu)
    depth = w_in.shape[0]
    nbp, tp, _ = x_prompt.shape
    nbs, ts, _ = x_sample.shape
    dt = x_prompt.dtype
    w = W_BRANCH
    zero_states = (jnp.zeros((nbp, CONV_W - 1, w), dt), jnp.zeros((nbp, w), dt),
                   jnp.zeros((nbp, RW_NPROJ * w), dt), jnp.zeros((nbp, RW_HEADS, RW_HS, RW_HS), dt),
                   jnp.zeros((nbp, S5_GROUPS, S5_STATE, 2), dt))
    nb = min(SUBLANES, nbp, nbs)
    tiles_p = (min(512, nbp * tp), 1, min(512, tp), 1, min(256, tp), nb, min(256, tp), min(128, tp), True)
    tiles_s = (min(512, nbs * ts), min(64, nbs), ts, min(16, nbs), ts, nb, ts, ts, False)
    tq = min(256, tp)
    n_pool = cache_k.shape[1]
    cache_kt = cache_k.transpose(0, 1, 3, 4, 2).reshape(depth, n_pool, W_BRANCH, PAGE_SIZE)
    cache_vt = cache_v.transpose(0, 1, 3, 4, 2).reshape(depth, n_pool, W_BRANCH, PAGE_SIZE)
    xp, xs = x_prompt, x_sample
    outs_p = [[] for _ in range(7)]
    outs_s = [[] for _ in range(7)]
    for l in range(depth):
        prep = _prep_layer(tuple(wt[l] for wt in weights), 0.8 - 0.6 * math.exp(-0.3 * l))
        xp, st_p = _layer(xp, functools.partial(_attn_prompt, tq=tq), zero_states, prep, tiles_p)
        attend_s = functools.partial(_attn_sample, cache_kt=cache_kt, cache_vt=cache_vt, layer=l,
                                     page_table=page_table, nseq=min(2, nbs))
        st_in = (state_conv[l], state_lru[l], state_shift[l], state_wkv[l], state_ssm[l])
        xs, st_s = _layer(xs, attend_s, st_in, prep, tiles_s)
        for n in range(7):
            outs_p[n].append(st_p[n])
            outs_s[n].append(st_s[n])
    k_p, v_p, conv_p, lru_p, shift_p, wkv_p, ssm_p = [jnp.stack(z) for z in outs_p]
    k_s, v_s, conv_s, lru_s, shift_s, wkv_s, ssm_s = [jnp.stack(z) for z in outs_s]
    return (xp, xs, k_p, k_s, v_p, v_s, conv_p, conv_s, lru_p, lru_s,
            shift_p, shift_s, wkv_p, wkv_s, ssm_p, ssm_s)
```

```python
import functools
import math

import jax
import jax.numpy as jnp
from jax import lax
from jax.experimental import pallas as pl
from jax.experimental.pallas import tpu as pltpu

F32 = jnp.float32
BF16 = jnp.bfloat16

D_MODEL = 1024
W_BRANCH = 256
A_HEADS = 4
A_QK = 32
A_V = 64
PAGE_SIZE = 128
LRU_BLOCKS = 4
LRU_BW = 64
CONV_W = 4
LRU_C = 8.0
RW_HEADS = 4
RW_HS = 64
RW_NPROJ = 5
RW_GN_EPS = RW_HS * 1e-5
S5_CH = 16
S5_GROUPS = 16
S5_STATE = 64
S5_W = S5_GROUPS * S5_STATE
NORM_EPS = 1e-6
NEG_BIG = -1e30

OFF_AQ = 0
OFF_AK = OFF_AQ + W_BRANCH
OFF_AV = OFF_AK + W_BRANCH
OFF_B = OFF_AV + W_BRANCH
OFF_C = OFF_B + W_BRANCH
OFF_D = OFF_C + RW_NPROJ * W_BRANCH
OFF_G = OFF_D + W_BRANCH
D_IN = OFF_G + D_MODEL
_SEGS = ((OFF_AQ, OFF_AK), (OFF_AK, OFF_AV), (OFF_AV, OFF_B), (OFF_B, OFF_C),
         (OFF_C, OFF_D), (OFF_D, OFF_G), (OFF_G, D_IN))

SUBLANES = 8
LANES = 128
VMEM_LIMIT_MIB = 56


def _cparams(*sem):
    return pltpu.CompilerParams(dimension_semantics=sem,
                                vmem_limit_bytes=VMEM_LIMIT_MIB * 1024 * 1024)


def _nt(a, b):
    return lax.dot_general(a, b, (((1,), (1,)), ((), ())), preferred_element_type=F32)


def _mm(a, b):
    return jnp.dot(a, b, preferred_element_type=F32)


def _segsum(x, ones_blk):
    hi = x.astype(BF16)
    lo = (x - hi.astype(F32)).astype(BF16)
    return _mm(hi, ones_blk) + _mm(lo, ones_blk)


def _sigmoid(x):
    return jax.nn.sigmoid(x)


def _softplus(x):
    return jnp.maximum(x, 0.0) + jnp.log1p(jnp.exp(-jnp.abs(x)))


_SEG_K, _SEG_V, _SEG_PC = 1, 2, 4
_N_RWKV_W = 11


def _rwkv_mix(p, prev8, wrefs):
    mu_ref, w0_ref, w1_ref, w2_ref, a0_ref, a1_ref, a2_ref, kkw_ref, ka_ref, rk_ref, ones_ref = wrefs
    bb, tt, _ = p.shape
    w = W_BRANCH
    p_prev = _shift_in(prev8, p, 1)
    xm = (p + (p_prev - p) * mu_ref[...]).reshape(bb * tt, RW_NPROJ * w)
    xr, xw, xk, xv, xa = (xm[:, n * w:(n + 1) * w] for n in range(RW_NPROJ))
    lw = _mm(jnp.tanh(_mm(xw.astype(BF16), w1_ref[...])).astype(BF16), w2_ref[...])
    wlog = -_softplus(-(w0_ref[...] + lw)) - 0.5
    decay = jnp.exp(-jnp.exp(wlog))
    a = _sigmoid(a0_ref[...] + _mm(_mm(xa.astype(BF16), a1_ref[...]).astype(BF16), a2_ref[...]))
    kk = xk * kkw_ref[...]
    ones_blk = ones_ref[...]
    kk = kk / jnp.maximum(jnp.sqrt(_segsum(kk * kk, ones_blk)), 1e-12)
    k = xk * (1.0 + (a - 1.0) * ka_ref[...])
    bonus = _segsum(xr * k * rk_ref[...], ones_blk) * xv
    return (decay, -kk, kk * a, k, xr), xv, bonus


def _inproj_body(x_ref, g_ref, w_ref, wkvt_ref, sh0_ref, *rest, kv_t):
    wrefs = rest[:_N_RWKV_W]
    seg_refs = rest[_N_RWKV_W:_N_RWKV_W + len(_SEGS) - 1]
    jv_ref, rv_ref, bonus_ref, shout_ref, prev_ref = rest[_N_RWKV_W + len(_SEGS) - 1:]
    bb, tt, _ = x_ref.shape
    w = W_BRANCH

    @pl.when(pl.program_id(1) == 0)
    def _():
        prev_ref[...] = jnp.broadcast_to(sh0_ref[...], prev_ref.shape)

    x = x_ref[...].reshape(bb * tt, D_MODEL)
    h = x * lax.rsqrt(jnp.mean(x * x, axis=-1, keepdims=True) + NORM_EPS) * g_ref[...]
    hb = h.astype(BF16)
    segs = [s for n, s in enumerate(_SEGS) if n != _SEG_PC]
    if kv_t:
        kvt = _nt(wkvt_ref[...], hb)
        seg_refs[_SEG_K][0] = kvt[:w]
        seg_refs[_SEG_V][0] = kvt[w:]
    for n, ((lo, hi), o_ref) in enumerate(zip(segs, seg_refs)):
        if not (kv_t and n in (_SEG_K, _SEG_V)):
            o_ref[...] = _mm(hb, w_ref[:, lo:hi]).reshape(bb, tt, hi - lo)

    p = _mm(hb, w_ref[:, OFF_C:OFF_D]).reshape(bb, tt, RW_NPROJ * w)
    jvs, xv, bonus = _rwkv_mix(p, prev_ref[...], wrefs)
    prev_ref[...] = p[:, tt - SUBLANES:tt]
    shout_ref[...] = p[:, tt - 1:tt]
    for n, val in enumerate(jvs):
        jv_ref[:, :, n * w:(n + 1) * w] = val.reshape(bb, tt, w)
    rv_ref[...] = xv.reshape(bb, tt, w)
    bonus_ref[...] = bonus.reshape(bb, tt, w)


def _inproj(x, g, w_bf, wkvt_bf, sh0, rwkv_w, bb, tt, kv_t):
    b, t, _ = x.shape
    assert bb == 1 or not kv_t
    w = W_BRANCH
    wp = RW_NPROJ * w
    seq = lambda i, j: (i, j, 0)
    per_b = lambda i, j: (i, 0, 0)
    const = lambda i, j: (0, 0)
    out_specs, out_shape = [], []
    for n, (lo, hi) in enumerate(s for m, s in enumerate(_SEGS) if m != _SEG_PC):
        if kv_t and n in (_SEG_K, _SEG_V):
            out_specs.append(pl.BlockSpec((1, hi - lo, tt), lambda i, j: (i, 0, j)))
            out_shape.append(jax.ShapeDtypeStruct((b, hi - lo, t), F32))
        else:
            out_specs.append(pl.BlockSpec((bb, tt, hi - lo), seq))
            out_shape.append(jax.ShapeDtypeStruct((b, t, hi - lo), F32))
    for width in (wp, w, w):
        out_specs.append(pl.BlockSpec((bb, tt, width), seq))
        out_shape.append(jax.ShapeDtypeStruct((b, t, width), F32))
    out_specs.append(pl.BlockSpec((bb, 1, wp), per_b))
    out_shape.append(jax.ShapeDtypeStruct((b, 1, wp), F32))
    return pl.pallas_call(
        functools.partial(_inproj_body, kv_t=kv_t),
        grid=(b // bb, t // tt),
        in_specs=[pl.BlockSpec((bb, tt, D_MODEL), seq),
                  pl.BlockSpec((1, D_MODEL), const),
                  pl.BlockSpec((D_MODEL, D_IN), const),
                  pl.BlockSpec((2 * w, D_MODEL), const),
                  pl.BlockSpec((bb, 1, wp), per_b)]
        + [pl.BlockSpec(r.shape, const) for r in rwkv_w],
        out_specs=out_specs,
        out_shape=out_shape,
        scratch_shapes=[pltpu.VMEM((bb, SUBLANES, wp), F32)],
        compiler_params=_cparams("parallel", "arbitrary"),
        name="inproj",
    )(x, g, w_bf, wkvt_bf, sh0, *rwkv_w)


def _attn_prompt_body(lam_ref, q_ref, k_ref, v_ref, o_ref, kb_ref, vb_ref, qm_ref, acc_ref, m_ref, l_ref, *, tq):
    qi = pl.program_id(1)
    nhc = 2 * A_HEADS
    lane = lax.broadcasted_iota(jnp.int32, (1, W_BRANCH), 1)

    @pl.when(qi == 0)
    def _():
        kb_ref[...] = k_ref[0].astype(BF16)
        vb_ref[...] = v_ref[0].astype(BF16)

    q = q_ref[0] * (A_QK ** -0.5 * math.log2(math.e))
    for hc in range(nhc):
        qm_ref[hc * tq:(hc + 1) * tq, :] = jnp.where(lane // A_QK == hc, q, 0.0).astype(BF16)
    m_ref[...] = jnp.full(m_ref.shape, NEG_BIG, F32)
    l_ref[...] = jnp.zeros(l_ref.shape, F32)
    acc_ref[...] = jnp.zeros(acc_ref.shape, F32)

    def wide(x):
        return jnp.concatenate([x, x], axis=1)

    def block(kb, diagonal):
        ks = pl.multiple_of(kb * tq, tq)
        s = _mm(qm_ref[...], kb_ref[:, pl.ds(ks, tq)])
        if diagonal:
            r = lax.broadcasted_iota(jnp.int32, (nhc * tq, 1), 0) % tq
            c = lax.broadcasted_iota(jnp.int32, (1, tq), 1)
            s = jnp.where(c <= r, s, NEG_BIG)
        m_old = m_ref[...]
        m_new = jnp.maximum(m_old, jnp.max(s, axis=-1, keepdims=True))
        alpha = jnp.exp2(m_old - m_new)
        p = jnp.exp2(s - wide(m_new))
        l_ref[...] = alpha * l_ref[...] + jnp.sum(p, axis=-1, keepdims=True)
        m_ref[...] = m_new
        pb = p.astype(BF16)
        a_v = alpha[:, :A_V]
        for h in range(A_HEADS):
            rows = slice(2 * h * tq, (2 * h + 2) * tq)
            pv = _nt(pb[rows], vb_ref[h * A_V:(h + 1) * A_V, pl.ds(ks, tq)])
            acc_ref[rows, :] = a_v[rows] * acc_ref[rows, :] + pv

    def off_diagonal(kb, carry):
        block(kb, False)
        return carry

    lax.fori_loop(0, qi, off_diagonal, 0)
    block(qi, True)
    lam = lam_ref[0]
    inv_l = (1.0 / l_ref[...])[:, :A_V]
    heads = []
    for h in range(A_HEADS):
        r0 = slice(2 * h * tq, (2 * h + 1) * tq)
        r1 = slice((2 * h + 1) * tq, (2 * h + 2) * tq)
        heads.append(acc_ref[r0, :] * inv_l[r0] - lam * (acc_ref[r1, :] * inv_l[r1]))
    o_ref[0] = jnp.concatenate(heads, axis=1)


def _attn_prompt(lam, q, kt, vt, tq):
    b, t, _ = q.shape
    nhc = 2 * A_HEADS
    return pl.pallas_call(
        functools.partial(_attn_prompt_body, tq=tq),
        grid=(b, t // tq),
        in_specs=[pl.BlockSpec(memory_space=pltpu.SMEM),
                  pl.BlockSpec((1, tq, W_BRANCH), lambda i, j: (i, j, 0)),
                  pl.BlockSpec((1, W_BRANCH, t), lambda i, j: (i, 0, 0)),
                  pl.BlockSpec((1, W_BRANCH, t), lambda i, j: (i, 0, 0))],
        out_specs=pl.BlockSpec((1, tq, W_BRANCH), lambda i, j: (i, j, 0)),
        out_shape=jax.ShapeDtypeStruct((b, t, W_BRANCH), F32),
        scratch_shapes=[pltpu.VMEM((W_BRANCH, t), BF16),
                        pltpu.VMEM((W_BRANCH, t), BF16),
                        pltpu.VMEM((nhc * tq, W_BRANCH), BF16),
                        pltpu.VMEM((nhc * tq, A_V), F32),
                        pltpu.VMEM((nhc * tq, LANES), F32),
                        pltpu.VMEM((nhc * tq, LANES), F32)],
        compiler_params=_cparams("parallel", "arbitrary"),
        name="attn_prompt",
    )(lam, q, kt, vt)


def _attn_sample_body(pt_ref, lam_ref, q_ref, kn_ref, vn_ref, *rest, n_pages, tq, nseq):
    del pt_ref
    kp_refs = rest[:nseq * n_pages]
    vp_refs = rest[nseq * n_pages:2 * nseq * n_pages]
    o_ref = rest[2 * nseq * n_pages]
    nhc = 2 * A_HEADS
    lane = lax.broadcasted_iota(jnp.int32, (1, W_BRANCH), 1)
    rowhc = lax.broadcasted_iota(jnp.int32, (nhc * tq, 1), 0) // tq
    tcol = lax.broadcasted_iota(jnp.int32, (nhc * tq, tq), 1)
    trow = lax.broadcasted_iota(jnp.int32, (nhc * tq, tq), 0) % tq
    lam = lam_ref[0]

    def diff(p):
        return jnp.concatenate(
            [p[(2 * h) * tq:(2 * h + 1) * tq] - lam * p[(2 * h + 1) * tq:(2 * h + 2) * tq]
             for h in range(A_HEADS)], axis=0)

    for n in range(nseq):
        kp = kp_refs[n * n_pages:(n + 1) * n_pages]
        vp = vp_refs[n * n_pages:(n + 1) * n_pages]
        q = q_ref[n] * (A_QK ** -0.5)
        qbd = jnp.where(lane // A_QK == rowhc, jnp.tile(q, (nhc, 1)), 0.0).astype(BF16)
        s_past = jnp.concatenate([_mm(qbd, kp[j][0, 0].astype(BF16)) for j in range(n_pages)], axis=1)
        s_new = _nt(qbd, kn_ref[n].astype(BF16))
        s_new = jnp.where(tcol <= trow, s_new, NEG_BIG)
        m = jnp.maximum(jnp.max(s_past, axis=-1, keepdims=True), jnp.max(s_new, axis=-1, keepdims=True))
        p_past = jnp.exp(s_past - m)
        p_new = jnp.exp(s_new - m)
        inv = 1.0 / (jnp.sum(p_past, axis=-1, keepdims=True) + jnp.sum(p_new, axis=-1, keepdims=True))
        w_past = diff(p_past * inv).astype(BF16)
        w_new = diff(p_new * inv).astype(BF16)
        out_all = _mm(w_new, vn_ref[n].astype(BF16))
        for j in range(n_pages):
            out_all = out_all + _nt(w_past[:, j * PAGE_SIZE:(j + 1) * PAGE_SIZE], vp[j][0, 0].astype(BF16))
        out = jnp.zeros((tq, W_BRANCH), F32)
        for h in range(A_HEADS):
            out = out + jnp.where(lane // A_V == h, out_all[h * tq:(h + 1) * tq], 0.0)
        o_ref[n] = out


def _attn_sample(lam, q, k, v, cache_kt, cache_vt, layer, page_table, nseq):
    b, tq, _ = q.shape
    n_pages = page_table.shape[1]
    pt = page_table.reshape(-1)
    seq = pl.BlockSpec((nseq, tq, W_BRANCH), lambda i, pt_r, lam_r: (i, 0, 0))

    def page_spec(n, j):
        return pl.BlockSpec((1, 1, W_BRANCH, PAGE_SIZE),
                            lambda i, pt_r, lam_r: (layer, pt_r[(i * nseq + n) * n_pages + j], 0, 0))

    pages = [page_spec(n, j) for n in range(nseq) for j in range(n_pages)]
    grid_spec = pltpu.PrefetchScalarGridSpec(
        num_scalar_prefetch=2,
        grid=(b // nseq,),
        in_specs=[seq, seq, seq] + pages * 2,
        out_specs=seq,
    )
    return pl.pallas_call(
        functools.partial(_attn_sample_body, n_pages=n_pages, tq=tq, nseq=nseq),
        grid_spec=grid_spec,
        out_shape=jax.ShapeDtypeStruct((b, tq, W_BRANCH), F32),
        compiler_params=_cparams("parallel"),
        name="attn_sample",
    )(pt, lam, q, k, v, *([cache_kt] * (nseq * n_pages)), *([cache_vt] * (nseq * n_pages)))


def _shift_in(prev8, x, s):
    tt = x.shape[1]
    ext = jnp.concatenate([prev8, x], axis=1)
    return pltpu.roll(ext, s, axis=1)[:, SUBLANES:SUBLANES + tt]


def _to_time_major(src_ref, dst_ref, row0, nb, tt):
    for b in range(nb):
        for c in range(dst_ref.shape[0]):
            dst_ref[c, pl.ds(row0 + b, tt, stride=nb), :] = src_ref[b, :, c * LANES:(c + 1) * LANES]


def _from_time_major(src_ref, dst_ref, nb, tt):
    for b in range(nb):
        for c in range(src_ref.shape[0]):
            dst_ref[b, :, c * LANES:(c + 1) * LANES] = src_ref[c, pl.ds(b, tt, stride=nb), :]


def _planes_get(ref, r0, r1):
    return jnp.concatenate([ref[c, r0:r1, :] for c in range(ref.shape[0])], axis=1)


def _planes_set(ref, r0, r1, val):
    for c in range(ref.shape[0]):
        ref[c, r0:r1, :] = val[:, c * LANES:(c + 1) * LANES]


def _mixb_body(xb_ref, cst_ref, h0_ref, cw_ref, cb_ref, wg_ref, bg_ref, lam_ref,
               ob_ref, cout_ref, hout_ref, xt_ref, ht_ref, hcar_ref, *, nb, tt):
    w = W_BRANCH
    rows = tt * nb
    hist = (CONV_W - 1) * nb

    @pl.when(pl.program_id(1) == 0)
    def _():
        _planes_set(xt_ref, 0, hist, cst_ref[...].reshape(hist, w))
        hcar_ref[...] = h0_ref[...]

    _to_time_major(xb_ref, xt_ref, hist, nb, tt)
    cw = cw_ref[...]
    xc = cb_ref[...] + _planes_get(xt_ref, 0, rows) * cw[0:1]
    for s in range(1, CONV_W):
        xc = xc + _planes_get(xt_ref, s * nb, s * nb + rows) * cw[s:s + 1]
    tail = _planes_get(xt_ref, rows, rows + hist)
    cout_ref[...] = tail.reshape(CONV_W - 1, nb, w)
    _planes_set(xt_ref, 0, hist, tail)

    gates = _mm(xc.astype(BF16), wg_ref[...]) + bg_ref[...]
    r = _sigmoid(gates[:, :w])
    i = _sigmoid(gates[:, w:])
    log_a = -LRU_C * r * _softplus(-lam_ref[...])
    a = jnp.exp(log_a)
    b = jnp.sqrt(1.0 - jnp.exp(2.0 * log_a)) * (i * xc)
    h = hcar_ref[...]
    for t in range(tt):
        sl = slice(t * nb, (t + 1) * nb)
        h = a[sl] * h + b[sl]
        _planes_set(ht_ref, t * nb, (t + 1) * nb, h)
    hcar_ref[...] = h
    hout_ref[...] = h
    _from_time_major(ht_ref, ob_ref, nb, tt)


def _mixb(xb, cst, h0, cw, cb, wg, bg, lru_lam, nb, tt):
    b, t, w = xb.shape
    const = lambda i, j: (0, 0)
    return pl.pallas_call(
        functools.partial(_mixb_body, nb=nb, tt=tt),
        grid=(b // nb, t // tt),
        in_specs=[pl.BlockSpec((nb, tt, w), lambda i, j: (i, j, 0)),
                  pl.BlockSpec((CONV_W - 1, nb, w), lambda i, j: (0, i, 0)),
                  pl.BlockSpec((nb, w), lambda i, j: (i, 0)),
                  pl.BlockSpec((CONV_W, w), const),
                  pl.BlockSpec((1, w), const),
                  pl.BlockSpec((w, 2 * w), const),
                  pl.BlockSpec((1, 2 * w), const),
                  pl.BlockSpec((1, w), const)],
        out_specs=[pl.BlockSpec((nb, tt, w), lambda i, j: (i, j, 0)),
                   pl.BlockSpec((CONV_W - 1, nb, w), lambda i, j: (0, i, 0)),
                   pl.BlockSpec((nb, w), lambda i, j: (i, 0))],
        out_shape=[jax.ShapeDtypeStruct((b, t, w), F32),
                   jax.ShapeDtypeStruct((CONV_W - 1, b, w), F32),
                   jax.ShapeDtypeStruct((b, w), F32)],
        scratch_shapes=[pltpu.VMEM((w // LANES, (tt + CONV_W - 1) * nb, LANES), F32),
                        pltpu.VMEM((w // LANES, tt * nb, LANES), F32),
                        pltpu.VMEM((nb, w), F32)],
        compiler_params=_cparams("parallel", "arbitrary"),
        name="conv_rglru",
    )(xb, cst, h0, cw, cb, wg, bg, lru_lam)


_JV_DECAY, _JV_NKK, _JV_KKA, _JV_K, _JV_R = range(5)
RW_NJV = 5
XT = LANES
RELAY_UNROLL = 8
Z_PAD_ROWS = SUBLANES


def _tree_sum(parts):
    while len(parts) > 1:
        parts = [parts[n] + parts[n + 1] for n in range(0, len(parts), 2)]
    return parts[0]


def _rwkv_steps(zrow, zv_ref, zy_ref, s_ref, nsteps, rows, n_acc=4):
    def accumulate(parts, j, term):
        parts[j % n_acc] = term if parts[j % n_acc] is None else parts[j % n_acc] + term

    sa_parts = [None] * n_acc
    for j in range(RW_HS):
        accumulate(sa_parts, j, s_ref[j] * zrow(_JV_NKK, j, 0))

    def step(t, sa):
        r0 = pl.multiple_of(t * rows, rows)
        v = zv_ref[pl.ds(r0, rows), :]
        t_next = jnp.minimum(t + 1, nsteps - 1)
        y_parts = [None] * n_acc
        sa_parts = [None] * n_acc
        for j in range(RW_HS):
            s = s_ref[j] * zrow(_JV_DECAY, j, t) + sa * zrow(_JV_KKA, j, t) + v * zrow(_JV_K, j, t)
            s_ref[j] = s
            accumulate(y_parts, j, s * zrow(_JV_R, j, t))
            accumulate(sa_parts, j, s * zrow(_JV_NKK, j, t_next))
        zy_ref[pl.ds(r0, rows), :] = _tree_sum(y_parts)
        return _tree_sum(sa_parts)

    lax.fori_loop(0, nsteps, step, _tree_sum(sa_parts))


def _rwkv_scan_nat_body(jv_ref, v_ref, s0_ref, y_ref, s_ref, e_ref, z_ref, zv_ref, zy_ref, *, nb):
    reps = LANES // (nb * RW_HEADS)
    rows = RW_HS // reps
    nbh = nb * RW_HEADS

    @pl.when(pl.program_id(0) == 0)
    def _():
        s_ref[...] = s0_ref[...]

    halves = [(b, h0) for b in range(nb) for h0 in range(0, W_BRANCH, LANES)]

    def fill_e(src_ref, lane0):
        for b, h0 in halves:
            e_ref[b * W_BRANCH + h0:b * W_BRANCH + h0 + LANES, :] = src_ref[b, :, lane0 + h0:lane0 + h0 + LANES].T

    for q in range(RW_NJV):
        fill_e(jv_ref, q * W_BRANCH)

        def relay(jg, carry, q=q):
            for jj in range(RELAY_UNROLL):
                j = jg * RELAY_UNROLL + jj
                r = e_ref[pl.ds(j, nbh, stride=RW_HS), :]
                z_ref[q, j, 0:XT, :] = jnp.concatenate([r] * reps, axis=0).T
            return carry

        lax.fori_loop(0, RW_HS // RELAY_UNROLL, relay, 0)
    fill_e(v_ref, 0)
    for ih in range(rows):
        tile = jnp.concatenate([e_ref[pl.ds(rep * rows + ih, nbh, stride=RW_HS), :] for rep in range(reps)], axis=0)
        zv_ref[pl.ds(ih, XT, stride=rows), :] = tile.T

    _rwkv_steps(lambda which, j, t: z_ref[which, j, pl.ds(t, 1), :], zv_ref, zy_ref, s_ref, XT, rows)

    for ih in range(rows):
        tile = zy_ref[pl.ds(ih, XT, stride=rows), :].T
        for rep in range(reps):
            e_ref[pl.ds(rep * rows + ih, nbh, stride=RW_HS), :] = tile[rep * nbh:(rep + 1) * nbh]
    for b, h0 in halves:
        y_ref[b, :, h0:h0 + LANES] = e_ref[b * W_BRANCH + h0:b * W_BRANCH + h0 + LANES, :].T


def _rwkv_scan_nat(jv, v, s0):
    nb, t, wq = jv.shape
    rows = s0.shape[1]
    seq = lambda i: (0, i, 0)
    st = lambda i: (0, 0, 0)
    return pl.pallas_call(
        functools.partial(_rwkv_scan_nat_body, nb=nb),
        grid=(t // XT,),
        in_specs=[pl.BlockSpec((nb, XT, wq), seq),
                  pl.BlockSpec((nb, XT, W_BRANCH), seq),
                  pl.BlockSpec((RW_HS, rows, LANES), st)],
        out_specs=[pl.BlockSpec((nb, XT, W_BRANCH), seq),
                   pl.BlockSpec((RW_HS, rows, LANES), st)],
        out_shape=[jax.ShapeDtypeStruct((nb, t, W_BRANCH), F32),
                   jax.ShapeDtypeStruct((RW_HS, rows, LANES), F32)],
        scratch_shapes=[pltpu.VMEM((nb * W_BRANCH, XT), F32),
                        pltpu.VMEM((RW_NJV, RW_HS, XT + Z_PAD_ROWS, LANES), F32),
                        pltpu.VMEM((XT * rows, LANES), F32),
                        pltpu.VMEM((XT * rows, LANES), F32)],
        compiler_params=_cparams("arbitrary"),
        name="rwkv_scan",
    )(jv, v, s0)


def _rwkv_scan_lanes_body(z_ref, zv_ref, s0_ref, zy_ref, s_ref, *, nsteps, rows):
    s_ref[...] = s0_ref[...]
    _rwkv_steps(lambda which, j, t: z_ref[0, which, j, pl.ds(t, 1), :], zv_ref.at[0], zy_ref.at[0],
                s_ref.at[0], nsteps, rows)


def _rwkv_scan_lanes(z, zv, s0):
    g, nq, hs, t, _ = z.shape
    rows = s0.shape[2]
    return pl.pallas_call(
        functools.partial(_rwkv_scan_lanes_body, nsteps=t, rows=rows),
        grid=(g,),
        in_specs=[pl.BlockSpec((1, nq, hs, t, LANES), lambda i: (i, 0, 0, 0, 0)),
                  pl.BlockSpec((1, t * rows, LANES), lambda i: (i, 0, 0)),
                  pl.BlockSpec((1, hs, rows, LANES), lambda i: (i, 0, 0, 0))],
        out_specs=[pl.BlockSpec((1, t * rows, LANES), lambda i: (i, 0, 0)),
                   pl.BlockSpec((1, hs, rows, LANES), lambda i: (i, 0, 0, 0))],
        out_shape=[jax.ShapeDtypeStruct((g, t * rows, LANES), F32),
                   jax.ShapeDtypeStruct((g, hs, rows, LANES), F32)],
        compiler_params=_cparams("parallel"),
        name="rwkv_scan_lanes",
    )(z, zv, s0)


def _lane_split(b):
    seqs = min(b, LANES // RW_HEADS)
    return b // seqs, seqs, LANES // (RW_HEADS * seqs)


def _state_to_lanes(s, g, bl, reps):
    rows = RW_HS // reps
    s = s.reshape(g, bl, RW_HEADS, reps, rows, RW_HS).transpose(0, 5, 4, 3, 1, 2)
    return s.reshape(g, RW_HS, rows, LANES)


def _state_from_lanes(s, bl, reps):
    g, _, rows, _ = s.shape
    s = s.reshape(g, RW_HS, rows, reps, bl, RW_HEADS).transpose(0, 4, 5, 3, 2, 1)
    return s.reshape(g * bl, RW_HEADS, RW_HS, RW_HS)


def _rwkv_scan(jv, v, wkv_s):
    b, t, _ = jv.shape
    g, bl, reps = _lane_split(b)
    s0 = _state_to_lanes(wkv_s, g, bl, reps)
    if g == 1 and t % XT == 0:
        y, s = _rwkv_scan_nat(jv, v, s0[0])
        return y, _state_from_lanes(s[None], bl, reps)
    rows = RW_HS // reps
    z = jv.reshape(g, bl, t, RW_NJV, RW_HEADS, RW_HS).transpose(0, 3, 5, 2, 1, 4)
    z = jnp.broadcast_to(z[:, :, :, :, None], (g, RW_NJV, RW_HS, t, reps, bl, RW_HEADS)).reshape(g, RW_NJV, RW_HS, t, LANES)
    zv = v.reshape(g, bl, t, RW_HEADS, reps, rows).transpose(0, 2, 5, 4, 1, 3).reshape(g, t * rows, LANES)
    zy, s = _rwkv_scan_lanes(z, zv, s0)
    y = zy.reshape(g, t, rows, reps, bl, RW_HEADS).transpose(0, 4, 1, 5, 3, 2).reshape(b, t, W_BRANCH)
    return y, _state_from_lanes(s, bl, reps)


S5_CHUNK_STEPS = 32


def _s5_body(u_ref, xre_ref, xim_ref, bmat_ref, cre_ref, cim_ref, ar_ref, ai_ref,
             d_ref, wglu_ref, bglu_ref, od_ref, hre_out, him_out,
             ut_ref, yt_ref, hre_s, him_s, car_re, car_im, *, nb, tt):
    @pl.when(pl.program_id(1) == 0)
    def _():
        car_re[...] = xre_ref[...]
        car_im[...] = xim_ref[...]

    _to_time_major(u_ref, ut_ref, 0, nb, tt)
    ar = jnp.broadcast_to(ar_ref[...], (nb, S5_W))
    ai = jnp.broadcast_to(ai_ref[...], (nb, S5_W))
    hr = car_re[...]
    hi = car_im[...]
    chunk = min(tt, S5_CHUNK_STEPS)
    for c0 in range(0, tt, chunk):
        bu = _mm(_planes_get(ut_ref, c0 * nb, (c0 + chunk) * nb).astype(BF16), bmat_ref[...])
        for s in range(chunk):
            src = slice(s * nb, (s + 1) * nb)
            hr, hi = (ar * hr - ai * hi + bu[src, :S5_W]), (ar * hi + ai * hr + bu[src, S5_W:])
            dst = slice((c0 + s) * nb, (c0 + s + 1) * nb)
            hre_s[dst, :] = hr
            him_s[dst, :] = hi
    car_re[...] = hr
    car_im[...] = hi
    hre_out[...] = hr
    him_out[...] = hi
    y = _mm(hre_s[...].astype(BF16), cre_ref[...]) - _mm(him_s[...].astype(BF16), cim_ref[...])
    y = y + d_ref[...] * _planes_get(ut_ref, 0, tt * nb)
    z = 0.5 * y * (1.0 + jnp.tanh(math.sqrt(2.0 / math.pi) * (y + 0.044715 * (y * y * y))))
    _planes_set(yt_ref, 0, tt * nb, z * _sigmoid(_mm(z.astype(BF16), wglu_ref[...]) + bglu_ref[...]))
    _from_time_major(yt_ref, od_ref, nb, tt)


def _s5(u, xre, xim, bmat, cre, cim, ar, ai, d, wglu, bglu, nb, tt):
    b, t, w = u.shape
    seq = lambda i, j: (i, j, 0)
    per_b = lambda i, j: (i, 0)
    c2 = lambda i, j: (0, 0)
    rows = tt * nb
    return pl.pallas_call(
        functools.partial(_s5_body, nb=nb, tt=tt),
        grid=(b // nb, t // tt),
        in_specs=[pl.BlockSpec((nb, tt, w), seq),
                  pl.BlockSpec((nb, S5_W), per_b),
                  pl.BlockSpec((nb, S5_W), per_b),
                  pl.BlockSpec(bmat.shape, c2),
                  pl.BlockSpec(cre.shape, c2),
                  pl.BlockSpec(cim.shape, c2),
                  pl.BlockSpec((1, S5_W), c2),
                  pl.BlockSpec((1, S5_W), c2),
                  pl.BlockSpec((1, w), c2),
                  pl.BlockSpec((w, w), c2),
                  pl.BlockSpec((1, w), c2)],
        out_specs=[pl.BlockSpec((nb, tt, w), seq),
                   pl.BlockSpec((nb, S5_W), per_b),
                   pl.BlockSpec((nb, S5_W), per_b)],
        out_shape=[jax.ShapeDtypeStruct((b, t, w), F32),
                   jax.ShapeDtypeStruct((b, S5_W), F32),
                   jax.ShapeDtypeStruct((b, S5_W), F32)],
        scratch_shapes=[pltpu.VMEM((w // LANES, rows, LANES), F32), pltpu.VMEM((w // LANES, rows, LANES), F32),
                        pltpu.VMEM((rows, S5_W), F32), pltpu.VMEM((rows, S5_W), F32),
                        pltpu.VMEM((nb, S5_W), F32), pltpu.VMEM((nb, S5_W), F32)],
        compiler_params=_cparams("parallel", "arbitrary"),
        name="s5",
    )(u, xre, xim, bmat, cre, cim, ar, ai, d, wglu, bglu)


def _s5_params(lre, lim, logdt, b_re, b_im, c_re, c_im):
    dt = jnp.exp(logdt)[:, None]
    mag = jnp.exp(lre * dt)
    ab_re, ab_im = mag * jnp.cos(lim * dt), mag * jnp.sin(lim * dt)
    den = lre * lre + lim * lim
    pr = ab_re - 1.0
    f_re = (pr * lre + ab_im * lim) / den
    f_im = (ab_im * lre - pr * lim) / den
    bb_re = f_re[..., None] * b_re - f_im[..., None] * b_im
    bb_im = f_re[..., None] * b_im + f_im[..., None] * b_re
    eye = jnp.eye(S5_GROUPS, dtype=F32)
    in_re = jnp.einsum('gnc,gh->gchn', bb_re, eye).reshape(W_BRANCH, S5_W)
    in_im = jnp.einsum('gnc,gh->gchn', bb_im, eye).reshape(W_BRANCH, S5_W)
    bmat = jnp.concatenate([in_re, in_im], axis=1).astype(BF16)
    cre = jnp.einsum('gcn,gh->gnhc', c_re, eye).reshape(S5_W, W_BRANCH).astype(BF16)
    cim = jnp.einsum('gcn,gh->gnhc', c_im, eye).reshape(S5_W, W_BRANCH).astype(BF16)
    return bmat, cre, cim, ab_re.reshape(1, S5_W), ab_im.reshape(1, S5_W)


def _outproj_body(x_ref, oa_ref, ob_ref, yc_ref, bonus_ref, od_ref, gate_ref,
                  subln_ref, gnw_ref, gnb_ref, ones_ref, wout_ref, npost_ref, y_ref, *, attn_scale):
    w = W_BRANCH
    ones_blk = ones_ref[...]
    oa = oa_ref[...]
    oa = oa * lax.rsqrt(_segsum(oa * oa, ones_blk) * (1.0 / A_V) + NORM_EPS) * subln_ref[...] * attn_scale
    yc = yc_ref[...]
    mean = _segsum(yc, ones_blk) * (1.0 / RW_HS)
    cen = yc - mean
    var = _segsum(cen * cen, ones_blk) * (1.0 / RW_HS)
    oc = cen * lax.rsqrt(var + RW_GN_EPS) * gnw_ref[...] + gnb_ref[...] + bonus_ref[...]
    gate = gate_ref[...]
    z = None
    for n, branch in enumerate((oa, ob_ref[...], oc, od_ref[...])):
        g = gate[:, n * w:(n + 1) * w]
        part = _mm((branch * (g * _sigmoid(g))).astype(BF16), wout_ref[n * w:(n + 1) * w, :])
        z = part if z is None else z + part
    zn = z * lax.rsqrt(jnp.mean(z * z, axis=-1, keepdims=True) + NORM_EPS) * npost_ref[...]
    y_ref[...] = x_ref[...] + zn


def _outproj(x2, oa, ob, yc, bonus, od, gate, subln, gnw, gnb, ones_blk, wout_bf, npost, attn_scale, tm):
    n = x2.shape[0]
    w = W_BRANCH
    row = lambda i: (i, 0)
    const = lambda i: (0, 0)
    rw = pl.BlockSpec((tm, w), row)
    rd = pl.BlockSpec((tm, D_MODEL), row)
    vec = pl.BlockSpec((1, w), const)
    return pl.pallas_call(
        functools.partial(_outproj_body, attn_scale=attn_scale),
        grid=(n // tm,),
        in_specs=[rd, rw, rw, rw, rw, rw, rd, vec, vec, vec,
                  pl.BlockSpec((w, w), const),
                  pl.BlockSpec((D_MODEL, D_MODEL), const),
                  pl.BlockSpec((1, D_MODEL), const)],
        out_specs=rd,
        out_shape=jax.ShapeDtypeStruct((n, D_MODEL), F32),
        compiler_params=_cparams("parallel"),
        name="outproj",
    )(x2, oa, ob, yc, bonus, od, gate, subln, gnw, gnb, ones_blk, wout_bf, npost)


def _block_diag(blocks):
    n, k, _ = blocks.shape
    eye = jnp.eye(n, dtype=blocks.dtype)
    return jnp.einsum('nij,nm->nimj', blocks, eye).reshape(n * k, n * k)


def _prep_layer(wts, lam_init):
    (norm_pre, norm_post, w_in, w_out, lam_q1, lam_k1, lam_q2, lam_k2, subln_w, conv_w, conv_b,
     lru_wa, lru_ba, lru_wx, lru_bx, lru_lam,
     rw_mu, rw_w0, rw_w1, rw_w2, rw_a0, rw_a1, rw_a2, rw_kk, rw_ka, rw_rk, rw_gnw, rw_gnb,
     s5_lre, s5_lim, s5_logdt, s5_bre, s5_bim, s5_cre, s5_cim, s5_d, s5_wglu, s5_bglu) = wts
    w = W_BRANCH
    row = lambda v: v.reshape(1, -1)
    return dict(
        norm_pre=row(norm_pre), w_in=w_in.astype(BF16), w_out=w_out.astype(BF16), norm_post=row(norm_post),
        w_kvt=w_in[:, OFF_AK:OFF_B].T.astype(BF16),
        lam=(jnp.exp(jnp.sum(lam_q1 * lam_k1)) - jnp.exp(jnp.sum(lam_q2 * lam_k2)) + lam_init).reshape(1),
        attn_scale=1.0 - lam_init, subln=row(jnp.tile(subln_w, A_HEADS)),
        conv_w=conv_w, conv_b=row(conv_b),
        lru_w=jnp.concatenate([_block_diag(lru_wa), _block_diag(lru_wx)], axis=1).astype(BF16),
        lru_b=jnp.concatenate([lru_ba, lru_bx]).reshape(1, 2 * w), lru_lam=row(lru_lam),
        rwkv=(row(rw_mu), row(rw_w0), rw_w1.astype(BF16), rw_w2.astype(BF16), row(rw_a0),
              rw_a1.astype(BF16), rw_a2.astype(BF16), row(rw_kk), row(rw_ka), row(rw_rk)),
        gnw=row(rw_gnw), gnb=row(rw_gnb),
        ones_blk=_block_diag(jnp.ones((RW_HEADS, RW_HS, RW_HS), BF16)),
        s5=_s5_params(s5_lre, s5_lim, s5_logdt, s5_bre, s5_bim, s5_cre, s5_cim)
        + (row(s5_d), s5_wglu.astype(BF16), row(s5_bglu)),
    )


def _layer(x, attend, states, p, tiles):
    conv_buf, lru_h, shift_prev, wkv_s, ssm_x = states
    b, t, _ = x.shape
    n = b * t
    w = W_BRANCH
    tm, bb_proj, tt_proj, nb, tt_b, tt_d, kv_t = tiles

    q, k, v, xb, u, gate, jv, rv, bonus, new_shift = _inproj(
        x, p["norm_pre"], p["w_in"], p["w_kvt"], shift_prev.reshape(b, 1, RW_NPROJ * w),
        p["rwkv"] + (p["ones_blk"],), bb_proj, tt_proj, kv_t)

    oa = attend(p["lam"], q, k, v)
    if kv_t:
        new_k = k.reshape(b, A_HEADS, A_V, t).transpose(0, 3, 1, 2)
        new_v = v.reshape(b, A_HEADS, A_V, t).transpose(0, 3, 1, 2)
    else:
        new_k, new_v = k.reshape(b, t, A_HEADS, A_V), v.reshape(b, t, A_HEADS, A_V)

    ob, conv_t, new_h = _mixb(xb, conv_buf.transpose(1, 0, 2), lru_h, p["conv_w"], p["conv_b"],
                              p["lru_w"], p["lru_b"], p["lru_lam"], nb, tt_b)
    new_conv = conv_t.transpose(1, 0, 2)

    yc, new_s = _rwkv_scan(jv, rv, wkv_s)

    od, hre, him = _s5(u, ssm_x[..., 0].reshape(b, S5_W), ssm_x[..., 1].reshape(b, S5_W),
                       *p["s5"], nb, tt_d)
    new_ssm = jnp.stack([hre.reshape(b, S5_GROUPS, S5_STATE), him.reshape(b, S5_GROUPS, S5_STATE)], axis=-1)

    y = _outproj(x.reshape(n, D_MODEL), oa.reshape(n, w), ob.reshape(n, w), yc.reshape(n, w),
                 bonus.reshape(n, w), od.reshape(n, w), gate.reshape(n, D_MODEL), p["subln"], p["gnw"], p["gnb"],
                 p["ones_blk"], p["w_out"], p["norm_post"], p["attn_scale"], tm)
    new_states = (new_k, new_v, new_conv,
                  new_h.reshape(b, w), new_shift.reshape(b, RW_NPROJ * w), new_s, new_ssm)
    return y.reshape(b, t, D_MODEL), new_states


def kernel(x_prompt, x_sample, cache_k, cache_v, page_table, state_conv, state_lru, state_shift, state_wkv, state_ssm, norm_pre, norm_post, w_in, w_out, lam_q1, lam_k1, lam_q2, lam_k2, subln_w, conv_w, conv_b, lru_wa, lru_ba, lru_wx, lru_bx, lru_lam, rw_mu, rw_w0, rw_w1, rw_w2, rw_a0, rw_a1, rw_a2, rw_kk, rw_ka, rw_rk, rw_gnw, rw_gnb, s5_lre, s5_lim, s5_logdt, s5_bre, s5_bim, s5_cre, s5_cim, s5_d, s5_wglu, s5_bglu):
    weights = (norm_pre, norm_post, w_in, w_out, lam_q1, lam_k1, lam_q2, lam_k2, subln_w,
               conv_w, conv_b, lru_wa, lru_ba, lru_wx, lru_bx, lru_lam,
               rw_mu, rw_w0, rw_w1, rw_w2, rw_a0, rw_a1, rw_a2, rw_kk, rw_ka, rw_rk, rw_gnw, rw_gnb,
               s5_lre, s5_lim, s5_logdt, s5_bre, s5_bim, s5_cre, s5_cim, s5_d, s5_wglu, s5_bglu)
    depth = w_in.shape[0]
    nbp, tp, _ = x_prompt.shape
    nbs, ts, _ = x_sample.shape
    dt = x_prompt.dtype
    w = W_BRANCH
    zero_states = (jnp.zeros((nbp, CONV_W - 1, w), dt), jnp.zeros((nbp, w), dt),
                   jnp.zeros((nbp, RW_NPROJ * w), dt), jnp.zeros((nbp, RW_HEADS, RW_HS, RW_HS), dt),
                   jnp.zeros((nbp, S5_GROUPS, S5_STATE, 2), dt))
    nb = min(SUBLANES, nbp, nbs)
    tiles_p = (min(512, nbp * tp), 1, min(512, tp), nb, min(256, tp), min(128, tp), True)
    tiles_s = (min(512, nbs * ts), min(64, nbs), ts, nb, ts, ts, False)
    tq = min(256, tp)
    n_pool = cache_k.shape[1]
    cache_kt = cache_k.transpose(0, 1, 3, 4, 2).reshape(depth, n_pool, W_BRANCH, PAGE_SIZE)
    cache_vt = cache_v.transpose(0, 1, 3, 4, 2).reshape(depth, n_pool, W_BRANCH, PAGE_SIZE)
    xp, xs = x_prompt, x_sample
    outs_p = [[] for _ in range(7)]
    outs_s = [[] for _ in range(7)]
    for l in range(depth):
        prep = _prep_layer(tuple(wt[l] for wt in weights), 0.8 - 0.6 * math.exp(-0.3 * l))
        xp, st_p = _layer(xp, functools.partial(_attn_prompt, tq=tq), zero_states, prep, tiles_p)
        attend_s = functools.partial(_attn_sample, cache_kt=cache_kt, cache_vt=cache_vt, layer=l,
                                     page_table=page_table, nseq=min(2, nbs))
        st_in = (state_conv[l], state_lru[l], state_shift[l], state_wkv[l], state_ssm[l])
        xs, st_s = _layer(xs, attend_s, st_in, prep, tiles_s)
        for n in range(7):
            outs_p[n].append(st_p[n])
            outs_s[n].append(st_s[n])
    k_p, v_p, conv_p, lru_p, shift_p, wkv_p, ssm_p = [jnp.stack(z) for z in outs_p]
    k_s, v_s, conv_s, lru_s, shift_s, wkv_s, ssm_s = [jnp.stack(z) for z in outs_s]
    return (xp, xs, k_p, k_s, v_p, v_s, conv_p, conv_s, lru_p, lru_s,
            shift_p, shift_s, wkv_p, wkv_s, ssm_p, ssm_s)
```

```python
import functools
import math

import jax
import jax.numpy as jnp
from jax import lax
from jax.experimental import pallas as pl
from jax.experimental.pallas import tpu as pltpu

F32 = jnp.float32
BF16 = jnp.bfloat16

D_MODEL = 1024
W_BRANCH = 256
A_HEADS = 4
A_QK = 32
A_V = 64
PAGE_SIZE = 128
LRU_BLOCKS = 4
LRU_BW = 64
CONV_W = 4
LRU_C = 8.0
RW_HEADS = 4
RW_HS = 64
RW_NPROJ = 5
RW_GN_EPS = RW_HS * 1e-5
S5_CH = 16
S5_GROUPS = 16
S5_STATE = 64
S5_W = S5_GROUPS * S5_STATE
NORM_EPS = 1e-6
NEG_BIG = -1e30

OFF_AQ = 0
OFF_AK = OFF_AQ + W_BRANCH
OFF_AV = OFF_AK + W_BRANCH
OFF_B = OFF_AV + W_BRANCH
OFF_C = OFF_B + W_BRANCH
OFF_D = OFF_C + RW_NPROJ * W_BRANCH
OFF_G = OFF_D + W_BRANCH
D_IN = OFF_G + D_MODEL
_SEGS = ((OFF_AQ, OFF_AK), (OFF_AK, OFF_AV), (OFF_AV, OFF_B), (OFF_B, OFF_C),
         (OFF_C, OFF_D), (OFF_D, OFF_G), (OFF_G, D_IN))

SUBLANES = 8
LANES = 128
VMEM_LIMIT_MIB = 56


def _cparams(*sem):
    return pltpu.CompilerParams(dimension_semantics=sem,
                                vmem_limit_bytes=VMEM_LIMIT_MIB * 1024 * 1024)


def _nt(a, b):
    return lax.dot_general(a, b, (((1,), (1,)), ((), ())), preferred_element_type=F32)


def _mm(a, b):
    return jnp.dot(a, b, preferred_element_type=F32)


def _segsum(x, ones_blk):
    hi = x.astype(BF16)
    lo = (x - hi.astype(F32)).astype(BF16)
    return _mm(hi, ones_blk) + _mm(lo, ones_blk)


def _sigmoid(x):
    return jax.nn.sigmoid(x)


def _softplus(x):
    return jnp.maximum(x, 0.0) + jnp.log1p(jnp.exp(-jnp.abs(x)))


_SEG_K, _SEG_V, _SEG_PC = 1, 2, 4
_N_RWKV_W = 11


def _rwkv_mix(p, prev8, wrefs):
    mu_ref, w0_ref, w1_ref, w2_ref, a0_ref, a1_ref, a2_ref, kkw_ref, ka_ref, rk_ref, ones_ref = wrefs
    bb, tt, _ = p.shape
    w = W_BRANCH
    p_prev = _shift_in(prev8, p, 1)
    xm = (p + (p_prev - p) * mu_ref[...]).reshape(bb * tt, RW_NPROJ * w)
    xr, xw, xk, xv, xa = (xm[:, n * w:(n + 1) * w] for n in range(RW_NPROJ))
    lw = _mm(jnp.tanh(_mm(xw.astype(BF16), w1_ref[...])).astype(BF16), w2_ref[...])
    wlog = -_softplus(-(w0_ref[...] + lw)) - 0.5
    decay = jnp.exp(-jnp.exp(wlog))
    a = _sigmoid(a0_ref[...] + _mm(_mm(xa.astype(BF16), a1_ref[...]).astype(BF16), a2_ref[...]))
    kk = xk * kkw_ref[...]
    ones_blk = ones_ref[...]
    kk = kk / jnp.maximum(jnp.sqrt(_segsum(kk * kk, ones_blk)), 1e-12)
    k = xk * (1.0 + (a - 1.0) * ka_ref[...])
    bonus = _segsum(xr * k * rk_ref[...], ones_blk) * xv
    return (decay, -kk, kk * a, k, xr), xv, bonus


def _inproj_body(x_ref, g_ref, w_ref, wkvt_ref, sh0_ref, *rest, kv_t):
    wrefs = rest[:_N_RWKV_W]
    seg_refs = rest[_N_RWKV_W:_N_RWKV_W + len(_SEGS) - 1]
    jv_ref, rv_ref, bonus_ref, shout_ref, prev_ref = rest[_N_RWKV_W + len(_SEGS) - 1:]
    bb, tt, _ = x_ref.shape
    w = W_BRANCH

    @pl.when(pl.program_id(1) == 0)
    def _():
        prev_ref[...] = jnp.broadcast_to(sh0_ref[...], prev_ref.shape)

    x = x_ref[...].reshape(bb * tt, D_MODEL)
    h = x * lax.rsqrt(jnp.mean(x * x, axis=-1, keepdims=True) + NORM_EPS) * g_ref[...]
    hb = h.astype(BF16)
    segs = [s for n, s in enumerate(_SEGS) if n != _SEG_PC]
    if kv_t:
        kvt = _nt(wkvt_ref[...], hb)
        seg_refs[_SEG_K][0] = kvt[:w]
        seg_refs[_SEG_V][0] = kvt[w:]
    for n, ((lo, hi), o_ref) in enumerate(zip(segs, seg_refs)):
        if not (kv_t and n in (_SEG_K, _SEG_V)):
            o_ref[...] = _mm(hb, w_ref[:, lo:hi]).reshape(bb, tt, hi - lo)

    p = _mm(hb, w_ref[:, OFF_C:OFF_D]).reshape(bb, tt, RW_NPROJ * w)
    jvs, xv, bonus = _rwkv_mix(p, prev_ref[...], wrefs)
    prev_ref[...] = p[:, tt - SUBLANES:tt]
    shout_ref[...] = p[:, tt - 1:tt]
    if kv_t:
        for n, val in enumerate(jvs):
            jv_ref[n] = val.T
        rv_ref[...] = xv.T
    else:
        for n, val in enumerate(jvs):
            jv_ref[:, :, n * w:(n + 1) * w] = val.reshape(bb, tt, w)
        rv_ref[...] = xv.reshape(bb, tt, w)
    bonus_ref[...] = bonus.reshape(bb, tt, w)


def _inproj(x, g, w_bf, wkvt_bf, sh0, rwkv_w, bb, tt, kv_t):
    b, t, _ = x.shape
    assert bb == 1 or not kv_t
    w = W_BRANCH
    wp = RW_NPROJ * w
    seq = lambda i, j: (i, j, 0)
    per_b = lambda i, j: (i, 0, 0)
    const = lambda i, j: (0, 0)
    out_specs, out_shape = [], []
    for n, (lo, hi) in enumerate(s for m, s in enumerate(_SEGS) if m != _SEG_PC):
        if kv_t and n in (_SEG_K, _SEG_V):
            out_specs.append(pl.BlockSpec((1, hi - lo, tt), lambda i, j: (i, 0, j)))
            out_shape.append(jax.ShapeDtypeStruct((b, hi - lo, t), F32))
        else:
            out_specs.append(pl.BlockSpec((bb, tt, hi - lo), seq))
            out_shape.append(jax.ShapeDtypeStruct((b, t, hi - lo), F32))
    if kv_t:
        out_specs += [pl.BlockSpec((RW_NJV, w, tt), lambda i, j: (0, i, j)),
                      pl.BlockSpec((w, tt), lambda i, j: (i, j))]
        out_shape += [jax.ShapeDtypeStruct((RW_NJV, b * w, t), F32), jax.ShapeDtypeStruct((b * w, t), F32)]
    else:
        out_specs += [pl.BlockSpec((bb, tt, wp), seq), pl.BlockSpec((bb, tt, w), seq)]
        out_shape += [jax.ShapeDtypeStruct((b, t, wp), F32), jax.ShapeDtypeStruct((b, t, w), F32)]
    out_specs.append(pl.BlockSpec((bb, tt, w), seq))
    out_shape.append(jax.ShapeDtypeStruct((b, t, w), F32))
    out_specs.append(pl.BlockSpec((bb, 1, wp), per_b))
    out_shape.append(jax.ShapeDtypeStruct((b, 1, wp), F32))
    return pl.pallas_call(
        functools.partial(_inproj_body, kv_t=kv_t),
        grid=(b // bb, t // tt),
        in_specs=[pl.BlockSpec((bb, tt, D_MODEL), seq),
                  pl.BlockSpec((1, D_MODEL), const),
                  pl.BlockSpec((D_MODEL, D_IN), const),
                  pl.BlockSpec((2 * w, D_MODEL), const),
                  pl.BlockSpec((bb, 1, wp), per_b)]
        + [pl.BlockSpec(r.shape, const) for r in rwkv_w],
        out_specs=out_specs,
        out_shape=out_shape,
        scratch_shapes=[pltpu.VMEM((bb, SUBLANES, wp), F32)],
        compiler_params=_cparams("parallel", "arbitrary"),
        name="inproj",
    )(x, g, w_bf, wkvt_bf, sh0, *rwkv_w)


def _attn_prompt_body(lam_ref, q_ref, k_ref, v_ref, o_ref, kb_ref, vb_ref, qm_ref, acc_ref, m_ref, l_ref, *, tq):
    qi = pl.program_id(1)
    nhc = 2 * A_HEADS
    lane = lax.broadcasted_iota(jnp.int32, (1, W_BRANCH), 1)

    @pl.when(qi == 0)
    def _():
        kb_ref[...] = k_ref[0].astype(BF16)
        vb_ref[...] = v_ref[0].astype(BF16)

    q = q_ref[0] * (A_QK ** -0.5 * math.log2(math.e))
    for hc in range(nhc):
        qm_ref[hc * tq:(hc + 1) * tq, :] = jnp.where(lane // A_QK == hc, q, 0.0).astype(BF16)
    m_ref[...] = jnp.full(m_ref.shape, NEG_BIG, F32)
    l_ref[...] = jnp.zeros(l_ref.shape, F32)
    acc_ref[...] = jnp.zeros(acc_ref.shape, F32)

    def block(kb, width, diagonal):
        ks = pl.multiple_of(kb * tq, tq)
        s = _mm(qm_ref[...], kb_ref[:, pl.ds(ks, width)])
        if diagonal:
            r = lax.broadcasted_iota(jnp.int32, (nhc * tq, 1), 0) % tq
            c = lax.broadcasted_iota(jnp.int32, (1, tq), 1)
            s = jnp.where(c <= r, s, NEG_BIG)
        m_old = m_ref[...]
        m_new = jnp.maximum(m_old, jnp.max(s, axis=-1, keepdims=True))
        alpha = jnp.exp2(m_old - m_new)
        p = jnp.exp2(s - jnp.concatenate([m_new] * (width // LANES), axis=1))
        l_ref[...] = alpha * l_ref[...] + jnp.sum(p, axis=-1, keepdims=True)
        m_ref[...] = m_new
        pb = p.astype(BF16)
        a_v = alpha[:, :A_V]
        for h in range(A_HEADS):
            rows = slice(2 * h * tq, (2 * h + 2) * tq)
            pv = _nt(pb[rows], vb_ref[h * A_V:(h + 1) * A_V, pl.ds(ks, width)])
            acc_ref[rows, :] = a_v[rows] * acc_ref[rows, :] + pv

    def off_diagonal_pair(kp, carry):
        block(2 * kp, 2 * tq, False)
        return carry

    lax.fori_loop(0, qi // 2, off_diagonal_pair, 0)

    @pl.when(qi % 2 == 1)
    def _():
        block(qi - 1, tq, False)

    block(qi, tq, True)
    lam = lam_ref[0]
    inv_l = (1.0 / l_ref[...])[:, :A_V]
    heads = []
    for h in range(A_HEADS):
        r0 = slice(2 * h * tq, (2 * h + 1) * tq)
        r1 = slice((2 * h + 1) * tq, (2 * h + 2) * tq)
        heads.append(acc_ref[r0, :] * inv_l[r0] - lam * (acc_ref[r1, :] * inv_l[r1]))
    o_ref[0] = jnp.concatenate(heads, axis=1)


def _attn_prompt(lam, q, kt, vt, tq):
    b, t, _ = q.shape
    nhc = 2 * A_HEADS
    return pl.pallas_call(
        functools.partial(_attn_prompt_body, tq=tq),
        grid=(b, t // tq),
        in_specs=[pl.BlockSpec(memory_space=pltpu.SMEM),
                  pl.BlockSpec((1, tq, W_BRANCH), lambda i, j: (i, j, 0)),
                  pl.BlockSpec((1, W_BRANCH, t), lambda i, j: (i, 0, 0)),
                  pl.BlockSpec((1, W_BRANCH, t), lambda i, j: (i, 0, 0))],
        out_specs=pl.BlockSpec((1, tq, W_BRANCH), lambda i, j: (i, j, 0)),
        out_shape=jax.ShapeDtypeStruct((b, t, W_BRANCH), F32),
        scratch_shapes=[pltpu.VMEM((W_BRANCH, t), BF16),
                        pltpu.VMEM((W_BRANCH, t), BF16),
                        pltpu.VMEM((nhc * tq, W_BRANCH), BF16),
                        pltpu.VMEM((nhc * tq, A_V), F32),
                        pltpu.VMEM((nhc * tq, LANES), F32),
                        pltpu.VMEM((nhc * tq, LANES), F32)],
        compiler_params=_cparams("parallel", "arbitrary"),
        name="attn_prompt",
    )(lam, q, kt, vt)


def _attn_sample_body(pt_ref, lam_ref, q_ref, kn_ref, vn_ref, *rest, n_pages, tq, nseq):
    del pt_ref
    kp_refs = rest[:nseq * n_pages]
    vp_refs = rest[nseq * n_pages:2 * nseq * n_pages]
    o_ref = rest[2 * nseq * n_pages]
    nhc = 2 * A_HEADS
    lane = lax.broadcasted_iota(jnp.int32, (1, W_BRANCH), 1)
    rowhc = lax.broadcasted_iota(jnp.int32, (nhc * tq, 1), 0) // tq
    tcol = lax.broadcasted_iota(jnp.int32, (nhc * tq, tq), 1)
    trow = lax.broadcasted_iota(jnp.int32, (nhc * tq, tq), 0) % tq
    lam = lam_ref[0]

    def diff(p):
        return jnp.concatenate(
            [p[(2 * h) * tq:(2 * h + 1) * tq] - lam * p[(2 * h + 1) * tq:(2 * h + 2) * tq]
             for h in range(A_HEADS)], axis=0)

    for n in range(nseq):
        kp = kp_refs[n * n_pages:(n + 1) * n_pages]
        vp = vp_refs[n * n_pages:(n + 1) * n_pages]
        q = q_ref[n] * (A_QK ** -0.5)
        qbd = jnp.where(lane // A_QK == rowhc, jnp.tile(q, (nhc, 1)), 0.0).astype(BF16)
        s_past = jnp.concatenate([_mm(qbd, kp[j][0, 0].astype(BF16)) for j in range(n_pages)], axis=1)
        s_new = _nt(qbd, kn_ref[n].astype(BF16))
        s_new = jnp.where(tcol <= trow, s_new, NEG_BIG)
        m = jnp.maximum(jnp.max(s_past, axis=-1, keepdims=True), jnp.max(s_new, axis=-1, keepdims=True))
        p_past = jnp.exp(s_past - m)
        p_new = jnp.exp(s_new - m)
        inv = 1.0 / (jnp.sum(p_past, axis=-1, keepdims=True) + jnp.sum(p_new, axis=-1, keepdims=True))
        w_past = diff(p_past * inv).astype(BF16)
        w_new = diff(p_new * inv).astype(BF16)
        out_all = _mm(w_new, vn_ref[n].astype(BF16))
        for j in range(n_pages):
            out_all = out_all + _nt(w_past[:, j * PAGE_SIZE:(j + 1) * PAGE_SIZE], vp[j][0, 0].astype(BF16))
        out = jnp.zeros((tq, W_BRANCH), F32)
        for h in range(A_HEADS):
            out = out + jnp.where(lane // A_V == h, out_all[h * tq:(h + 1) * tq], 0.0)
        o_ref[n] = out


def _attn_sample(lam, q, k, v, cache_kt, cache_vt, layer, page_table, nseq):
    b, tq, _ = q.shape
    n_pages = page_table.shape[1]
    pt = page_table.reshape(-1)
    seq = pl.BlockSpec((nseq, tq, W_BRANCH), lambda i, pt_r, lam_r: (i, 0, 0))

    def page_spec(n, j):
        return pl.BlockSpec((1, 1, W_BRANCH, PAGE_SIZE),
                            lambda i, pt_r, lam_r: (layer, pt_r[(i * nseq + n) * n_pages + j], 0, 0))

    pages = [page_spec(n, j) for n in range(nseq) for j in range(n_pages)]
    grid_spec = pltpu.PrefetchScalarGridSpec(
        num_scalar_prefetch=2,
        grid=(b // nseq,),
        in_specs=[seq, seq, seq] + pages * 2,
        out_specs=seq,
    )
    return pl.pallas_call(
        functools.partial(_attn_sample_body, n_pages=n_pages, tq=tq, nseq=nseq),
        grid_spec=grid_spec,
        out_shape=jax.ShapeDtypeStruct((b, tq, W_BRANCH), F32),
        compiler_params=_cparams("parallel"),
        name="attn_sample",
    )(pt, lam, q, k, v, *([cache_kt] * (nseq * n_pages)), *([cache_vt] * (nseq * n_pages)))


def _shift_in(prev8, x, s):
    tt = x.shape[1]
    ext = jnp.concatenate([prev8, x], axis=1)
    return pltpu.roll(ext, s, axis=1)[:, SUBLANES:SUBLANES + tt]


def _to_time_major(src_ref, dst_ref, row0, nb, tt):
    for b in range(nb):
        for c in range(dst_ref.shape[0]):
            dst_ref[c, pl.ds(row0 + b, tt, stride=nb), :] = src_ref[b, :, c * LANES:(c + 1) * LANES]


def _from_time_major(src_ref, dst_ref, nb, tt):
    for b in range(nb):
        for c in range(src_ref.shape[0]):
            dst_ref[b, :, c * LANES:(c + 1) * LANES] = src_ref[c, pl.ds(b, tt, stride=nb), :]


def _planes_get(ref, r0, r1):
    return jnp.concatenate([ref[c, r0:r1, :] for c in range(ref.shape[0])], axis=1)


def _planes_set(ref, r0, r1, val):
    for c in range(ref.shape[0]):
        ref[c, r0:r1, :] = val[:, c * LANES:(c + 1) * LANES]


def _mixb_body(xb_ref, cst_ref, h0_ref, cw_ref, cb_ref, wg_ref, bg_ref, lam_ref,
               ob_ref, cout_ref, hout_ref, xt_ref, ht_ref, hcar_ref, *, nb, tt):
    w = W_BRANCH
    rows = tt * nb
    hist = (CONV_W - 1) * nb

    @pl.when(pl.program_id(1) == 0)
    def _():
        _planes_set(xt_ref, 0, hist, cst_ref[...].reshape(hist, w))
        hcar_ref[...] = h0_ref[...]

    _to_time_major(xb_ref, xt_ref, hist, nb, tt)
    cw = cw_ref[...]
    xc = cb_ref[...] + _planes_get(xt_ref, 0, rows) * cw[0:1]
    for s in range(1, CONV_W):
        xc = xc + _planes_get(xt_ref, s * nb, s * nb + rows) * cw[s:s + 1]
    tail = _planes_get(xt_ref, rows, rows + hist)
    cout_ref[...] = tail.reshape(CONV_W - 1, nb, w)
    _planes_set(xt_ref, 0, hist, tail)

    gates = _mm(xc.astype(BF16), wg_ref[...]) + bg_ref[...]
    r = _sigmoid(gates[:, :w])
    i = _sigmoid(gates[:, w:])
    log_a = -LRU_C * r * _softplus(-lam_ref[...])
    a = jnp.exp(log_a)
    b = jnp.sqrt(1.0 - jnp.exp(2.0 * log_a)) * (i * xc)
    h = hcar_ref[...]
    for t in range(tt):
        sl = slice(t * nb, (t + 1) * nb)
        h = a[sl] * h + b[sl]
        _planes_set(ht_ref, t * nb, (t + 1) * nb, h)
    hcar_ref[...] = h
    hout_ref[...] = h
    _from_time_major(ht_ref, ob_ref, nb, tt)


def _mixb(xb, cst, h0, cw, cb, wg, bg, lru_lam, nb, tt):
    b, t, w = xb.shape
    const = lambda i, j: (0, 0)
    return pl.pallas_call(
        functools.partial(_mixb_body, nb=nb, tt=tt),
        grid=(b // nb, t // tt),
        in_specs=[pl.BlockSpec((nb, tt, w), lambda i, j: (i, j, 0)),
                  pl.BlockSpec((CONV_W - 1, nb, w), lambda i, j: (0, i, 0)),
                  pl.BlockSpec((nb, w), lambda i, j: (i, 0)),
                  pl.BlockSpec((CONV_W, w), const),
                  pl.BlockSpec((1, w), const),
                  pl.BlockSpec((w, 2 * w), const),
                  pl.BlockSpec((1, 2 * w), const),
                  pl.BlockSpec((1, w), const)],
        out_specs=[pl.BlockSpec((nb, tt, w), lambda i, j: (i, j, 0)),
                   pl.BlockSpec((CONV_W - 1, nb, w), lambda i, j: (0, i, 0)),
                   pl.BlockSpec((nb, w), lambda i, j: (i, 0))],
        out_shape=[jax.ShapeDtypeStruct((b, t, w), F32),
                   jax.ShapeDtypeStruct((CONV_W - 1, b, w), F32),
                   jax.ShapeDtypeStruct((b, w), F32)],
        scratch_shapes=[pltpu.VMEM((w // LANES, (tt + CONV_W - 1) * nb, LANES), F32),
                        pltpu.VMEM((w // LANES, tt * nb, LANES), F32),
                        pltpu.VMEM((nb, w), F32)],
        compiler_params=_cparams("parallel", "arbitrary"),
        name="conv_rglru",
    )(xb, cst, h0, cw, cb, wg, bg, lru_lam)


_JV_DECAY, _JV_NKK, _JV_KKA, _JV_K, _JV_R = range(5)
RW_NJV = 5
XT = LANES
RELAY_UNROLL = 8
Z_PAD_ROWS = SUBLANES


def _tree_sum(parts):
    while len(parts) > 1:
        parts = [parts[n] + parts[n + 1] for n in range(0, len(parts), 2)]
    return parts[0]


def _rwkv_steps(zrow, zv_ref, zy_ref, s_ref, nsteps, rows, n_acc=4):
    def accumulate(parts, j, term):
        parts[j % n_acc] = term if parts[j % n_acc] is None else parts[j % n_acc] + term

    sa_parts = [None] * n_acc
    for j in range(RW_HS):
        accumulate(sa_parts, j, s_ref[j] * zrow(_JV_NKK, j, 0))

    def step(t, sa):
        r0 = pl.multiple_of(t * rows, rows)
        v = zv_ref[pl.ds(r0, rows), :]
        t_next = jnp.minimum(t + 1, nsteps - 1)
        y_parts = [None] * n_acc
        sa_parts = [None] * n_acc
        for j in range(RW_HS):
            s = s_ref[j] * zrow(_JV_DECAY, j, t) + sa * zrow(_JV_KKA, j, t) + v * zrow(_JV_K, j, t)
            s_ref[j] = s
            accumulate(y_parts, j, s * zrow(_JV_R, j, t))
            accumulate(sa_parts, j, s * zrow(_JV_NKK, j, t_next))
        zy_ref[pl.ds(r0, rows), :] = _tree_sum(y_parts)
        return _tree_sum(sa_parts)

    lax.fori_loop(0, nsteps, step, _tree_sum(sa_parts))


def _rwkv_scan_fm_body(jv_ref, v_ref, s0_ref, y_ref, s_ref, z_ref, zv_ref, zy_ref, *, nb):
    reps = LANES // (nb * RW_HEADS)
    rows = RW_HS // reps
    nbh = nb * RW_HEADS

    @pl.when(pl.program_id(0) == 0)
    def _():
        s_ref[...] = s0_ref[...]

    for q in range(RW_NJV):
        def relay(jg, carry, q=q):
            for jj in range(RELAY_UNROLL):
                j = jg * RELAY_UNROLL + jj
                r = jv_ref[q, pl.ds(j, nbh, stride=RW_HS), :]
                z_ref[q, j, 0:XT, :] = jnp.concatenate([r] * reps, axis=0).T
            return carry

        lax.fori_loop(0, RW_HS // RELAY_UNROLL, relay, 0)
    for ih in range(rows):
        tile = jnp.concatenate([v_ref[pl.ds(rep * rows + ih, nbh, stride=RW_HS), :] for rep in range(reps)], axis=0)
        zv_ref[pl.ds(ih, XT, stride=rows), :] = tile.T

    _rwkv_steps(lambda which, j, t: z_ref[which, j, pl.ds(t, 1), :], zv_ref, zy_ref, s_ref, XT, rows)

    for ih in range(rows):
        tile = zy_ref[pl.ds(ih, XT, stride=rows), :].T
        for rep in range(reps):
            y_ref[pl.ds(rep * rows + ih, nbh, stride=RW_HS), :] = tile[rep * nbh:(rep + 1) * nbh]


def _rwkv_scan_fm(jv, v, s0):
    nq, nrow, t = jv.shape
    rows = s0.shape[1]
    st = lambda i: (0, 0, 0)
    return pl.pallas_call(
        functools.partial(_rwkv_scan_fm_body, nb=nrow // W_BRANCH),
        grid=(t // XT,),
        in_specs=[pl.BlockSpec((nq, nrow, XT), lambda i: (0, 0, i)),
                  pl.BlockSpec((nrow, XT), lambda i: (0, i)),
                  pl.BlockSpec((RW_HS, rows, LANES), st)],
        out_specs=[pl.BlockSpec((nrow, XT), lambda i: (0, i)),
                   pl.BlockSpec((RW_HS, rows, LANES), st)],
        out_shape=[jax.ShapeDtypeStruct((nrow, t), F32),
                   jax.ShapeDtypeStruct((RW_HS, rows, LANES), F32)],
        scratch_shapes=[pltpu.VMEM((RW_NJV, RW_HS, XT + Z_PAD_ROWS, LANES), F32),
                        pltpu.VMEM((XT * rows, LANES), F32),
                        pltpu.VMEM((XT * rows, LANES), F32)],
        compiler_params=_cparams("arbitrary"),
        name="rwkv_scan",
    )(jv, v, s0)


def _rwkv_scan_lanes_body(z_ref, zv_ref, s0_ref, zy_ref, s_ref, *, nsteps, rows):
    s_ref[...] = s0_ref[...]
    _rwkv_steps(lambda which, j, t: z_ref[0, which, j, pl.ds(t, 1), :], zv_ref.at[0], zy_ref.at[0],
                s_ref.at[0], nsteps, rows)


def _rwkv_scan_lanes(z, zv, s0):
    g, nq, hs, t, _ = z.shape
    rows = s0.shape[2]
    return pl.pallas_call(
        functools.partial(_rwkv_scan_lanes_body, nsteps=t, rows=rows),
        grid=(g,),
        in_specs=[pl.BlockSpec((1, nq, hs, t, LANES), lambda i: (i, 0, 0, 0, 0)),
                  pl.BlockSpec((1, t * rows, LANES), lambda i: (i, 0, 0)),
                  pl.BlockSpec((1, hs, rows, LANES), lambda i: (i, 0, 0, 0))],
        out_specs=[pl.BlockSpec((1, t * rows, LANES), lambda i: (i, 0, 0)),
                   pl.BlockSpec((1, hs, rows, LANES), lambda i: (i, 0, 0, 0))],
        out_shape=[jax.ShapeDtypeStruct((g, t * rows, LANES), F32),
                   jax.ShapeDtypeStruct((g, hs, rows, LANES), F32)],
        compiler_params=_cparams("parallel"),
        name="rwkv_scan_lanes",
    )(z, zv, s0)


def _lane_split(b):
    seqs = min(b, LANES // RW_HEADS)
    return b // seqs, seqs, LANES // (RW_HEADS * seqs)


def _state_to_lanes(s, g, bl, reps):
    rows = RW_HS // reps
    s = s.reshape(g, bl, RW_HEADS, reps, rows, RW_HS).transpose(0, 5, 4, 3, 1, 2)
    return s.reshape(g, RW_HS, rows, LANES)


def _state_from_lanes(s, bl, reps):
    g, _, rows, _ = s.shape
    s = s.reshape(g, RW_HS, rows, reps, bl, RW_HEADS).transpose(0, 4, 5, 3, 2, 1)
    return s.reshape(g * bl, RW_HEADS, RW_HS, RW_HS)


def _rwkv_scan(jv, v, wkv_s, feature_major):
    b = wkv_s.shape[0]
    g, bl, reps = _lane_split(b)
    s0 = _state_to_lanes(wkv_s, g, bl, reps)
    if feature_major:
        assert g == 1 and jv.shape[-1] % XT == 0
        y, s = _rwkv_scan_fm(jv, v, s0[0])
        return y, _state_from_lanes(s[None], bl, reps)
    t = jv.shape[1]
    rows = RW_HS // reps
    z = jv.reshape(g, bl, t, RW_NJV, RW_HEADS, RW_HS).transpose(0, 3, 5, 2, 1, 4)
    z = jnp.broadcast_to(z[:, :, :, :, None], (g, RW_NJV, RW_HS, t, reps, bl, RW_HEADS)).reshape(g, RW_NJV, RW_HS, t, LANES)
    zv = v.reshape(g, bl, t, RW_HEADS, reps, rows).transpose(0, 2, 5, 4, 1, 3).reshape(g, t * rows, LANES)
    zy, s = _rwkv_scan_lanes(z, zv, s0)
    y = zy.reshape(g, t, rows, reps, bl, RW_HEADS).transpose(0, 4, 1, 5, 3, 2).reshape(b, t, W_BRANCH)
    return y, _state_from_lanes(s, bl, reps)


S5_CHUNK_STEPS = 32


def _s5_body(u_ref, xre_ref, xim_ref, bmat_ref, cre_ref, cim_ref, ar_ref, ai_ref,
             d_ref, wglu_ref, bglu_ref, od_ref, hre_out, him_out,
             ut_ref, yt_ref, hre_s, him_s, car_re, car_im, *, nb, tt):
    @pl.when(pl.program_id(1) == 0)
    def _():
        car_re[...] = xre_ref[...]
        car_im[...] = xim_ref[...]

    _to_time_major(u_ref, ut_ref, 0, nb, tt)
    ar = jnp.broadcast_to(ar_ref[...], (nb, S5_W))
    ai = jnp.broadcast_to(ai_ref[...], (nb, S5_W))
    hr = car_re[...]
    hi = car_im[...]
    chunk = min(tt, S5_CHUNK_STEPS)
    for c0 in range(0, tt, chunk):
        bu = _mm(_planes_get(ut_ref, c0 * nb, (c0 + chunk) * nb).astype(BF16), bmat_ref[...])
        for s in range(chunk):
            src = slice(s * nb, (s + 1) * nb)
            hr, hi = (ar * hr - ai * hi + bu[src, :S5_W]), (ar * hi + ai * hr + bu[src, S5_W:])
            dst = slice((c0 + s) * nb, (c0 + s + 1) * nb)
            hre_s[dst, :] = hr
            him_s[dst, :] = hi
    car_re[...] = hr
    car_im[...] = hi
    hre_out[...] = hr
    him_out[...] = hi
    y = _mm(hre_s[...].astype(BF16), cre_ref[...]) - _mm(him_s[...].astype(BF16), cim_ref[...])
    y = y + d_ref[...] * _planes_get(ut_ref, 0, tt * nb)
    z = 0.5 * y * (1.0 + jnp.tanh(math.sqrt(2.0 / math.pi) * (y + 0.044715 * (y * y * y))))
    _planes_set(yt_ref, 0, tt * nb, z * _sigmoid(_mm(z.astype(BF16), wglu_ref[...]) + bglu_ref[...]))
    _from_time_major(yt_ref, od_ref, nb, tt)


def _s5(u, xre, xim, bmat, cre, cim, ar, ai, d, wglu, bglu, nb, tt):
    b, t, w = u.shape
    seq = lambda i, j: (i, j, 0)
    per_b = lambda i, j: (i, 0)
    c2 = lambda i, j: (0, 0)
    rows = tt * nb
    return pl.pallas_call(
        functools.partial(_s5_body, nb=nb, tt=tt),
        grid=(b // nb, t // tt),
        in_specs=[pl.BlockSpec((nb, tt, w), seq),
                  pl.BlockSpec((nb, S5_W), per_b),
                  pl.BlockSpec((nb, S5_W), per_b),
                  pl.BlockSpec(bmat.shape, c2),
                  pl.BlockSpec(cre.shape, c2),
                  pl.BlockSpec(cim.shape, c2),
                  pl.BlockSpec((1, S5_W), c2),
                  pl.BlockSpec((1, S5_W), c2),
                  pl.BlockSpec((1, w), c2),
                  pl.BlockSpec((w, w), c2),
                  pl.BlockSpec((1, w), c2)],
        out_specs=[pl.BlockSpec((nb, tt, w), seq),
                   pl.BlockSpec((nb, S5_W), per_b),
                   pl.BlockSpec((nb, S5_W), per_b)],
        out_shape=[jax.ShapeDtypeStruct((b, t, w), F32),
                   jax.ShapeDtypeStruct((b, S5_W), F32),
                   jax.ShapeDtypeStruct((b, S5_W), F32)],
        scratch_shapes=[pltpu.VMEM((w // LANES, rows, LANES), F32), pltpu.VMEM((w // LANES, rows, LANES), F32),
                        pltpu.VMEM((rows, S5_W), F32), pltpu.VMEM((rows, S5_W), F32),
                        pltpu.VMEM((nb, S5_W), F32), pltpu.VMEM((nb, S5_W), F32)],
        compiler_params=_cparams("parallel", "arbitrary"),
        name="s5",
    )(u, xre, xim, bmat, cre, cim, ar, ai, d, wglu, bglu)


def _s5_params(lre, lim, logdt, b_re, b_im, c_re, c_im):
    dt = jnp.exp(logdt)[:, None]
    mag = jnp.exp(lre * dt)
    ab_re, ab_im = mag * jnp.cos(lim * dt), mag * jnp.sin(lim * dt)
    den = lre * lre + lim * lim
    pr = ab_re - 1.0
    f_re = (pr * lre + ab_im * lim) / den
    f_im = (ab_im * lre - pr * lim) / den
    bb_re = f_re[..., None] * b_re - f_im[..., None] * b_im
    bb_im = f_re[..., None] * b_im + f_im[..., None] * b_re
    eye = jnp.eye(S5_GROUPS, dtype=F32)
    in_re = jnp.einsum('gnc,gh->gchn', bb_re, eye).reshape(W_BRANCH, S5_W)
    in_im = jnp.einsum('gnc,gh->gchn', bb_im, eye).reshape(W_BRANCH, S5_W)
    bmat = jnp.concatenate([in_re, in_im], axis=1).astype(BF16)
    cre = jnp.einsum('gcn,gh->gnhc', c_re, eye).reshape(S5_W, W_BRANCH).astype(BF16)
    cim = jnp.einsum('gcn,gh->gnhc', c_im, eye).reshape(S5_W, W_BRANCH).astype(BF16)
    return bmat, cre, cim, ab_re.reshape(1, S5_W), ab_im.reshape(1, S5_W)


def _outproj_body(x_ref, oa_ref, ob_ref, yc_ref, bonus_ref, od_ref, gate_ref,
                  subln_ref, gnw_ref, gnb_ref, ones_ref, wout_ref, npost_ref, y_ref, *, attn_scale, yc_t):
    w = W_BRANCH
    ones_blk = ones_ref[...]
    oa = oa_ref[...]
    oa = oa * lax.rsqrt(_segsum(oa * oa, ones_blk) * (1.0 / A_V) + NORM_EPS) * subln_ref[...] * attn_scale
    yc = yc_ref[...].T if yc_t else yc_ref[...]
    mean = _segsum(yc, ones_blk) * (1.0 / RW_HS)
    cen = yc - mean
    var = _segsum(cen * cen, ones_blk) * (1.0 / RW_HS)
    oc = cen * lax.rsqrt(var + RW_GN_EPS) * gnw_ref[...] + gnb_ref[...] + bonus_ref[...]
    gate = gate_ref[...]
    z = None
    for n, branch in enumerate((oa, ob_ref[...], oc, od_ref[...])):
        g = gate[:, n * w:(n + 1) * w]
        part = _mm((branch * (g * _sigmoid(g))).astype(BF16), wout_ref[n * w:(n + 1) * w, :])
        z = part if z is None else z + part
    zn = z * lax.rsqrt(jnp.mean(z * z, axis=-1, keepdims=True) + NORM_EPS) * npost_ref[...]
    y_ref[...] = x_ref[...] + zn


def _outproj(x2, oa, ob, yc, bonus, od, gate, subln, gnw, gnb, ones_blk, wout_bf, npost, attn_scale, tm, yc_t):
    n = x2.shape[0]
    w = W_BRANCH
    row = lambda i: (i, 0)
    const = lambda i: (0, 0)
    rw = pl.BlockSpec((tm, w), row)
    rd = pl.BlockSpec((tm, D_MODEL), row)
    vec = pl.BlockSpec((1, w), const)
    if yc_t:
        tiles_per_seq = yc.shape[1] // tm
        yc_spec = pl.BlockSpec((w, tm), lambda i: (i // tiles_per_seq, i % tiles_per_seq))
    else:
        yc_spec = rw
    return pl.pallas_call(
        functools.partial(_outproj_body, attn_scale=attn_scale, yc_t=yc_t),
        grid=(n // tm,),
        in_specs=[rd, rw, rw, yc_spec, rw, rw, rd, vec, vec, vec,
                  pl.BlockSpec((w, w), const),
                  pl.BlockSpec((D_MODEL, D_MODEL), const),
                  pl.BlockSpec((1, D_MODEL), const)],
        out_specs=rd,
        out_shape=jax.ShapeDtypeStruct((n, D_MODEL), F32),
        compiler_params=_cparams("parallel"),
        name="outproj",
    )(x2, oa, ob, yc, bonus, od, gate, subln, gnw, gnb, ones_blk, wout_bf, npost)


def _block_diag(blocks):
    n, k, _ = blocks.shape
    eye = jnp.eye(n, dtype=blocks.dtype)
    return jnp.einsum('nij,nm->nimj', blocks, eye).reshape(n * k, n * k)


def _prep_layer(wts, lam_init):
    (norm_pre, norm_post, w_in, w_out, lam_q1, lam_k1, lam_q2, lam_k2, subln_w, conv_w, conv_b,
     lru_wa, lru_ba, lru_wx, lru_bx, lru_lam,
     rw_mu, rw_w0, rw_w1, rw_w2, rw_a0, rw_a1, rw_a2, rw_kk, rw_ka, rw_rk, rw_gnw, rw_gnb,
     s5_lre, s5_lim, s5_logdt, s5_bre, s5_bim, s5_cre, s5_cim, s5_d, s5_wglu, s5_bglu) = wts
    w = W_BRANCH
    row = lambda v: v.reshape(1, -1)
    return dict(
        norm_pre=row(norm_pre), w_in=w_in.astype(BF16), w_out=w_out.astype(BF16), norm_post=row(norm_post),
        w_kvt=w_in[:, OFF_AK:OFF_B].T.astype(BF16),
        lam=(jnp.exp(jnp.sum(lam_q1 * lam_k1)) - jnp.exp(jnp.sum(lam_q2 * lam_k2)) + lam_init).reshape(1),
        attn_scale=1.0 - lam_init, subln=row(jnp.tile(subln_w, A_HEADS)),
        conv_w=conv_w, conv_b=row(conv_b),
        lru_w=jnp.concatenate([_block_diag(lru_wa), _block_diag(lru_wx)], axis=1).astype(BF16),
        lru_b=jnp.concatenate([lru_ba, lru_bx]).reshape(1, 2 * w), lru_lam=row(lru_lam),
        rwkv=(row(rw_mu), row(rw_w0), rw_w1.astype(BF16), rw_w2.astype(BF16), row(rw_a0),
              rw_a1.astype(BF16), rw_a2.astype(BF16), row(rw_kk), row(rw_ka), row(rw_rk)),
        gnw=row(rw_gnw), gnb=row(rw_gnb),
        ones_blk=_block_diag(jnp.ones((RW_HEADS, RW_HS, RW_HS), BF16)),
        s5=_s5_params(s5_lre, s5_lim, s5_logdt, s5_bre, s5_bim, s5_cre, s5_cim)
        + (row(s5_d), s5_wglu.astype(BF16), row(s5_bglu)),
    )


def _layer(x, attend, states, p, tiles):
    conv_buf, lru_h, shift_prev, wkv_s, ssm_x = states
    b, t, _ = x.shape
    n = b * t
    w = W_BRANCH
    tm, bb_proj, tt_proj, nb, tt_b, tt_d, kv_t = tiles

    q, k, v, xb, u, gate, jv, rv, bonus, new_shift = _inproj(
        x, p["norm_pre"], p["w_in"], p["w_kvt"], shift_prev.reshape(b, 1, RW_NPROJ * w),
        p["rwkv"] + (p["ones_blk"],), bb_proj, tt_proj, kv_t)

    oa = attend(p["lam"], q, k, v)
    if kv_t:
        new_k = k.reshape(b, A_HEADS, A_V, t).transpose(0, 3, 1, 2)
        new_v = v.reshape(b, A_HEADS, A_V, t).transpose(0, 3, 1, 2)
    else:
        new_k, new_v = k.reshape(b, t, A_HEADS, A_V), v.reshape(b, t, A_HEADS, A_V)

    ob, conv_t, new_h = _mixb(xb, conv_buf.transpose(1, 0, 2), lru_h, p["conv_w"], p["conv_b"],
                              p["lru_w"], p["lru_b"], p["lru_lam"], nb, tt_b)
    new_conv = conv_t.transpose(1, 0, 2)

    yc, new_s = _rwkv_scan(jv, rv, wkv_s, kv_t)

    od, hre, him = _s5(u, ssm_x[..., 0].reshape(b, S5_W), ssm_x[..., 1].reshape(b, S5_W),
                       *p["s5"], nb, tt_d)
    new_ssm = jnp.stack([hre.reshape(b, S5_GROUPS, S5_STATE), him.reshape(b, S5_GROUPS, S5_STATE)], axis=-1)

    y = _outproj(x.reshape(n, D_MODEL), oa.reshape(n, w), ob.reshape(n, w), yc if kv_t else yc.reshape(n, w),
                 bonus.reshape(n, w), od.reshape(n, w), gate.reshape(n, D_MODEL), p["subln"], p["gnw"], p["gnb"],
                 p["ones_blk"], p["w_out"], p["norm_post"], p["attn_scale"], tm, kv_t)
    new_states = (new_k, new_v, new_conv,
                  new_h.reshape(b, w), new_shift.reshape(b, RW_NPROJ * w), new_s, new_ssm)
    return y.reshape(b, t, D_MODEL), new_states


def kernel(x_prompt, x_sample, cache_k, cache_v, page_table, state_conv, state_lru, state_shift, state_wkv, state_ssm, norm_pre, norm_post, w_in, w_out, lam_q1, lam_k1, lam_q2, lam_k2, subln_w, conv_w, conv_b, lru_wa, lru_ba, lru_wx, lru_bx, lru_lam, rw_mu, rw_w0, rw_w1, rw_w2, rw_a0, rw_a1, rw_a2, rw_kk, rw_ka, rw_rk, rw_gnw, rw_gnb, s5_lre, s5_lim, s5_logdt, s5_bre, s5_bim, s5_cre, s5_cim, s5_d, s5_wglu, s5_bglu):
    weights = (norm_pre, norm_post, w_in, w_out, lam_q1, lam_k1, lam_q2, lam_k2, subln_w,
               conv_w, conv_b, lru_wa, lru_ba, lru_wx, lru_bx, lru_lam,
               rw_mu, rw_w0, rw_w1, rw_w2, rw_a0, rw_a1, rw_a2, rw_kk, rw_ka, rw_rk, rw_gnw, rw_gnb,
               s5_lre, s5_lim, s5_logdt, s5_bre, s5_bim, s5_cre, s5_cim, s5_d, s5_wglu, s5_bglu)
    depth = w_in.shape[0]
    nbp, tp, _ = x_prompt.shape
    nbs, ts, _ = x_sample.shape
    dt = x_prompt.dtype
    w = W_BRANCH
    zero_states = (jnp.zeros((nbp, CONV_W - 1, w), dt), jnp.zeros((nbp, w), dt),
                   jnp.zeros((nbp, RW_NPROJ * w), dt), jnp.zeros((nbp, RW_HEADS, RW_HS, RW_HS), dt),
                   jnp.zeros((nbp, S5_GROUPS, S5_STATE, 2), dt))
    nb = min(SUBLANES, nbp, nbs)
    tiles_p = (min(512, nbp * tp), 1, min(512, tp), nb, min(256, tp), min(128, tp), True)
    tiles_s = (min(512, nbs * ts), min(64, nbs), ts, nb, ts, ts, False)
    tq = min(256, tp)
    n_pool = cache_k.shape[1]
    cache_kt = cache_k.transpose(0, 1, 3, 4, 2).reshape(depth, n_pool, W_BRANCH, PAGE_SIZE)
    cache_vt = cache_v.transpose(0, 1, 3, 4, 2).reshape(depth, n_pool, W_BRANCH, PAGE_SIZE)
    xp, xs = x_prompt, x_sample
    outs_p = [[] for _ in range(7)]
    outs_s = [[] for _ in range(7)]
    for l in range(depth):
        prep = _prep_layer(tuple(wt[l] for wt in weights), 0.8 - 0.6 * math.exp(-0.3 * l))
        xp, st_p = _layer(xp, functools.partial(_attn_prompt, tq=tq), zero_states, prep, tiles_p)
        attend_s = functools.partial(_attn_sample, cache_kt=cache_kt, cache_vt=cache_vt, layer=l,
                                     page_table=page_table, nseq=min(2, nbs))
        st_in = (state_conv[l], state_lru[l], state_shift[l], state_wkv[l], state_ssm[l])
        xs, st_s = _layer(xs, attend_s, st_in, prep, tiles_s)
        for n in range(7):
            outs_p[n].append(st_p[n])
            outs_s[n].append(st_s[n])
    k_p, v_p, conv_p, lru_p, shift_p, wkv_p, ssm_p = [jnp.stack(z) for z in outs_p]
    k_s, v_s, conv_s, lru_s, shift_s, wkv_s, ssm_s = [jnp.stack(z) for z in outs_s]
    return (xp, xs, k_p, k_s, v_p, v_s, conv_p, conv_s, lru_p, lru_s,
            shift_p, shift_s, wkv_p, wkv_s, ssm_p, ssm_s)
```

```python
import functools
import math

import jax
import jax.numpy as jnp
from jax import lax
from jax.experimental import pallas as pl
from jax.experimental.pallas import tpu as pltpu

F32 = jnp.float32
BF16 = jnp.bfloat16

D_MODEL = 1024
W_BRANCH = 256
A_HEADS = 4
A_QK = 32
A_V = 64
PAGE_SIZE = 128
LRU_BLOCKS = 4
LRU_BW = 64
CONV_W = 4
LRU_C = 8.0
RW_HEADS = 4
RW_HS = 64
RW_NPROJ = 5
RW_GN_EPS = RW_HS * 1e-5
S5_CH = 16
S5_GROUPS = 16
S5_STATE = 64
S5_W = S5_GROUPS * S5_STATE
NORM_EPS = 1e-6
NEG_BIG = -1e30

OFF_AQ = 0
OFF_AK = OFF_AQ + W_BRANCH
OFF_AV = OFF_AK + W_BRANCH
OFF_B = OFF_AV + W_BRANCH
OFF_C = OFF_B + W_BRANCH
OFF_D = OFF_C + RW_NPROJ * W_BRANCH
OFF_G = OFF_D + W_BRANCH
D_IN = OFF_G + D_MODEL
_SEGS = ((OFF_AQ, OFF_AK), (OFF_AK, OFF_AV), (OFF_AV, OFF_B), (OFF_B, OFF_C),
         (OFF_C, OFF_D), (OFF_D, OFF_G), (OFF_G, D_IN))

SUBLANES = 8
LANES = 128
VMEM_LIMIT_MIB = 56


def _cparams(*sem):
    return pltpu.CompilerParams(dimension_semantics=sem,
                                vmem_limit_bytes=VMEM_LIMIT_MIB * 1024 * 1024)


def _nt(a, b):
    return lax.dot_general(a, b, (((1,), (1,)), ((), ())), preferred_element_type=F32)


def _mm(a, b):
    return jnp.dot(a, b, preferred_element_type=F32)


def _segsum(x, ones_blk):
    hi = x.astype(BF16)
    lo = (x - hi.astype(F32)).astype(BF16)
    return _mm(hi, ones_blk) + _mm(lo, ones_blk)


def _sigmoid(x):
    return jax.nn.sigmoid(x)


def _softplus(x):
    return jnp.maximum(x, 0.0) + jnp.log1p(jnp.exp(-jnp.abs(x)))


_SEG_K, _SEG_V, _SEG_PC = 1, 2, 4
_N_RWKV_W = 11


def _rwkv_mix(p, prev8, wrefs):
    mu_ref, w0_ref, w1_ref, w2_ref, a0_ref, a1_ref, a2_ref, kkw_ref, ka_ref, rk_ref, ones_ref = wrefs
    bb, tt, _ = p.shape
    w = W_BRANCH
    p_prev = _shift_in(prev8, p, 1)
    xm = (p + (p_prev - p) * mu_ref[...]).reshape(bb * tt, RW_NPROJ * w)
    xr, xw, xk, xv, xa = (xm[:, n * w:(n + 1) * w] for n in range(RW_NPROJ))
    lw = _mm(jnp.tanh(_mm(xw.astype(BF16), w1_ref[...])).astype(BF16), w2_ref[...])
    wlog = -_softplus(-(w0_ref[...] + lw)) - 0.5
    decay = jnp.exp(-jnp.exp(wlog))
    a = _sigmoid(a0_ref[...] + _mm(_mm(xa.astype(BF16), a1_ref[...]).astype(BF16), a2_ref[...]))
    kk = xk * kkw_ref[...]
    ones_blk = ones_ref[...]
    kk = kk / jnp.maximum(jnp.sqrt(_segsum(kk * kk, ones_blk)), 1e-12)
    k = xk * (1.0 + (a - 1.0) * ka_ref[...])
    bonus = _segsum(xr * k * rk_ref[...], ones_blk) * xv
    return (decay, -kk, kk * a, k, xr), xv, bonus


def _inproj_body(x_ref, g_ref, w_ref, wkvt_ref, sh0_ref, *rest, kv_t):
    wrefs = rest[:_N_RWKV_W]
    seg_refs = rest[_N_RWKV_W:_N_RWKV_W + len(_SEGS) - 1]
    jv_ref, rv_ref, bonus_ref, shout_ref, prev_ref = rest[_N_RWKV_W + len(_SEGS) - 1:]
    bb, tt, _ = x_ref.shape
    w = W_BRANCH

    @pl.when(pl.program_id(1) == 0)
    def _():
        prev_ref[...] = jnp.broadcast_to(sh0_ref[...], prev_ref.shape)

    x = x_ref[...].reshape(bb * tt, D_MODEL)
    h = x * lax.rsqrt(jnp.mean(x * x, axis=-1, keepdims=True) + NORM_EPS) * g_ref[...]
    hb = h.astype(BF16)
    segs = [s for n, s in enumerate(_SEGS) if n != _SEG_PC]
    if kv_t:
        kvt = _nt(wkvt_ref[...], hb)
        seg_refs[_SEG_K][0] = kvt[:w]
        seg_refs[_SEG_V][0] = kvt[w:]
    for n, ((lo, hi), o_ref) in enumerate(zip(segs, seg_refs)):
        if not (kv_t and n in (_SEG_K, _SEG_V)):
            o_ref[...] = _mm(hb, w_ref[:, lo:hi]).reshape(bb, tt, hi - lo)

    p = _mm(hb, w_ref[:, OFF_C:OFF_D]).reshape(bb, tt, RW_NPROJ * w)
    jvs, xv, bonus = _rwkv_mix(p, prev_ref[...], wrefs)
    prev_ref[...] = p[:, tt - SUBLANES:tt]
    shout_ref[...] = p[:, tt - 1:tt]
    if kv_t:
        for n, val in enumerate(jvs):
            jv_ref[n] = val.T
        rv_ref[...] = xv.T
    else:
        for n, val in enumerate(jvs):
            jv_ref[:, :, n * w:(n + 1) * w] = val.reshape(bb, tt, w)
        rv_ref[...] = xv.reshape(bb, tt, w)
    bonus_ref[...] = bonus.reshape(bb, tt, w)


def _inproj(x, g, w_bf, wkvt_bf, sh0, rwkv_w, bb, tt, kv_t):
    b, t, _ = x.shape
    assert bb == 1 or not kv_t
    w = W_BRANCH
    wp = RW_NPROJ * w
    seq = lambda i, j: (i, j, 0)
    per_b = lambda i, j: (i, 0, 0)
    const = lambda i, j: (0, 0)
    out_specs, out_shape = [], []
    for n, (lo, hi) in enumerate(s for m, s in enumerate(_SEGS) if m != _SEG_PC):
        if kv_t and n in (_SEG_K, _SEG_V):
            out_specs.append(pl.BlockSpec((1, hi - lo, tt), lambda i, j: (i, 0, j)))
            out_shape.append(jax.ShapeDtypeStruct((b, hi - lo, t), F32))
        else:
            out_specs.append(pl.BlockSpec((bb, tt, hi - lo), seq))
            out_shape.append(jax.ShapeDtypeStruct((b, t, hi - lo), F32))
    if kv_t:
        out_specs += [pl.BlockSpec((RW_NJV, w, tt), lambda i, j: (0, i, j)),
                      pl.BlockSpec((w, tt), lambda i, j: (i, j))]
        out_shape += [jax.ShapeDtypeStruct((RW_NJV, b * w, t), F32), jax.ShapeDtypeStruct((b * w, t), F32)]
    else:
        out_specs += [pl.BlockSpec((bb, tt, wp), seq), pl.BlockSpec((bb, tt, w), seq)]
        out_shape += [jax.ShapeDtypeStruct((b, t, wp), F32), jax.ShapeDtypeStruct((b, t, w), F32)]
    out_specs.append(pl.BlockSpec((bb, tt, w), seq))
    out_shape.append(jax.ShapeDtypeStruct((b, t, w), F32))
    out_specs.append(pl.BlockSpec((bb, 1, wp), per_b))
    out_shape.append(jax.ShapeDtypeStruct((b, 1, wp), F32))
    return pl.pallas_call(
        functools.partial(_inproj_body, kv_t=kv_t),
        grid=(b // bb, t // tt),
        in_specs=[pl.BlockSpec((bb, tt, D_MODEL), seq),
                  pl.BlockSpec((1, D_MODEL), const),
                  pl.BlockSpec((D_MODEL, D_IN), const),
                  pl.BlockSpec((2 * w, D_MODEL), const),
                  pl.BlockSpec((bb, 1, wp), per_b)]
        + [pl.BlockSpec(r.shape, const) for r in rwkv_w],
        out_specs=out_specs,
        out_shape=out_shape,
        scratch_shapes=[pltpu.VMEM((bb, SUBLANES, wp), F32)],
        compiler_params=_cparams("parallel", "arbitrary"),
        name="inproj",
    )(x, g, w_bf, wkvt_bf, sh0, *rwkv_w)


def _attn_prompt_body(lam_ref, q_ref, k_ref, v_ref, o_ref, kb_ref, vb_ref, qm_ref, acc_ref, m_ref, l_ref, *, tq):
    qi = pl.program_id(1)
    nhc = 2 * A_HEADS
    lane = lax.broadcasted_iota(jnp.int32, (1, W_BRANCH), 1)

    @pl.when(qi == 0)
    def _():
        kb_ref[...] = k_ref[0].astype(BF16)
        vb_ref[...] = v_ref[0].astype(BF16)

    q = q_ref[0] * (A_QK ** -0.5 * math.log2(math.e))
    for hc in range(nhc):
        qm_ref[hc * tq:(hc + 1) * tq, :] = jnp.where(lane // A_QK == hc, q, 0.0).astype(BF16)
    m_ref[...] = jnp.full(m_ref.shape, NEG_BIG, F32)
    l_ref[...] = jnp.zeros(l_ref.shape, F32)
    acc_ref[...] = jnp.zeros(acc_ref.shape, F32)

    def block(kb, width, diagonal):
        ks = pl.multiple_of(kb * tq, tq)
        s = _mm(qm_ref[...], kb_ref[:, pl.ds(ks, width)])
        if diagonal:
            r = lax.broadcasted_iota(jnp.int32, (nhc * tq, 1), 0) % tq
            c = lax.broadcasted_iota(jnp.int32, (1, tq), 1)
            s = jnp.where(c <= r, s, NEG_BIG)
        m_old = m_ref[...]
        m_new = jnp.maximum(m_old, jnp.max(s, axis=-1, keepdims=True))
        alpha = jnp.exp2(m_old - m_new)
        p = jnp.exp2(s - jnp.concatenate([m_new] * (width // LANES), axis=1))
        l_ref[...] = alpha * l_ref[...] + jnp.sum(p, axis=-1, keepdims=True)
        m_ref[...] = m_new
        pb = p.astype(BF16)
        a_v = alpha[:, :A_V]
        for h in range(A_HEADS):
            rows = slice(2 * h * tq, (2 * h + 2) * tq)
            pv = _nt(pb[rows], vb_ref[h * A_V:(h + 1) * A_V, pl.ds(ks, width)])
            acc_ref[rows, :] = a_v[rows] * acc_ref[rows, :] + pv

    def off_diagonal_pair(kp, carry):
        block(2 * kp, 2 * tq, False)
        return carry

    lax.fori_loop(0, qi // 2, off_diagonal_pair, 0)

    @pl.when(qi % 2 == 1)
    def _():
        block(qi - 1, tq, False)

    block(qi, tq, True)
    lam = lam_ref[0]
    inv_l = (1.0 / l_ref[...])[:, :A_V]
    heads = []
    for h in range(A_HEADS):
        r0 = slice(2 * h * tq, (2 * h + 1) * tq)
        r1 = slice((2 * h + 1) * tq, (2 * h + 2) * tq)
        heads.append(acc_ref[r0, :] * inv_l[r0] - lam * (acc_ref[r1, :] * inv_l[r1]))
    o_ref[0] = jnp.concatenate(heads, axis=1)


def _attn_prompt(lam, q, kt, vt, tq):
    b, t, _ = q.shape
    nhc = 2 * A_HEADS
    return pl.pallas_call(
        functools.partial(_attn_prompt_body, tq=tq),
        grid=(b, t // tq),
        in_specs=[pl.BlockSpec(memory_space=pltpu.SMEM),
                  pl.BlockSpec((1, tq, W_BRANCH), lambda i, j: (i, j, 0)),
                  pl.BlockSpec((1, W_BRANCH, t), lambda i, j: (i, 0, 0)),
                  pl.BlockSpec((1, W_BRANCH, t), lambda i, j: (i, 0, 0))],
        out_specs=pl.BlockSpec((1, tq, W_BRANCH), lambda i, j: (i, j, 0)),
        out_shape=jax.ShapeDtypeStruct((b, t, W_BRANCH), F32),
        scratch_shapes=[pltpu.VMEM((W_BRANCH, t), BF16),
                        pltpu.VMEM((W_BRANCH, t), BF16),
                        pltpu.VMEM((nhc * tq, W_BRANCH), BF16),
                        pltpu.VMEM((nhc * tq, A_V), F32),
                        pltpu.VMEM((nhc * tq, LANES), F32),
                        pltpu.VMEM((nhc * tq, LANES), F32)],
        compiler_params=_cparams("parallel", "arbitrary"),
        name="attn_prompt",
    )(lam, q, kt, vt)


def _attn_sample_body(pt_ref, lam_ref, q_ref, kn_ref, vn_ref, *rest, n_pages, tq, nseq):
    del pt_ref
    kp_refs = rest[:nseq * n_pages]
    vp_refs = rest[nseq * n_pages:2 * nseq * n_pages]
    o_ref = rest[2 * nseq * n_pages]
    nhc = 2 * A_HEADS
    lane = lax.broadcasted_iota(jnp.int32, (1, W_BRANCH), 1)
    rowhc = lax.broadcasted_iota(jnp.int32, (nhc * tq, 1), 0) // tq
    tcol = lax.broadcasted_iota(jnp.int32, (nhc * tq, tq), 1)
    trow = lax.broadcasted_iota(jnp.int32, (nhc * tq, tq), 0) % tq
    lam = lam_ref[0]

    def diff(p):
        return jnp.concatenate(
            [p[(2 * h) * tq:(2 * h + 1) * tq] - lam * p[(2 * h + 1) * tq:(2 * h + 2) * tq]
             for h in range(A_HEADS)], axis=0)

    for n in range(nseq):
        kp = kp_refs[n * n_pages:(n + 1) * n_pages]
        vp = vp_refs[n * n_pages:(n + 1) * n_pages]
        q = q_ref[n] * (A_QK ** -0.5)
        qbd = jnp.where(lane // A_QK == rowhc, jnp.tile(q, (nhc, 1)), 0.0).astype(BF16)
        s_past = jnp.concatenate([_mm(qbd, kp[j][0, 0].astype(BF16)) for j in range(n_pages)], axis=1)
        s_new = _nt(qbd, kn_ref[n].astype(BF16))
        s_new = jnp.where(tcol <= trow, s_new, NEG_BIG)
        m = jnp.maximum(jnp.max(s_past, axis=-1, keepdims=True), jnp.max(s_new, axis=-1, keepdims=True))
        p_past = jnp.exp(s_past - m)
        p_new = jnp.exp(s_new - m)
        inv = 1.0 / (jnp.sum(p_past, axis=-1, keepdims=True) + jnp.sum(p_new, axis=-1, keepdims=True))
        w_past = diff(p_past * inv).astype(BF16)
        w_new = diff(p_new * inv).astype(BF16)
        out_all = _mm(w_new, vn_ref[n].astype(BF16))
        for j in range(n_pages):
            out_all = out_all + _nt(w_past[:, j * PAGE_SIZE:(j + 1) * PAGE_SIZE], vp[j][0, 0].astype(BF16))
        out = jnp.zeros((tq, W_BRANCH), F32)
        for h in range(A_HEADS):
            out = out + jnp.where(lane // A_V == h, out_all[h * tq:(h + 1) * tq], 0.0)
        o_ref[n] = out


def _attn_sample(lam, q, k, v, cache_kt, cache_vt, layer, page_table, nseq):
    b, tq, _ = q.shape
    n_pages = page_table.shape[1]
    pt = page_table.reshape(-1)
    seq = pl.BlockSpec((nseq, tq, W_BRANCH), lambda i, pt_r, lam_r: (i, 0, 0))

    def page_spec(n, j):
        return pl.BlockSpec((1, 1, W_BRANCH, PAGE_SIZE),
                            lambda i, pt_r, lam_r: (layer, pt_r[(i * nseq + n) * n_pages + j], 0, 0))

    pages = [page_spec(n, j) for n in range(nseq) for j in range(n_pages)]
    grid_spec = pltpu.PrefetchScalarGridSpec(
        num_scalar_prefetch=2,
        grid=(b // nseq,),
        in_specs=[seq, seq, seq] + pages * 2,
        out_specs=seq,
    )
    return pl.pallas_call(
        functools.partial(_attn_sample_body, n_pages=n_pages, tq=tq, nseq=nseq),
        grid_spec=grid_spec,
        out_shape=jax.ShapeDtypeStruct((b, tq, W_BRANCH), F32),
        compiler_params=_cparams("parallel"),
        name="attn_sample",
    )(pt, lam, q, k, v, *([cache_kt] * (nseq * n_pages)), *([cache_vt] * (nseq * n_pages)))


def _shift_in(prev8, x, s):
    tt = x.shape[1]
    ext = jnp.concatenate([prev8, x], axis=1)
    return pltpu.roll(ext, s, axis=1)[:, SUBLANES:SUBLANES + tt]


def _to_time_major(src_ref, dst_ref, row0, nb, tt):
    for b in range(nb):
        for c in range(dst_ref.shape[0]):
            dst_ref[c, pl.ds(row0 + b, tt, stride=nb), :] = src_ref[b, :, c * LANES:(c + 1) * LANES]


def _from_time_major(src_ref, dst_ref, nb, tt):
    for b in range(nb):
        for c in range(src_ref.shape[0]):
            dst_ref[b, :, c * LANES:(c + 1) * LANES] = src_ref[c, pl.ds(b, tt, stride=nb), :]


def _planes_get(ref, r0, r1):
    return jnp.concatenate([ref[c, r0:r1, :] for c in range(ref.shape[0])], axis=1)


def _planes_set(ref, r0, r1, val):
    for c in range(ref.shape[0]):
        ref[c, r0:r1, :] = val[:, c * LANES:(c + 1) * LANES]


def _mixb_body(xb_ref, cst_ref, h0_ref, cw_ref, cb_ref, wg_ref, bg_ref, lam_ref,
               ob_ref, cout_ref, hout_ref, xt_ref, ht_ref, hcar_ref, *, nb, tt):
    w = W_BRANCH
    rows = tt * nb
    hist = (CONV_W - 1) * nb

    @pl.when(pl.program_id(1) == 0)
    def _():
        _planes_set(xt_ref, 0, hist, cst_ref[...].reshape(hist, w))
        hcar_ref[...] = h0_ref[...]

    _to_time_major(xb_ref, xt_ref, hist, nb, tt)
    cw = cw_ref[...]
    xc = cb_ref[...] + _planes_get(xt_ref, 0, rows) * cw[0:1]
    for s in range(1, CONV_W):
        xc = xc + _planes_get(xt_ref, s * nb, s * nb + rows) * cw[s:s + 1]
    tail = _planes_get(xt_ref, rows, rows + hist)
    cout_ref[...] = tail.reshape(CONV_W - 1, nb, w)
    _planes_set(xt_ref, 0, hist, tail)

    gates = _mm(xc.astype(BF16), wg_ref[...]) + bg_ref[...]
    r = _sigmoid(gates[:, :w])
    i = _sigmoid(gates[:, w:])
    log_a = -LRU_C * r * _softplus(-lam_ref[...])
    a = jnp.exp(log_a)
    b = jnp.sqrt(1.0 - jnp.exp(2.0 * log_a)) * (i * xc)
    h = hcar_ref[...]
    for t in range(tt):
        sl = slice(t * nb, (t + 1) * nb)
        h = a[sl] * h + b[sl]
        _planes_set(ht_ref, t * nb, (t + 1) * nb, h)
    hcar_ref[...] = h
    hout_ref[...] = h
    _from_time_major(ht_ref, ob_ref, nb, tt)


def _mixb(xb, cst, h0, cw, cb, wg, bg, lru_lam, nb, tt):
    b, t, w = xb.shape
    const = lambda i, j: (0, 0)
    return pl.pallas_call(
        functools.partial(_mixb_body, nb=nb, tt=tt),
        grid=(b // nb, t // tt),
        in_specs=[pl.BlockSpec((nb, tt, w), lambda i, j: (i, j, 0)),
                  pl.BlockSpec((CONV_W - 1, nb, w), lambda i, j: (0, i, 0)),
                  pl.BlockSpec((nb, w), lambda i, j: (i, 0)),
                  pl.BlockSpec((CONV_W, w), const),
                  pl.BlockSpec((1, w), const),
                  pl.BlockSpec((w, 2 * w), const),
                  pl.BlockSpec((1, 2 * w), const),
                  pl.BlockSpec((1, w), const)],
        out_specs=[pl.BlockSpec((nb, tt, w), lambda i, j: (i, j, 0)),
                   pl.BlockSpec((CONV_W - 1, nb, w), lambda i, j: (0, i, 0)),
                   pl.BlockSpec((nb, w), lambda i, j: (i, 0))],
        out_shape=[jax.ShapeDtypeStruct((b, t, w), F32),
                   jax.ShapeDtypeStruct((CONV_W - 1, b, w), F32),
                   jax.ShapeDtypeStruct((b, w), F32)],
        scratch_shapes=[pltpu.VMEM((w // LANES, (tt + CONV_W - 1) * nb, LANES), F32),
                        pltpu.VMEM((w // LANES, tt * nb, LANES), F32),
                        pltpu.VMEM((nb, w), F32)],
        compiler_params=_cparams("parallel", "arbitrary"),
        name="conv_rglru",
    )(xb, cst, h0, cw, cb, wg, bg, lru_lam)


_JV_DECAY, _JV_NKK, _JV_KKA, _JV_K, _JV_R = range(5)
RW_NJV = 5
XT = LANES
RELAY_UNROLL = 32
Z_PAD_ROWS = SUBLANES


def _tree_sum(parts):
    while len(parts) > 1:
        parts = [parts[n] + parts[n + 1] for n in range(0, len(parts), 2)]
    return parts[0]


def _rwkv_steps(zrow, zv_ref, zy_ref, s_ref, nsteps, rows, n_acc=4):
    def accumulate(parts, j, term):
        parts[j % n_acc] = term if parts[j % n_acc] is None else parts[j % n_acc] + term

    sa_parts = [None] * n_acc
    for j in range(RW_HS):
        accumulate(sa_parts, j, s_ref[j] * zrow(_JV_NKK, j, 0))

    def step(t, sa):
        r0 = pl.multiple_of(t * rows, rows)
        v = zv_ref[pl.ds(r0, rows), :]
        t_next = jnp.minimum(t + 1, nsteps - 1)
        y_parts = [None] * n_acc
        sa_parts = [None] * n_acc
        for j in range(RW_HS):
            s = s_ref[j] * zrow(_JV_DECAY, j, t) + sa * zrow(_JV_KKA, j, t) + v * zrow(_JV_K, j, t)
            s_ref[j] = s
            accumulate(y_parts, j, s * zrow(_JV_R, j, t))
            accumulate(sa_parts, j, s * zrow(_JV_NKK, j, t_next))
        zy_ref[pl.ds(r0, rows), :] = _tree_sum(y_parts)
        return _tree_sum(sa_parts)

    lax.fori_loop(0, nsteps, step, _tree_sum(sa_parts))


def _rwkv_scan_fm_body(jv_ref, v_ref, s0_ref, y_ref, s_ref, z_ref, zv_ref, zy_ref, *, nb):
    reps = LANES // (nb * RW_HEADS)
    rows = RW_HS // reps
    nbh = nb * RW_HEADS

    @pl.when(pl.program_id(0) == 0)
    def _():
        s_ref[...] = s0_ref[...]

    for q in range(RW_NJV):
        def relay(jg, carry, q=q):
            for jj in range(RELAY_UNROLL):
                j = jg * RELAY_UNROLL + jj
                r = jv_ref[q, pl.ds(j, nbh, stride=RW_HS), :]
                z_ref[q, j, 0:XT, :] = jnp.concatenate([r] * reps, axis=0).T
            return carry

        lax.fori_loop(0, RW_HS // RELAY_UNROLL, relay, 0)
    for ih in range(rows):
        tile = jnp.concatenate([v_ref[pl.ds(rep * rows + ih, nbh, stride=RW_HS), :] for rep in range(reps)], axis=0)
        zv_ref[pl.ds(ih, XT, stride=rows), :] = tile.T

    _rwkv_steps(lambda which, j, t: z_ref[which, j, pl.ds(t, 1), :], zv_ref, zy_ref, s_ref, XT, rows)

    for ih in range(rows):
        tile = zy_ref[pl.ds(ih, XT, stride=rows), :].T
        for rep in range(reps):
            y_ref[pl.ds(rep * rows + ih, nbh, stride=RW_HS), :] = tile[rep * nbh:(rep + 1) * nbh]


def _rwkv_scan_fm(jv, v, s0):
    nq, nrow, t = jv.shape
    rows = s0.shape[1]
    st = lambda i: (0, 0, 0)
    return pl.pallas_call(
        functools.partial(_rwkv_scan_fm_body, nb=nrow // W_BRANCH),
        grid=(t // XT,),
        in_specs=[pl.BlockSpec((nq, nrow, XT), lambda i: (0, 0, i)),
                  pl.BlockSpec((nrow, XT), lambda i: (0, i)),
                  pl.BlockSpec((RW_HS, rows, LANES), st)],
        out_specs=[pl.BlockSpec((nrow, XT), lambda i: (0, i)),
                   pl.BlockSpec((RW_HS, rows, LANES), st)],
        out_shape=[jax.ShapeDtypeStruct((nrow, t), F32),
                   jax.ShapeDtypeStruct((RW_HS, rows, LANES), F32)],
        scratch_shapes=[pltpu.VMEM((RW_NJV, RW_HS, XT + Z_PAD_ROWS, LANES), F32),
                        pltpu.VMEM((XT * rows, LANES), F32),
                        pltpu.VMEM((XT * rows, LANES), F32)],
        compiler_params=_cparams("arbitrary"),
        name="rwkv_scan",
    )(jv, v, s0)


SEQLANES_ROWS = 4


def _rwkv_scan_seqlanes_body(z_ref, zv_ref, s0_ref, zy_ref, s_ref, *, nsteps):
    def value_rows(ig, carry):
        for ii in range(SEQLANES_ROWS):
            i = ig * SEQLANES_ROWS + ii
            s = s0_ref[0, 0, i]
            for t in range(nsteps):
                sa = jnp.sum(s * z_ref[t, _JV_NKK, 0], axis=0, keepdims=True)
                v = zv_ref[t, 0, pl.ds(i, 1), :]
                s = s * z_ref[t, _JV_DECAY, 0] + sa * z_ref[t, _JV_KKA, 0] + v * z_ref[t, _JV_K, 0]
                zy_ref[t, 0, pl.ds(i, 1), :] = jnp.sum(s * z_ref[t, _JV_R, 0], axis=0, keepdims=True)
            s_ref[0, i] = s
        return carry

    lax.fori_loop(0, RW_HS // SEQLANES_ROWS, value_rows, 0)


def _rwkv_scan_seqlanes(z, zv, s_all, layer):
    t, nq, nh, hs, _ = z.shape
    return pl.pallas_call(
        functools.partial(_rwkv_scan_seqlanes_body, nsteps=t),
        grid=(nh,),
        in_specs=[pl.BlockSpec((t, nq, 1, hs, LANES), lambda i: (0, 0, i, 0, 0)),
                  pl.BlockSpec((t, 1, hs, LANES), lambda i: (0, i, 0, 0)),
                  pl.BlockSpec((1, 1, hs, hs, LANES), lambda i: (layer, i, 0, 0, 0))],
        out_specs=[pl.BlockSpec((t, 1, hs, LANES), lambda i: (0, i, 0, 0)),
                   pl.BlockSpec((1, hs, hs, LANES), lambda i: (i, 0, 0, 0))],
        out_shape=[jax.ShapeDtypeStruct((t, nh, hs, LANES), F32),
                   jax.ShapeDtypeStruct((nh, hs, hs, LANES), F32)],
        compiler_params=_cparams("parallel"),
        name="rwkv_scan_seqlanes",
    )(z, zv, s_all)


def _lane_split(b):
    seqs = min(b, LANES // RW_HEADS)
    return b // seqs, seqs, LANES // (RW_HEADS * seqs)


def _state_to_lanes(s, g, bl, reps):
    rows = RW_HS // reps
    s = s.reshape(g, bl, RW_HEADS, reps, rows, RW_HS).transpose(0, 5, 4, 3, 1, 2)
    return s.reshape(g, RW_HS, rows, LANES)


def _state_from_lanes(s, bl, reps):
    g, _, rows, _ = s.shape
    s = s.reshape(g, RW_HS, rows, reps, bl, RW_HEADS).transpose(0, 4, 5, 3, 2, 1)
    return s.reshape(g * bl, RW_HEADS, RW_HS, RW_HS)


def _rwkv_scan(jv, v, wkv_s, feature_major):
    if feature_major:
        b = wkv_s.shape[0]
        g, bl, reps = _lane_split(b)
        assert g == 1 and jv.shape[-1] % XT == 0
        y, s = _rwkv_scan_fm(jv, v, _state_to_lanes(wkv_s, g, bl, reps)[0])
        return y, _state_from_lanes(s[None], bl, reps)
    s_all, layer = wkv_s
    b, t, _ = jv.shape
    assert b == LANES
    z = jv.reshape(b, t, RW_NJV, RW_HEADS, RW_HS).transpose(1, 2, 3, 4, 0)
    zv = v.reshape(b, t, RW_HEADS, RW_HS).transpose(1, 2, 3, 0)
    zy, s = _rwkv_scan_seqlanes(z, zv, s_all, layer)
    return zy.transpose(3, 0, 1, 2).reshape(b, t, W_BRANCH), s


S5_CHUNK_STEPS = 32


def _s5_body(u_ref, xre_ref, xim_ref, bmat_ref, cre_ref, cim_ref, ar_ref, ai_ref,
             d_ref, wglu_ref, bglu_ref, od_ref, hre_out, him_out,
             ut_ref, yt_ref, hre_s, him_s, car_re, car_im, *, nb, tt):
    @pl.when(pl.program_id(1) == 0)
    def _():
        car_re[...] = xre_ref[...]
        car_im[...] = xim_ref[...]

    _to_time_major(u_ref, ut_ref, 0, nb, tt)
    ar = jnp.broadcast_to(ar_ref[...], (nb, S5_W))
    ai = jnp.broadcast_to(ai_ref[...], (nb, S5_W))
    hr = car_re[...]
    hi = car_im[...]
    chunk = min(tt, S5_CHUNK_STEPS)
    for c0 in range(0, tt, chunk):
        bu = _mm(_planes_get(ut_ref, c0 * nb, (c0 + chunk) * nb).astype(BF16), bmat_ref[...])
        for s in range(chunk):
            src = slice(s * nb, (s + 1) * nb)
            hr, hi = (ar * hr - ai * hi + bu[src, :S5_W]), (ar * hi + ai * hr + bu[src, S5_W:])
            dst = slice((c0 + s) * nb, (c0 + s + 1) * nb)
            hre_s[dst, :] = hr
            him_s[dst, :] = hi
    car_re[...] = hr
    car_im[...] = hi
    hre_out[...] = hr
    him_out[...] = hi
    y = _mm(hre_s[...].astype(BF16), cre_ref[...]) - _mm(him_s[...].astype(BF16), cim_ref[...])
    y = y + d_ref[...] * _planes_get(ut_ref, 0, tt * nb)
    z = 0.5 * y * (1.0 + jnp.tanh(math.sqrt(2.0 / math.pi) * (y + 0.044715 * (y * y * y))))
    _planes_set(yt_ref, 0, tt * nb, z * _sigmoid(_mm(z.astype(BF16), wglu_ref[...]) + bglu_ref[...]))
    _from_time_major(yt_ref, od_ref, nb, tt)


def _s5(u, xre, xim, bmat, cre, cim, ar, ai, d, wglu, bglu, nb, tt):
    b, t, w = u.shape
    seq = lambda i, j: (i, j, 0)
    per_b = lambda i, j: (i, 0)
    c2 = lambda i, j: (0, 0)
    rows = tt * nb
    return pl.pallas_call(
        functools.partial(_s5_body, nb=nb, tt=tt),
        grid=(b // nb, t // tt),
        in_specs=[pl.BlockSpec((nb, tt, w), seq),
                  pl.BlockSpec((nb, S5_W), per_b),
                  pl.BlockSpec((nb, S5_W), per_b),
                  pl.BlockSpec(bmat.shape, c2),
                  pl.BlockSpec(cre.shape, c2),
                  pl.BlockSpec(cim.shape, c2),
                  pl.BlockSpec((1, S5_W), c2),
                  pl.BlockSpec((1, S5_W), c2),
                  pl.BlockSpec((1, w), c2),
                  pl.BlockSpec((w, w), c2),
                  pl.BlockSpec((1, w), c2)],
        out_specs=[pl.BlockSpec((nb, tt, w), seq),
                   pl.BlockSpec((nb, S5_W), per_b),
                   pl.BlockSpec((nb, S5_W), per_b)],
        out_shape=[jax.ShapeDtypeStruct((b, t, w), F32),
                   jax.ShapeDtypeStruct((b, S5_W), F32),
                   jax.ShapeDtypeStruct((b, S5_W), F32)],
        scratch_shapes=[pltpu.VMEM((w // LANES, rows, LANES), F32), pltpu.VMEM((w // LANES, rows, LANES), F32),
                        pltpu.VMEM((rows, S5_W), F32), pltpu.VMEM((rows, S5_W), F32),
                        pltpu.VMEM((nb, S5_W), F32), pltpu.VMEM((nb, S5_W), F32)],
        compiler_params=_cparams("parallel", "arbitrary"),
        name="s5",
    )(u, xre, xim, bmat, cre, cim, ar, ai, d, wglu, bglu)


def _s5_params(lre, lim, logdt, b_re, b_im, c_re, c_im):
    dt = jnp.exp(logdt)[:, None]
    mag = jnp.exp(lre * dt)
    ab_re, ab_im = mag * jnp.cos(lim * dt), mag * jnp.sin(lim * dt)
    den = lre * lre + lim * lim
    pr = ab_re - 1.0
    f_re = (pr * lre + ab_im * lim) / den
    f_im = (ab_im * lre - pr * lim) / den
    bb_re = f_re[..., None] * b_re - f_im[..., None] * b_im
    bb_im = f_re[..., None] * b_im + f_im[..., None] * b_re
    eye = jnp.eye(S5_GROUPS, dtype=F32)
    in_re = jnp.einsum('gnc,gh->gchn', bb_re, eye).reshape(W_BRANCH, S5_W)
    in_im = jnp.einsum('gnc,gh->gchn', bb_im, eye).reshape(W_BRANCH, S5_W)
    bmat = jnp.concatenate([in_re, in_im], axis=1).astype(BF16)
    cre = jnp.einsum('gcn,gh->gnhc', c_re, eye).reshape(S5_W, W_BRANCH).astype(BF16)
    cim = jnp.einsum('gcn,gh->gnhc', c_im, eye).reshape(S5_W, W_BRANCH).astype(BF16)
    return bmat, cre, cim, ab_re.reshape(1, S5_W), ab_im.reshape(1, S5_W)


def _outproj_body(x_ref, oa_ref, ob_ref, yc_ref, bonus_ref, od_ref, gate_ref,
                  subln_ref, gnw_ref, gnb_ref, ones_ref, wout_ref, npost_ref, y_ref, *, attn_scale, yc_t):
    w = W_BRANCH
    ones_blk = ones_ref[...]
    oa = oa_ref[...]
    oa = oa * lax.rsqrt(_segsum(oa * oa, ones_blk) * (1.0 / A_V) + NORM_EPS) * subln_ref[...] * attn_scale
    yc = yc_ref[...].T if yc_t else yc_ref[...]
    mean = _segsum(yc, ones_blk) * (1.0 / RW_HS)
    cen = yc - mean
    var = _segsum(cen * cen, ones_blk) * (1.0 / RW_HS)
    oc = cen * lax.rsqrt(var + RW_GN_EPS) * gnw_ref[...] + gnb_ref[...] + bonus_ref[...]
    gate = gate_ref[...]
    z = None
    for n, branch in enumerate((oa, ob_ref[...], oc, od_ref[...])):
        g = gate[:, n * w:(n + 1) * w]
        part = _mm((branch * (g * _sigmoid(g))).astype(BF16), wout_ref[n * w:(n + 1) * w, :])
        z = part if z is None else z + part
    zn = z * lax.rsqrt(jnp.mean(z * z, axis=-1, keepdims=True) + NORM_EPS) * npost_ref[...]
    y_ref[...] = x_ref[...] + zn


def _outproj(x2, oa, ob, yc, bonus, od, gate, subln, gnw, gnb, ones_blk, wout_bf, npost, attn_scale, tm, yc_t):
    n = x2.shape[0]
    w = W_BRANCH
    row = lambda i: (i, 0)
    const = lambda i: (0, 0)
    rw = pl.BlockSpec((tm, w), row)
    rd = pl.BlockSpec((tm, D_MODEL), row)
    vec = pl.BlockSpec((1, w), const)
    if yc_t:
        tiles_per_seq = yc.shape[1] // tm
        yc_spec = pl.BlockSpec((w, tm), lambda i: (i // tiles_per_seq, i % tiles_per_seq))
    else:
        yc_spec = rw
    return pl.pallas_call(
        functools.partial(_outproj_body, attn_scale=attn_scale, yc_t=yc_t),
        grid=(n // tm,),
        in_specs=[rd, rw, rw, yc_spec, rw, rw, rd, vec, vec, vec,
                  pl.BlockSpec((w, w), const),
                  pl.BlockSpec((D_MODEL, D_MODEL), const),
                  pl.BlockSpec((1, D_MODEL), const)],
        out_specs=rd,
        out_shape=jax.ShapeDtypeStruct((n, D_MODEL), F32),
        compiler_params=_cparams("parallel"),
        name="outproj",
    )(x2, oa, ob, yc, bonus, od, gate, subln, gnw, gnb, ones_blk, wout_bf, npost)


def _block_diag(blocks):
    n, k, _ = blocks.shape
    eye = jnp.eye(n, dtype=blocks.dtype)
    return jnp.einsum('nij,nm->nimj', blocks, eye).reshape(n * k, n * k)


def _prep_layer(wts, lam_init):
    (norm_pre, norm_post, w_in, w_out, lam_q1, lam_k1, lam_q2, lam_k2, subln_w, conv_w, conv_b,
     lru_wa, lru_ba, lru_wx, lru_bx, lru_lam,
     rw_mu, rw_w0, rw_w1, rw_w2, rw_a0, rw_a1, rw_a2, rw_kk, rw_ka, rw_rk, rw_gnw, rw_gnb,
     s5_lre, s5_lim, s5_logdt, s5_bre, s5_bim, s5_cre, s5_cim, s5_d, s5_wglu, s5_bglu) = wts
    w = W_BRANCH
    row = lambda v: v.reshape(1, -1)
    return dict(
        norm_pre=row(norm_pre), w_in=w_in.astype(BF16), w_out=w_out.astype(BF16), norm_post=row(norm_post),
        w_kvt=w_in[:, OFF_AK:OFF_B].T.astype(BF16),
        lam=(jnp.exp(jnp.sum(lam_q1 * lam_k1)) - jnp.exp(jnp.sum(lam_q2 * lam_k2)) + lam_init).reshape(1),
        attn_scale=1.0 - lam_init, subln=row(jnp.tile(subln_w, A_HEADS)),
        conv_w=conv_w, conv_b=row(conv_b),
        lru_w=jnp.concatenate([_block_diag(lru_wa), _block_diag(lru_wx)], axis=1).astype(BF16),
        lru_b=jnp.concatenate([lru_ba, lru_bx]).reshape(1, 2 * w), lru_lam=row(lru_lam),
        rwkv=(row(rw_mu), row(rw_w0), rw_w1.astype(BF16), rw_w2.astype(BF16), row(rw_a0),
              rw_a1.astype(BF16), rw_a2.astype(BF16), row(rw_kk), row(rw_ka), row(rw_rk)),
        gnw=row(rw_gnw), gnb=row(rw_gnb),
        ones_blk=_block_diag(jnp.ones((RW_HEADS, RW_HS, RW_HS), BF16)),
        s5=_s5_params(s5_lre, s5_lim, s5_logdt, s5_bre, s5_bim, s5_cre, s5_cim)
        + (row(s5_d), s5_wglu.astype(BF16), row(s5_bglu)),
    )


def _layer(x, attend, states, p, tiles):
    conv_buf, lru_h, shift_prev, wkv_s, ssm_x = states
    b, t, _ = x.shape
    n = b * t
    w = W_BRANCH
    tm, bb_proj, tt_proj, nb, tt_b, tt_d, kv_t = tiles

    q, k, v, xb, u, gate, jv, rv, bonus, new_shift = _inproj(
        x, p["norm_pre"], p["w_in"], p["w_kvt"], shift_prev.reshape(b, 1, RW_NPROJ * w),
        p["rwkv"] + (p["ones_blk"],), bb_proj, tt_proj, kv_t)

    oa = attend(p["lam"], q, k, v)
    if kv_t:
        new_k = k.reshape(b, A_HEADS, A_V, t).transpose(0, 3, 1, 2)
        new_v = v.reshape(b, A_HEADS, A_V, t).transpose(0, 3, 1, 2)
    else:
        new_k, new_v = k.reshape(b, t, A_HEADS, A_V), v.reshape(b, t, A_HEADS, A_V)

    ob, conv_t, new_h = _mixb(xb, conv_buf.transpose(1, 0, 2), lru_h, p["conv_w"], p["conv_b"],
                              p["lru_w"], p["lru_b"], p["lru_lam"], nb, tt_b)
    new_conv = conv_t.transpose(1, 0, 2)

    yc, new_s = _rwkv_scan(jv, rv, wkv_s, kv_t)

    od, hre, him = _s5(u, ssm_x[..., 0].reshape(b, S5_W), ssm_x[..., 1].reshape(b, S5_W),
                       *p["s5"], nb, tt_d)
    new_ssm = jnp.stack([hre.reshape(b, S5_GROUPS, S5_STATE), him.reshape(b, S5_GROUPS, S5_STATE)], axis=-1)

    y = _outproj(x.reshape(n, D_MODEL), oa.reshape(n, w), ob.reshape(n, w), yc if kv_t else yc.reshape(n, w),
                 bonus.reshape(n, w), od.reshape(n, w), gate.reshape(n, D_MODEL), p["subln"], p["gnw"], p["gnb"],
                 p["ones_blk"], p["w_out"], p["norm_post"], p["attn_scale"], tm, kv_t)
    new_states = (new_k, new_v, new_conv,
                  new_h.reshape(b, w), new_shift.reshape(b, RW_NPROJ * w), new_s, new_ssm)
    return y.reshape(b, t, D_MODEL), new_states


def kernel(x_prompt, x_sample, cache_k, cache_v, page_table, state_conv, state_lru, state_shift, state_wkv, state_ssm, norm_pre, norm_post, w_in, w_out, lam_q1, lam_k1, lam_q2, lam_k2, subln_w, conv_w, conv_b, lru_wa, lru_ba, lru_wx, lru_bx, lru_lam, rw_mu, rw_w0, rw_w1, rw_w2, rw_a0, rw_a1, rw_a2, rw_kk, rw_ka, rw_rk, rw_gnw, rw_gnb, s5_lre, s5_lim, s5_logdt, s5_bre, s5_bim, s5_cre, s5_cim, s5_d, s5_wglu, s5_bglu):
    weights = (norm_pre, norm_post, w_in, w_out, lam_q1, lam_k1, lam_q2, lam_k2, subln_w,
               conv_w, conv_b, lru_wa, lru_ba, lru_wx, lru_bx, lru_lam,
               rw_mu, rw_w0, rw_w1, rw_w2, rw_a0, rw_a1, rw_a2, rw_kk, rw_ka, rw_rk, rw_gnw, rw_gnb,
               s5_lre, s5_lim, s5_logdt, s5_bre, s5_bim, s5_cre, s5_cim, s5_d, s5_wglu, s5_bglu)
    depth = w_in.shape[0]
    nbp, tp, _ = x_prompt.shape
    nbs, ts, _ = x_sample.shape
    dt = x_prompt.dtype
    w = W_BRANCH
    zero_states = (jnp.zeros((nbp, CONV_W - 1, w), dt), jnp.zeros((nbp, w), dt),
                   jnp.zeros((nbp, RW_NPROJ * w), dt), jnp.zeros((nbp, RW_HEADS, RW_HS, RW_HS), dt),
                   jnp.zeros((nbp, S5_GROUPS, S5_STATE, 2), dt))
    nb = min(SUBLANES, nbp, nbs)
    tiles_p = (min(512, nbp * tp), 1, min(512, tp), nb, min(256, tp), min(128, tp), True)
    tiles_s = (min(512, nbs * ts), min(64, nbs), ts, nb, ts, ts, False)
    tq = min(256, tp)
    n_pool = cache_k.shape[1]
    cache_kt = cache_k.transpose(0, 1, 3, 4, 2).reshape(depth, n_pool, W_BRANCH, PAGE_SIZE)
    cache_vt = cache_v.transpose(0, 1, 3, 4, 2).reshape(depth, n_pool, W_BRANCH, PAGE_SIZE)
    wkv_seqlanes = state_wkv.transpose(0, 2, 3, 4, 1)
    xp, xs = x_prompt, x_sample
    outs_p = [[] for _ in range(7)]
    outs_s = [[] for _ in range(7)]
    for l in range(depth):
        prep = _prep_layer(tuple(wt[l] for wt in weights), 0.8 - 0.6 * math.exp(-0.3 * l))
        xp, st_p = _layer(xp, functools.partial(_attn_prompt, tq=tq), zero_states, prep, tiles_p)
        attend_s = functools.partial(_attn_sample, cache_kt=cache_kt, cache_vt=cache_vt, layer=l,
                                     page_table=page_table, nseq=min(2, nbs))
        st_in = (state_conv[l], state_lru[l], state_shift[l], (wkv_seqlanes, l), state_ssm[l])
        xs, st_s = _layer(xs, attend_s, st_in, prep, tiles_s)
        for n in range(7):
            outs_p[n].append(st_p[n])
            outs_s[n].append(st_s[n])
    k_p, v_p, conv_p, lru_p, shift_p, wkv_p, ssm_p = [jnp.stack(z) for z in outs_p]
    k_s, v_s, conv_s, lru_s, shift_s, wkv_s, ssm_s = [jnp.stack(z) for z in outs_s]
    wkv_s = wkv_s.transpose(0, 4, 1, 2, 3)
    return (xp, xs, k_p, k_s, v_p, v_s, conv_p, conv_s, lru_p, lru_s,
            shift_p, shift_s, wkv_p, wkv_s, ssm_p, ssm_s)
```

```python
import functools
import math

import jax
import jax.numpy as jnp
from jax import lax
from jax.experimental import pallas as pl
from jax.experimental.pallas import tpu as pltpu

F32 = jnp.float32
BF16 = jnp.bfloat16

D_MODEL = 1024
W_BRANCH = 256
A_HEADS = 4
A_QK = 32
A_V = 64
PAGE_SIZE = 128
LRU_BLOCKS = 4
LRU_BW = 64
CONV_W = 4
LRU_C = 8.0
RW_HEADS = 4
RW_HS = 64
RW_NPROJ = 5
RW_GN_EPS = RW_HS * 1e-5
S5_CH = 16
S5_GROUPS = 16
S5_STATE = 64
S5_W = S5_GROUPS * S5_STATE
NORM_EPS = 1e-6
NEG_BIG = -1e30

OFF_AQ = 0
OFF_AK = OFF_AQ + W_BRANCH
OFF_AV = OFF_AK + W_BRANCH
OFF_B = OFF_AV + W_BRANCH
OFF_C = OFF_B + W_BRANCH
OFF_D = OFF_C + RW_NPROJ * W_BRANCH
OFF_G = OFF_D + W_BRANCH
D_IN = OFF_G + D_MODEL
_SEGS = ((OFF_AQ, OFF_AK), (OFF_AK, OFF_AV), (OFF_AV, OFF_B), (OFF_B, OFF_C),
         (OFF_C, OFF_D), (OFF_D, OFF_G), (OFF_G, D_IN))

SUBLANES = 8
LANES = 128
VMEM_LIMIT_MIB = 56


def _cparams(*sem):
    return pltpu.CompilerParams(dimension_semantics=sem,
                                vmem_limit_bytes=VMEM_LIMIT_MIB * 1024 * 1024)


def _nt(a, b):
    return lax.dot_general(a, b, (((1,), (1,)), ((), ())), preferred_element_type=F32)


def _mm(a, b):
    return jnp.dot(a, b, preferred_element_type=F32)


def _segsum(x, ones_blk):
    hi = x.astype(BF16)
    lo = (x - hi.astype(F32)).astype(BF16)
    return _mm(hi, ones_blk) + _mm(lo, ones_blk)


def _sigmoid(x):
    return jax.nn.sigmoid(x)


def _softplus(x):
    return jnp.maximum(x, 0.0) + jnp.log1p(jnp.exp(-jnp.abs(x)))


_SEG_K, _SEG_V, _SEG_PC = 1, 2, 4
_N_RWKV_W = 11


def _rwkv_mix(p, prev8, wrefs):
    mu_ref, w0_ref, w1_ref, w2_ref, a0_ref, a1_ref, a2_ref, kkw_ref, ka_ref, rk_ref, ones_ref = wrefs
    bb, tt, _ = p.shape
    w = W_BRANCH
    p_prev = _shift_in(prev8, p, 1)
    xm = (p + (p_prev - p) * mu_ref[...]).reshape(bb * tt, RW_NPROJ * w)
    xr, xw, xk, xv, xa = (xm[:, n * w:(n + 1) * w] for n in range(RW_NPROJ))
    lw = _mm(jnp.tanh(_mm(xw.astype(BF16), w1_ref[...])).astype(BF16), w2_ref[...])
    wlog = -_softplus(-(w0_ref[...] + lw)) - 0.5
    decay = jnp.exp(-jnp.exp(wlog))
    a = _sigmoid(a0_ref[...] + _mm(_mm(xa.astype(BF16), a1_ref[...]).astype(BF16), a2_ref[...]))
    kk = xk * kkw_ref[...]
    ones_blk = ones_ref[...]
    kk = kk / jnp.maximum(jnp.sqrt(_segsum(kk * kk, ones_blk)), 1e-12)
    k = xk * (1.0 + (a - 1.0) * ka_ref[...])
    bonus = _segsum(xr * k * rk_ref[...], ones_blk) * xv
    return (decay, -kk, kk * a, k, xr), xv, bonus


def _inproj_body(x_ref, g_ref, w_ref, wkvt_ref, sh0_ref, *rest, kv_t):
    wrefs = rest[:_N_RWKV_W]
    seg_refs = rest[_N_RWKV_W:_N_RWKV_W + len(_SEGS) - 1]
    jv_ref, rv_ref, bonus_ref, shout_ref, prev_ref = rest[_N_RWKV_W + len(_SEGS) - 1:]
    bb, tt, _ = x_ref.shape
    w = W_BRANCH

    @pl.when(pl.program_id(1) == 0)
    def _():
        prev_ref[...] = jnp.broadcast_to(sh0_ref[...], prev_ref.shape)

    x = x_ref[...].reshape(bb * tt, D_MODEL)
    h = x * lax.rsqrt(jnp.mean(x * x, axis=-1, keepdims=True) + NORM_EPS) * g_ref[...]
    hb = h.astype(BF16)
    p = _mm(hb, w_ref[:, OFF_C:OFF_D]).reshape(bb, tt, RW_NPROJ * w)
    jvs, xv, bonus = _rwkv_mix(p, prev_ref[...], wrefs)
    prev_ref[...] = p[:, tt - SUBLANES:tt]
    shout_ref[...] = p[:, tt - 1:tt]
    if kv_t:
        for n, val in enumerate(jvs):
            jv_ref[n] = val.T
        rv_ref[...] = xv.T
    else:
        for n, val in enumerate(jvs):
            jv_ref[:, :, n * w:(n + 1) * w] = val.reshape(bb, tt, w)
        rv_ref[...] = xv.reshape(bb, tt, w)
    bonus_ref[...] = bonus.reshape(bb, tt, w)

    segs = [s for n, s in enumerate(_SEGS) if n != _SEG_PC]
    if kv_t:
        kvt = _nt(wkvt_ref[...], hb)
        seg_refs[_SEG_K][0] = kvt[:w]
        seg_refs[_SEG_V][0] = kvt[w:]
    for n, ((lo, hi), o_ref) in enumerate(zip(segs, seg_refs)):
        if not (kv_t and n in (_SEG_K, _SEG_V)):
            o_ref[...] = _mm(hb, w_ref[:, lo:hi]).reshape(bb, tt, hi - lo)


def _inproj(x, g, w_bf, wkvt_bf, sh0, rwkv_w, bb, tt, kv_t):
    b, t, _ = x.shape
    assert bb == 1 or not kv_t
    w = W_BRANCH
    wp = RW_NPROJ * w
    seq = lambda i, j: (i, j, 0)
    per_b = lambda i, j: (i, 0, 0)
    const = lambda i, j: (0, 0)
    out_specs, out_shape = [], []
    for n, (lo, hi) in enumerate(s for m, s in enumerate(_SEGS) if m != _SEG_PC):
        if kv_t and n in (_SEG_K, _SEG_V):
            out_specs.append(pl.BlockSpec((1, hi - lo, tt), lambda i, j: (i, 0, j)))
            out_shape.append(jax.ShapeDtypeStruct((b, hi - lo, t), F32))
        else:
            out_specs.append(pl.BlockSpec((bb, tt, hi - lo), seq))
            out_shape.append(jax.ShapeDtypeStruct((b, t, hi - lo), F32))
    if kv_t:
        out_specs += [pl.BlockSpec((RW_NJV, w, tt), lambda i, j: (0, i, j)),
                      pl.BlockSpec((w, tt), lambda i, j: (i, j))]
        out_shape += [jax.ShapeDtypeStruct((RW_NJV, b * w, t), F32), jax.ShapeDtypeStruct((b * w, t), F32)]
    else:
        out_specs += [pl.BlockSpec((bb, tt, wp), seq), pl.BlockSpec((bb, tt, w), seq)]
        out_shape += [jax.ShapeDtypeStruct((b, t, wp), F32), jax.ShapeDtypeStruct((b, t, w), F32)]
    out_specs.append(pl.BlockSpec((bb, tt, w), seq))
    out_shape.append(jax.ShapeDtypeStruct((b, t, w), F32))
    out_specs.append(pl.BlockSpec((bb, 1, wp), per_b))
    out_shape.append(jax.ShapeDtypeStruct((b, 1, wp), F32))
    return pl.pallas_call(
        functools.partial(_inproj_body, kv_t=kv_t),
        grid=(b // bb, t // tt),
        in_specs=[pl.BlockSpec((bb, tt, D_MODEL), seq),
                  pl.BlockSpec((1, D_MODEL), const),
                  pl.BlockSpec((D_MODEL, D_IN), const),
                  pl.BlockSpec((2 * w, D_MODEL), const),
                  pl.BlockSpec((bb, 1, wp), per_b)]
        + [pl.BlockSpec(r.shape, const) for r in rwkv_w],
        out_specs=out_specs,
        out_shape=out_shape,
        scratch_shapes=[pltpu.VMEM((bb, SUBLANES, wp), F32)],
        compiler_params=_cparams("parallel", "arbitrary"),
        name="inproj",
    )(x, g, w_bf, wkvt_bf, sh0, *rwkv_w)


def _attn_prompt_body(lam_ref, q_ref, k_ref, v_ref, o_ref, kb_ref, vb_ref, qm_ref, acc_ref, m_ref, l_ref, *, tq):
    qi = pl.program_id(1)
    nhc = 2 * A_HEADS
    lane = lax.broadcasted_iota(jnp.int32, (1, W_BRANCH), 1)

    @pl.when(qi == 0)
    def _():
        kb_ref[...] = k_ref[0].astype(BF16)
        vb_ref[...] = v_ref[0].astype(BF16)

    q = q_ref[0] * (A_QK ** -0.5 * math.log2(math.e))
    for hc in range(nhc):
        qm_ref[hc * tq:(hc + 1) * tq, :] = jnp.where(lane // A_QK == hc, q, 0.0).astype(BF16)
    def block(kb, width, diagonal):
        ks = pl.multiple_of(kb * tq, tq)
        s = _mm(qm_ref[...], kb_ref[:, pl.ds(ks, width)])
        if diagonal:
            r = lax.broadcasted_iota(jnp.int32, (nhc * tq, 1), 0) % tq
            c = lax.broadcasted_iota(jnp.int32, (1, tq), 1)
            s = jnp.where(c <= r, s, NEG_BIG)
            m_new = jnp.broadcast_to(jnp.max(s, axis=-1, keepdims=True), (nhc * tq, LANES))
        else:
            m_old = m_ref[...]
            m_new = jnp.maximum(m_old, jnp.max(s, axis=-1, keepdims=True))
            alpha = jnp.exp2(m_old - m_new)
            a_v = alpha[:, :A_V]
        p = jnp.exp2(s - jnp.concatenate([m_new] * (width // LANES), axis=1))
        l_new = jnp.sum(p, axis=-1, keepdims=True)
        l_ref[...] = jnp.broadcast_to(l_new, (nhc * tq, LANES)) if diagonal else alpha * l_ref[...] + l_new
        m_ref[...] = m_new
        pb = p.astype(BF16)
        for h in range(A_HEADS):
            rows = slice(2 * h * tq, (2 * h + 2) * tq)
            pv = _nt(pb[rows], vb_ref[h * A_V:(h + 1) * A_V, pl.ds(ks, width)])
            acc_ref[rows, :] = pv if diagonal else a_v[rows] * acc_ref[rows, :] + pv

    block(qi, tq, True)

    def off_diagonal_pair(kp, carry):
        block(2 * kp, 2 * tq, False)
        return carry

    lax.fori_loop(0, qi // 2, off_diagonal_pair, 0)

    @pl.when(qi % 2 == 1)
    def _():
        block(qi - 1, tq, False)

    lam = lam_ref[0]
    inv_l = (1.0 / l_ref[...])[:, :A_V]
    heads = []
    for h in range(A_HEADS):
        r0 = slice(2 * h * tq, (2 * h + 1) * tq)
        r1 = slice((2 * h + 1) * tq, (2 * h + 2) * tq)
        heads.append(acc_ref[r0, :] * inv_l[r0] - lam * (acc_ref[r1, :] * inv_l[r1]))
    o_ref[0] = jnp.concatenate(heads, axis=1)


def _attn_prompt(lam, q, kt, vt, tq):
    b, t, _ = q.shape
    nhc = 2 * A_HEADS
    return pl.pallas_call(
        functools.partial(_attn_prompt_body, tq=tq),
        grid=(b, t // tq),
        in_specs=[pl.BlockSpec(memory_space=pltpu.SMEM),
                  pl.BlockSpec((1, tq, W_BRANCH), lambda i, j: (i, j, 0)),
                  pl.BlockSpec((1, W_BRANCH, t), lambda i, j: (i, 0, 0)),
                  pl.BlockSpec((1, W_BRANCH, t), lambda i, j: (i, 0, 0))],
        out_specs=pl.BlockSpec((1, tq, W_BRANCH), lambda i, j: (i, j, 0)),
        out_shape=jax.ShapeDtypeStruct((b, t, W_BRANCH), F32),
        scratch_shapes=[pltpu.VMEM((W_BRANCH, t), BF16),
                        pltpu.VMEM((W_BRANCH, t), BF16),
                        pltpu.VMEM((nhc * tq, W_BRANCH), BF16),
                        pltpu.VMEM((nhc * tq, A_V), F32),
                        pltpu.VMEM((nhc * tq, LANES), F32),
                        pltpu.VMEM((nhc * tq, LANES), F32)],
        compiler_params=_cparams("parallel", "arbitrary"),
        name="attn_prompt",
    )(lam, q, kt, vt)


def _attn_sample_body(pt_ref, lam_ref, q_ref, kn_ref, vn_ref, *rest, n_pages, tq, nseq):
    del pt_ref
    kp_refs = rest[:nseq * n_pages]
    vp_refs = rest[nseq * n_pages:2 * nseq * n_pages]
    o_ref = rest[2 * nseq * n_pages]
    nhc = 2 * A_HEADS
    lane = lax.broadcasted_iota(jnp.int32, (1, W_BRANCH), 1)
    rowhc = lax.broadcasted_iota(jnp.int32, (nhc * tq, 1), 0) // tq
    tcol = lax.broadcasted_iota(jnp.int32, (nhc * tq, tq), 1)
    trow = lax.broadcasted_iota(jnp.int32, (nhc * tq, tq), 0) % tq
    lam = lam_ref[0]

    def diff(p):
        return jnp.concatenate(
            [p[(2 * h) * tq:(2 * h + 1) * tq] - lam * p[(2 * h + 1) * tq:(2 * h + 2) * tq]
             for h in range(A_HEADS)], axis=0)

    for n in range(nseq):
        kp = kp_refs[n * n_pages:(n + 1) * n_pages]
        vp = vp_refs[n * n_pages:(n + 1) * n_pages]
        q = q_ref[n] * (A_QK ** -0.5)
        qbd = jnp.where(lane // A_QK == rowhc, jnp.tile(q, (nhc, 1)), 0.0).astype(BF16)
        s_past = jnp.concatenate([_mm(qbd, kp[j][0, 0].astype(BF16)) for j in range(n_pages)], axis=1)
        s_new = _nt(qbd, kn_ref[n].astype(BF16))
        s_new = jnp.where(tcol <= trow, s_new, NEG_BIG)
        m = jnp.maximum(jnp.max(s_past, axis=-1, keepdims=True), jnp.max(s_new, axis=-1, keepdims=True))
        p_past = jnp.exp(s_past - m)
        p_new = jnp.exp(s_new - m)
        inv = 1.0 / (jnp.sum(p_past, axis=-1, keepdims=True) + jnp.sum(p_new, axis=-1, keepdims=True))
        w_past = diff(p_past * inv).astype(BF16)
        w_new = diff(p_new * inv).astype(BF16)
        out_all = _mm(w_new, vn_ref[n].astype(BF16))
        for j in range(n_pages):
            out_all = out_all + _nt(w_past[:, j * PAGE_SIZE:(j + 1) * PAGE_SIZE], vp[j][0, 0].astype(BF16))
        out = jnp.zeros((tq, W_BRANCH), F32)
        for h in range(A_HEADS):
            out = out + jnp.where(lane // A_V == h, out_all[h * tq:(h + 1) * tq], 0.0)
        o_ref[n] = out


def _attn_sample(lam, q, k, v, cache_kt, cache_vt, layer, page_table, nseq):
    b, tq, _ = q.shape
    n_pages = page_table.shape[1]
    pt = page_table.reshape(-1)
    seq = pl.BlockSpec((nseq, tq, W_BRANCH), lambda i, pt_r, lam_r: (i, 0, 0))

    def page_spec(n, j):
        return pl.BlockSpec((1, 1, W_BRANCH, PAGE_SIZE),
                            lambda i, pt_r, lam_r: (layer, pt_r[(i * nseq + n) * n_pages + j], 0, 0))

    pages = [page_spec(n, j) for n in range(nseq) for j in range(n_pages)]
    grid_spec = pltpu.PrefetchScalarGridSpec(
        num_scalar_prefetch=2,
        grid=(b // nseq,),
        in_specs=[seq, seq, seq] + pages * 2,
        out_specs=seq,
    )
    return pl.pallas_call(
        functools.partial(_attn_sample_body, n_pages=n_pages, tq=tq, nseq=nseq),
        grid_spec=grid_spec,
        out_shape=jax.ShapeDtypeStruct((b, tq, W_BRANCH), F32),
        compiler_params=_cparams("parallel"),
        name="attn_sample",
    )(pt, lam, q, k, v, *([cache_kt] * (nseq * n_pages)), *([cache_vt] * (nseq * n_pages)))


def _shift_in(prev8, x, s):
    tt = x.shape[1]
    ext = jnp.concatenate([prev8, x], axis=1)
    return pltpu.roll(ext, s, axis=1)[:, SUBLANES:SUBLANES + tt]


def _to_time_major(src_ref, dst_ref, row0, nb, tt):
    for b in range(nb):
        for c in range(dst_ref.shape[0]):
            dst_ref[c, pl.ds(row0 + b, tt, stride=nb), :] = src_ref[b, :, c * LANES:(c + 1) * LANES]


def _from_time_major(src_ref, dst_ref, nb, tt):
    for b in range(nb):
        for c in range(src_ref.shape[0]):
            dst_ref[b, :, c * LANES:(c + 1) * LANES] = src_ref[c, pl.ds(b, tt, stride=nb), :]


def _planes_get(ref, r0, r1):
    return jnp.concatenate([ref[c, r0:r1, :] for c in range(ref.shape[0])], axis=1)


def _planes_set(ref, r0, r1, val):
    for c in range(ref.shape[0]):
        ref[c, r0:r1, :] = val[:, c * LANES:(c + 1) * LANES]


def _mixb_body(xb_ref, cst_ref, h0_ref, cw_ref, cb_ref, wg_ref, bg_ref, lam_ref,
               ob_ref, cout_ref, hout_ref, xt_ref, ht_ref, hcar_ref, *, nb, tt):
    w = W_BRANCH
    rows = tt * nb
    hist = (CONV_W - 1) * nb

    @pl.when(pl.program_id(1) == 0)
    def _():
        _planes_set(xt_ref, 0, hist, cst_ref[...].reshape(hist, w))
        hcar_ref[...] = h0_ref[...]

    _to_time_major(xb_ref, xt_ref, hist, nb, tt)
    cw = cw_ref[...]
    xc = cb_ref[...] + _planes_get(xt_ref, 0, rows) * cw[0:1]
    for s in range(1, CONV_W):
        xc = xc + _planes_get(xt_ref, s * nb, s * nb + rows) * cw[s:s + 1]
    tail = _planes_get(xt_ref, rows, rows + hist)
    cout_ref[...] = tail.reshape(CONV_W - 1, nb, w)
    _planes_set(xt_ref, 0, hist, tail)

    gates = _mm(xc.astype(BF16), wg_ref[...]) + bg_ref[...]
    r = _sigmoid(gates[:, :w])
    i = _sigmoid(gates[:, w:])
    log_a = -LRU_C * r * _softplus(-lam_ref[...])
    a = jnp.exp(log_a)
    b = jnp.sqrt(1.0 - jnp.exp(2.0 * log_a)) * (i * xc)
    h = hcar_ref[...]
    for t in range(tt):
        sl = slice(t * nb, (t + 1) * nb)
        h = a[sl] * h + b[sl]
        _planes_set(ht_ref, t * nb, (t + 1) * nb, h)
    hcar_ref[...] = h
    hout_ref[...] = h
    _from_time_major(ht_ref, ob_ref, nb, tt)


def _mixb(xb, cst, h0, cw, cb, wg, bg, lru_lam, nb, tt):
    b, t, w = xb.shape
    const = lambda i, j: (0, 0)
    return pl.pallas_call(
        functools.partial(_mixb_body, nb=nb, tt=tt),
        grid=(b // nb, t // tt),
        in_specs=[pl.BlockSpec((nb, tt, w), lambda i, j: (i, j, 0)),
                  pl.BlockSpec((CONV_W - 1, nb, w), lambda i, j: (0, i, 0)),
                  pl.BlockSpec((nb, w), lambda i, j: (i, 0)),
                  pl.BlockSpec((CONV_W, w), const),
                  pl.BlockSpec((1, w), const),
                  pl.BlockSpec((w, 2 * w), const),
                  pl.BlockSpec((1, 2 * w), const),
                  pl.BlockSpec((1, w), const)],
        out_specs=[pl.BlockSpec((nb, tt, w), lambda i, j: (i, j, 0)),
                   pl.BlockSpec((CONV_W - 1, nb, w), lambda i, j: (0, i, 0)),
                   pl.BlockSpec((nb, w), lambda i, j: (i, 0))],
        out_shape=[jax.ShapeDtypeStruct((b, t, w), F32),
                   jax.ShapeDtypeStruct((CONV_W - 1, b, w), F32),
                   jax.ShapeDtypeStruct((b, w), F32)],
        scratch_shapes=[pltpu.VMEM((w // LANES, (tt + CONV_W - 1) * nb, LANES), F32),
                        pltpu.VMEM((w // LANES, tt * nb, LANES), F32),
                        pltpu.VMEM((nb, w), F32)],
        compiler_params=_cparams("parallel", "arbitrary"),
        name="conv_rglru",
    )(xb, cst, h0, cw, cb, wg, bg, lru_lam)


_JV_DECAY, _JV_NKK, _JV_KKA, _JV_K, _JV_R = range(5)
RW_NJV = 5
XT = LANES
RELAY_UNROLL = 32
Z_PAD_ROWS = SUBLANES


def _tree_sum(parts):
    while len(parts) > 1:
        parts = [parts[n] + parts[n + 1] for n in range(0, len(parts), 2)]
    return parts[0]


def _rwkv_steps(zrow, zv_ref, zy_ref, s_ref, nsteps, rows, n_acc=4):
    def accumulate(parts, j, term):
        parts[j % n_acc] = term if parts[j % n_acc] is None else parts[j % n_acc] + term

    sa_parts = [None] * n_acc
    for j in range(RW_HS):
        accumulate(sa_parts, j, s_ref[j] * zrow(_JV_NKK, j, 0))

    def step(t, sa):
        r0 = pl.multiple_of(t * rows, rows)
        v = zv_ref[pl.ds(r0, rows), :]
        t_next = jnp.minimum(t + 1, nsteps - 1)
        y_parts = [None] * n_acc
        sa_parts = [None] * n_acc
        for j in range(RW_HS):
            s = s_ref[j] * zrow(_JV_DECAY, j, t) + sa * zrow(_JV_KKA, j, t) + v * zrow(_JV_K, j, t)
            s_ref[j] = s
            accumulate(y_parts, j, s * zrow(_JV_R, j, t))
            accumulate(sa_parts, j, s * zrow(_JV_NKK, j, t_next))
        zy_ref[pl.ds(r0, rows), :] = _tree_sum(y_parts)
        return _tree_sum(sa_parts)

    lax.fori_loop(0, nsteps, step, _tree_sum(sa_parts))


def _rwkv_scan_fm_body(jv_ref, v_ref, s0_ref, y_ref, s_ref, z_ref, zv_ref, zy_ref, *, nb):
    reps = LANES // (nb * RW_HEADS)
    rows = RW_HS // reps
    nbh = nb * RW_HEADS

    @pl.when(pl.program_id(0) == 0)
    def _():
        s_ref[...] = s0_ref[...]

    for q in range(RW_NJV):
        def relay(jg, carry, q=q):
            for jj in range(RELAY_UNROLL):
                j = jg * RELAY_UNROLL + jj
                r = jv_ref[q, pl.ds(j, nbh, stride=RW_HS), :]
                z_ref[q, j, 0:XT, :] = jnp.concatenate([r] * reps, axis=0).T
            return carry

        lax.fori_loop(0, RW_HS // RELAY_UNROLL, relay, 0)
    for ih in range(rows):
        tile = jnp.concatenate([v_ref[pl.ds(rep * rows + ih, nbh, stride=RW_HS), :] for rep in range(reps)], axis=0)
        zv_ref[pl.ds(ih, XT, stride=rows), :] = tile.T

    _rwkv_steps(lambda which, j, t: z_ref[which, j, pl.ds(t, 1), :], zv_ref, zy_ref, s_ref, XT, rows)

    for ih in range(rows):
        tile = zy_ref[pl.ds(ih, XT, stride=rows), :].T
        for rep in range(reps):
            y_ref[pl.ds(rep * rows + ih, nbh, stride=RW_HS), :] = tile[rep * nbh:(rep + 1) * nbh]


def _rwkv_scan_fm(jv, v, s0):
    nq, nrow, t = jv.shape
    rows = s0.shape[1]
    st = lambda i: (0, 0, 0)
    return pl.pallas_call(
        functools.partial(_rwkv_scan_fm_body, nb=nrow // W_BRANCH),
        grid=(t // XT,),
        in_specs=[pl.BlockSpec((nq, nrow, XT), lambda i: (0, 0, i)),
                  pl.BlockSpec((nrow, XT), lambda i: (0, i)),
                  pl.BlockSpec((RW_HS, rows, LANES), st)],
        out_specs=[pl.BlockSpec((nrow, XT), lambda i: (0, i)),
                   pl.BlockSpec((RW_HS, rows, LANES), st)],
        out_shape=[jax.ShapeDtypeStruct((nrow, t), F32),
                   jax.ShapeDtypeStruct((RW_HS, rows, LANES), F32)],
        scratch_shapes=[pltpu.VMEM((RW_NJV, RW_HS, XT + Z_PAD_ROWS, LANES), F32),
                        pltpu.VMEM((XT * rows, LANES), F32),
                        pltpu.VMEM((XT * rows, LANES), F32)],
        compiler_params=_cparams("arbitrary"),
        name="rwkv_scan",
    )(jv, v, s0)


SEQLANES_ROWS = 4


def _rwkv_scan_seqlanes_body(z_ref, zv_ref, s0_ref, zy_ref, s_ref, *, nsteps):
    def value_rows(ig, carry):
        for ii in range(SEQLANES_ROWS):
            i = ig * SEQLANES_ROWS + ii
            s = s0_ref[0, 0, i]
            for t in range(nsteps):
                sa = jnp.sum(s * z_ref[t, _JV_NKK, 0], axis=0, keepdims=True)
                v = zv_ref[t, 0, pl.ds(i, 1), :]
                s = s * z_ref[t, _JV_DECAY, 0] + sa * z_ref[t, _JV_KKA, 0] + v * z_ref[t, _JV_K, 0]
                zy_ref[t, 0, pl.ds(i, 1), :] = jnp.sum(s * z_ref[t, _JV_R, 0], axis=0, keepdims=True)
            s_ref[0, i] = s
        return carry

    lax.fori_loop(0, RW_HS // SEQLANES_ROWS, value_rows, 0)


def _rwkv_scan_seqlanes(z, zv, s_all, layer):
    t, nq, nh, hs, _ = z.shape
    return pl.pallas_call(
        functools.partial(_rwkv_scan_seqlanes_body, nsteps=t),
        grid=(nh,),
        in_specs=[pl.BlockSpec((t, nq, 1, hs, LANES), lambda i: (0, 0, i, 0, 0)),
                  pl.BlockSpec((t, 1, hs, LANES), lambda i: (0, i, 0, 0)),
                  pl.BlockSpec((1, 1, hs, hs, LANES), lambda i: (layer, i, 0, 0, 0))],
        out_specs=[pl.BlockSpec((t, 1, hs, LANES), lambda i: (0, i, 0, 0)),
                   pl.BlockSpec((1, hs, hs, LANES), lambda i: (i, 0, 0, 0))],
        out_shape=[jax.ShapeDtypeStruct((t, nh, hs, LANES), F32),
                   jax.ShapeDtypeStruct((nh, hs, hs, LANES), F32)],
        compiler_params=_cparams("parallel"),
        name="rwkv_scan_seqlanes",
    )(z, zv, s_all)


def _lane_split(b):
    seqs = min(b, LANES // RW_HEADS)
    return b // seqs, seqs, LANES // (RW_HEADS * seqs)


def _state_to_lanes(s, g, bl, reps):
    rows = RW_HS // reps
    s = s.reshape(g, bl, RW_HEADS, reps, rows, RW_HS).transpose(0, 5, 4, 3, 1, 2)
    return s.reshape(g, RW_HS, rows, LANES)


def _state_from_lanes(s, bl, reps):
    g, _, rows, _ = s.shape
    s = s.reshape(g, RW_HS, rows, reps, bl, RW_HEADS).transpose(0, 4, 5, 3, 2, 1)
    return s.reshape(g * bl, RW_HEADS, RW_HS, RW_HS)


def _rwkv_scan(jv, v, wkv_s, feature_major):
    if feature_major:
        b = wkv_s.shape[0]
        g, bl, reps = _lane_split(b)
        assert g == 1 and jv.shape[-1] % XT == 0
        y, s = _rwkv_scan_fm(jv, v, _state_to_lanes(wkv_s, g, bl, reps)[0])
        return y, _state_from_lanes(s[None], bl, reps)
    s_all, layer = wkv_s
    b, t, _ = jv.shape
    assert b == LANES
    z = jv.reshape(b, t, RW_NJV, RW_HEADS, RW_HS).transpose(1, 2, 3, 4, 0)
    zv = v.reshape(b, t, RW_HEADS, RW_HS).transpose(1, 2, 3, 0)
    zy, s = _rwkv_scan_seqlanes(z, zv, s_all, layer)
    return zy.transpose(3, 0, 1, 2).reshape(b, t, W_BRANCH), s


S5_CHUNK_STEPS = 32


def _s5_body(u_ref, xre_ref, xim_ref, bmat_ref, cre_ref, cim_ref, ar_ref, ai_ref,
             d_ref, wglu_ref, bglu_ref, od_ref, hre_out, him_out,
             ut_ref, yt_ref, hre_s, him_s, car_re, car_im, *, nb, tt):
    @pl.when(pl.program_id(1) == 0)
    def _():
        car_re[...] = xre_ref[...]
        car_im[...] = xim_ref[...]

    _to_time_major(u_ref, ut_ref, 0, nb, tt)
    ar = jnp.broadcast_to(ar_ref[...], (nb, S5_W))
    ai = jnp.broadcast_to(ai_ref[...], (nb, S5_W))
    hr = car_re[...]
    hi = car_im[...]
    chunk = min(tt, S5_CHUNK_STEPS)
    for c0 in range(0, tt, chunk):
        bu = _mm(_planes_get(ut_ref, c0 * nb, (c0 + chunk) * nb).astype(BF16), bmat_ref[...])
        for s in range(chunk):
            src = slice(s * nb, (s + 1) * nb)
            hr, hi = (ar * hr - ai * hi + bu[src, :S5_W]), (ar * hi + ai * hr + bu[src, S5_W:])
            dst = slice((c0 + s) * nb, (c0 + s + 1) * nb)
            hre_s[dst, :] = hr
            him_s[dst, :] = hi
    car_re[...] = hr
    car_im[...] = hi
    hre_out[...] = hr
    him_out[...] = hi
    y = _mm(hre_s[...].astype(BF16), cre_ref[...]) - _mm(him_s[...].astype(BF16), cim_ref[...])
    y = y + d_ref[...] * _planes_get(ut_ref, 0, tt * nb)
    z = 0.5 * y * (1.0 + jnp.tanh(math.sqrt(2.0 / math.pi) * (y + 0.044715 * (y * y * y))))
    _planes_set(yt_ref, 0, tt * nb, z * _sigmoid(_mm(z.astype(BF16), wglu_ref[...]) + bglu_ref[...]))
    _from_time_major(yt_ref, od_ref, nb, tt)


def _s5(u, xre, xim, bmat, cre, cim, ar, ai, d, wglu, bglu, nb, tt):
    b, t, w = u.shape
    seq = lambda i, j: (i, j, 0)
    per_b = lambda i, j: (i, 0)
    c2 = lambda i, j: (0, 0)
    rows = tt * nb
    return pl.pallas_call(
        functools.partial(_s5_body, nb=nb, tt=tt),
        grid=(b // nb, t // tt),
        in_specs=[pl.BlockSpec((nb, tt, w), seq),
                  pl.BlockSpec((nb, S5_W), per_b),
                  pl.BlockSpec((nb, S5_W), per_b),
                  pl.BlockSpec(bmat.shape, c2),
                  pl.BlockSpec(cre.shape, c2),
                  pl.BlockSpec(cim.shape, c2),
                  pl.BlockSpec((1, S5_W), c2),
                  pl.BlockSpec((1, S5_W), c2),
                  pl.BlockSpec((1, w), c2),
                  pl.BlockSpec((w, w), c2),
                  pl.BlockSpec((1, w), c2)],
        out_specs=[pl.BlockSpec((nb, tt, w), seq),
                   pl.BlockSpec((nb, S5_W), per_b),
                   pl.BlockSpec((nb, S5_W), per_b)],
        out_shape=[jax.ShapeDtypeStruct((b, t, w), F32),
                   jax.ShapeDtypeStruct((b, S5_W), F32),
                   jax.ShapeDtypeStruct((b, S5_W), F32)],
        scratch_shapes=[pltpu.VMEM((w // LANES, rows, LANES), F32), pltpu.VMEM((w // LANES, rows, LANES), F32),
                        pltpu.VMEM((rows, S5_W), F32), pltpu.VMEM((rows, S5_W), F32),
                        pltpu.VMEM((nb, S5_W), F32), pltpu.VMEM((nb, S5_W), F32)],
        compiler_params=_cparams("parallel", "arbitrary"),
        name="s5",
    )(u, xre, xim, bmat, cre, cim, ar, ai, d, wglu, bglu)


def _s5_params(lre, lim, logdt, b_re, b_im, c_re, c_im):
    dt = jnp.exp(logdt)[:, None]
    mag = jnp.exp(lre * dt)
    ab_re, ab_im = mag * jnp.cos(lim * dt), mag * jnp.sin(lim * dt)
    den = lre * lre + lim * lim
    pr = ab_re - 1.0
    f_re = (pr * lre + ab_im * lim) / den
    f_im = (ab_im * lre - pr * lim) / den
    bb_re = f_re[..., None] * b_re - f_im[..., None] * b_im
    bb_im = f_re[..., None] * b_im + f_im[..., None] * b_re
    eye = jnp.eye(S5_GROUPS, dtype=F32)
    in_re = jnp.einsum('gnc,gh->gchn', bb_re, eye).reshape(W_BRANCH, S5_W)
    in_im = jnp.einsum('gnc,gh->gchn', bb_im, eye).reshape(W_BRANCH, S5_W)
    bmat = jnp.concatenate([in_re, in_im], axis=1).astype(BF16)
    cre = jnp.einsum('gcn,gh->gnhc', c_re, eye).reshape(S5_W, W_BRANCH).astype(BF16)
    cim = jnp.einsum('gcn,gh->gnhc', c_im, eye).reshape(S5_W, W_BRANCH).astype(BF16)
    return bmat, cre, cim, ab_re.reshape(1, S5_W), ab_im.reshape(1, S5_W)


def _outproj_body(x_ref, oa_ref, ob_ref, yc_ref, bonus_ref, od_ref, gate_ref,
                  subln_ref, gnw_ref, gnb_ref, ones_ref, wout_ref, npost_ref, y_ref, *, attn_scale, yc_t):
    w = W_BRANCH
    ones_blk = ones_ref[...]
    oa = oa_ref[...]
    oa = oa * lax.rsqrt(_segsum(oa * oa, ones_blk) * (1.0 / A_V) + NORM_EPS) * subln_ref[...] * attn_scale
    yc = yc_ref[...].T if yc_t else yc_ref[...]
    mean = _segsum(yc, ones_blk) * (1.0 / RW_HS)
    cen = yc - mean
    var = _segsum(cen * cen, ones_blk) * (1.0 / RW_HS)
    oc = cen * lax.rsqrt(var + RW_GN_EPS) * gnw_ref[...] + gnb_ref[...] + bonus_ref[...]
    gate = gate_ref[...]
    z = None
    for n, branch in enumerate((oa, ob_ref[...], oc, od_ref[...])):
        g = gate[:, n * w:(n + 1) * w]
        part = _mm((branch * (g * _sigmoid(g))).astype(BF16), wout_ref[n * w:(n + 1) * w, :])
        z = part if z is None else z + part
    zn = z * lax.rsqrt(jnp.mean(z * z, axis=-1, keepdims=True) + NORM_EPS) * npost_ref[...]
    y_ref[...] = x_ref[...] + zn


def _outproj(x2, oa, ob, yc, bonus, od, gate, subln, gnw, gnb, ones_blk, wout_bf, npost, attn_scale, tm, yc_t):
    n = x2.shape[0]
    w = W_BRANCH
    row = lambda i: (i, 0)
    const = lambda i: (0, 0)
    rw = pl.BlockSpec((tm, w), row)
    rd = pl.BlockSpec((tm, D_MODEL), row)
    vec = pl.BlockSpec((1, w), const)
    if yc_t:
        tiles_per_seq = yc.shape[1] // tm
        yc_spec = pl.BlockSpec((w, tm), lambda i: (i // tiles_per_seq, i % tiles_per_seq))
    else:
        yc_spec = rw
    return pl.pallas_call(
        functools.partial(_outproj_body, attn_scale=attn_scale, yc_t=yc_t),
        grid=(n // tm,),
        in_specs=[rd, rw, rw, yc_spec, rw, rw, rd, vec, vec, vec,
                  pl.BlockSpec((w, w), const),
                  pl.BlockSpec((D_MODEL, D_MODEL), const),
                  pl.BlockSpec((1, D_MODEL), const)],
        out_specs=rd,
        out_shape=jax.ShapeDtypeStruct((n, D_MODEL), F32),
        compiler_params=_cparams("parallel"),
        name="outproj",
    )(x2, oa, ob, yc, bonus, od, gate, subln, gnw, gnb, ones_blk, wout_bf, npost)


def _block_diag(blocks):
    n, k, _ = blocks.shape
    eye = jnp.eye(n, dtype=blocks.dtype)
    return jnp.einsum('nij,nm->nimj', blocks, eye).reshape(n * k, n * k)


def _prep_layer(wts, lam_init):
    (norm_pre, norm_post, w_in, w_out, lam_q1, lam_k1, lam_q2, lam_k2, subln_w, conv_w, conv_b,
     lru_wa, lru_ba, lru_wx, lru_bx, lru_lam,
     rw_mu, rw_w0, rw_w1, rw_w2, rw_a0, rw_a1, rw_a2, rw_kk, rw_ka, rw_rk, rw_gnw, rw_gnb,
     s5_lre, s5_lim, s5_logdt, s5_bre, s5_bim, s5_cre, s5_cim, s5_d, s5_wglu, s5_bglu) = wts
    w = W_BRANCH
    row = lambda v: v.reshape(1, -1)
    return dict(
        norm_pre=row(norm_pre), w_in=w_in.astype(BF16), w_out=w_out.astype(BF16), norm_post=row(norm_post),
        w_kvt=w_in[:, OFF_AK:OFF_B].T.astype(BF16),
        lam=(jnp.exp(jnp.sum(lam_q1 * lam_k1)) - jnp.exp(jnp.sum(lam_q2 * lam_k2)) + lam_init).reshape(1),
        attn_scale=1.0 - lam_init, subln=row(jnp.tile(subln_w, A_HEADS)),
        conv_w=conv_w, conv_b=row(conv_b),
        lru_w=jnp.concatenate([_block_diag(lru_wa), _block_diag(lru_wx)], axis=1).astype(BF16),
        lru_b=jnp.concatenate([lru_ba, lru_bx]).reshape(1, 2 * w), lru_lam=row(lru_lam),
        rwkv=(row(rw_mu), row(rw_w0), rw_w1.astype(BF16), rw_w2.astype(BF16), row(rw_a0),
              rw_a1.astype(BF16), rw_a2.astype(BF16), row(rw_kk), row(rw_ka), row(rw_rk)),
        gnw=row(rw_gnw), gnb=row(rw_gnb),
        ones_blk=_block_diag(jnp.ones((RW_HEADS, RW_HS, RW_HS), BF16)),
        s5=_s5_params(s5_lre, s5_lim, s5_logdt, s5_bre, s5_bim, s5_cre, s5_cim)
        + (row(s5_d), s5_wglu.astype(BF16), row(s5_bglu)),
    )


def _layer(x, attend, states, p, tiles):
    conv_buf, lru_h, shift_prev, wkv_s, ssm_x = states
    b, t, _ = x.shape
    n = b * t
    w = W_BRANCH
    tm, bb_proj, tt_proj, nb, tt_b, tt_d, kv_t = tiles

    q, k, v, xb, u, gate, jv, rv, bonus, new_shift = _inproj(
        x, p["norm_pre"], p["w_in"], p["w_kvt"], shift_prev.reshape(b, 1, RW_NPROJ * w),
        p["rwkv"] + (p["ones_blk"],), bb_proj, tt_proj, kv_t)

    oa = attend(p["lam"], q, k, v)
    if kv_t:
        new_k = k.reshape(b, A_HEADS, A_V, t).transpose(0, 3, 1, 2)
        new_v = v.reshape(b, A_HEADS, A_V, t).transpose(0, 3, 1, 2)
    else:
        new_k, new_v = k.reshape(b, t, A_HEADS, A_V), v.reshape(b, t, A_HEADS, A_V)

    ob, conv_t, new_h = _mixb(xb, conv_buf.transpose(1, 0, 2), lru_h, p["conv_w"], p["conv_b"],
                              p["lru_w"], p["lru_b"], p["lru_lam"], nb, tt_b)
    new_conv = conv_t.transpose(1, 0, 2)

    yc, new_s = _rwkv_scan(jv, rv, wkv_s, kv_t)

    od, hre, him = _s5(u, ssm_x[..., 0].reshape(b, S5_W), ssm_x[..., 1].reshape(b, S5_W),
                       *p["s5"], nb, tt_d)
    new_ssm = jnp.stack([hre.reshape(b, S5_GROUPS, S5_STATE), him.reshape(b, S5_GROUPS, S5_STATE)], axis=-1)

    y = _outproj(x.reshape(n, D_MODEL), oa.reshape(n, w), ob.reshape(n, w), yc if kv_t else yc.reshape(n, w),
                 bonus.reshape(n, w), od.reshape(n, w), gate.reshape(n, D_MODEL), p["subln"], p["gnw"], p["gnb"],
                 p["ones_blk"], p["w_out"], p["norm_post"], p["attn_scale"], tm, kv_t)
    new_states = (new_k, new_v, new_conv,
                  new_h.reshape(b, w), new_shift.reshape(b, RW_NPROJ * w), new_s, new_ssm)
    return y.reshape(b, t, D_MODEL), new_states


def kernel(x_prompt, x_sample, cache_k, cache_v, page_table, state_conv, state_lru, state_shift, state_wkv, state_ssm, norm_pre, norm_post, w_in, w_out, lam_q1, lam_k1, lam_q2, lam_k2, subln_w, conv_w, conv_b, lru_wa, lru_ba, lru_wx, lru_bx, lru_lam, rw_mu, rw_w0, rw_w1, rw_w2, rw_a0, rw_a1, rw_a2, rw_kk, rw_ka, rw_rk, rw_gnw, rw_gnb, s5_lre, s5_lim, s5_logdt, s5_bre, s5_bim, s5_cre, s5_cim, s5_d, s5_wglu, s5_bglu):
    weights = (norm_pre, norm_post, w_in, w_out, lam_q1, lam_k1, lam_q2, lam_k2, subln_w,
               conv_w, conv_b, lru_wa, lru_ba, lru_wx, lru_bx, lru_lam,
               rw_mu, rw_w0, rw_w1, rw_w2, rw_a0, rw_a1, rw_a2, rw_kk, rw_ka, rw_rk, rw_gnw, rw_gnb,
               s5_lre, s5_lim, s5_logdt, s5_bre, s5_bim, s5_cre, s5_cim, s5_d, s5_wglu, s5_bglu)
    depth = w_in.shape[0]
    nbp, tp, _ = x_prompt.shape
    nbs, ts, _ = x_sample.shape
    dt = x_prompt.dtype
    w = W_BRANCH
    zero_states = (jnp.zeros((nbp, CONV_W - 1, w), dt), jnp.zeros((nbp, w), dt),
                   jnp.zeros((nbp, RW_NPROJ * w), dt), jnp.zeros((nbp, RW_HEADS, RW_HS, RW_HS), dt),
                   jnp.zeros((nbp, S5_GROUPS, S5_STATE, 2), dt))
    nb = min(SUBLANES, nbp, nbs)
    tiles_p = (min(512, nbp * tp), 1, min(512, tp), nb, min(256, tp), min(128, tp), True)
    tiles_s = (min(512, nbs * ts), min(64, nbs), ts, nb, ts, ts, False)
    tq = min(256, tp)
    n_pool = cache_k.shape[1]
    cache_kt = cache_k.transpose(0, 1, 3, 4, 2).reshape(depth, n_pool, W_BRANCH, PAGE_SIZE)
    cache_vt = cache_v.transpose(0, 1, 3, 4, 2).reshape(depth, n_pool, W_BRANCH, PAGE_SIZE)
    wkv_seqlanes = state_wkv.transpose(0, 2, 3, 4, 1)
    xp, xs = x_prompt, x_sample
    outs_p = [[] for _ in range(7)]
    outs_s = [[] for _ in range(7)]
    for l in range(depth):
        prep = _prep_layer(tuple(wt[l] for wt in weights), 0.8 - 0.6 * math.exp(-0.3 * l))
        xp, st_p = _layer(xp, functools.partial(_attn_prompt, tq=tq), zero_states, prep, tiles_p)
        attend_s = functools.partial(_attn_sample, cache_kt=cache_kt, cache_vt=cache_vt, layer=l,
                                     page_table=page_table, nseq=min(2, nbs))
        st_in = (state_conv[l], state_lru[l], state_shift[l], (wkv_seqlanes, l), state_ssm[l])
        xs, st_s = _layer(xs, attend_s, st_in, prep, tiles_s)
        for n in range(7):
            outs_p[n].append(st_p[n])
            outs_s[n].append(st_s[n])
    k_p, v_p, conv_p, lru_p, shift_p, wkv_p, ssm_p = [jnp.stack(z) for z in outs_p]
    k_s, v_s, conv_s, lru_s, shift_s, wkv_s, ssm_s = [jnp.stack(z) for z in outs_s]
    wkv_s = wkv_s.transpose(0, 4, 1, 2, 3)
    return (xp, xs, k_p, k_s, v_p, v_s, conv_p, conv_s, lru_p, lru_s,
            shift_p, shift_s, wkv_p, wkv_s, ssm_p, ssm_s)
```

```python
import functools
import math

import jax
import jax.numpy as jnp
from jax import lax
from jax.experimental import pallas as pl
from jax.experimental.pallas import tpu as pltpu

F32 = jnp.float32
BF16 = jnp.bfloat16

D_MODEL = 1024
W_BRANCH = 256
A_HEADS = 4
A_QK = 32
A_V = 64
PAGE_SIZE = 128
LRU_BLOCKS = 4
LRU_BW = 64
CONV_W = 4
LRU_C = 8.0
RW_HEADS = 4
RW_HS = 64
RW_NPROJ = 5
RW_GN_EPS = RW_HS * 1e-5
S5_CH = 16
S5_GROUPS = 16
S5_STATE = 64
S5_W = S5_GROUPS * S5_STATE
NORM_EPS = 1e-6
NEG_BIG = -1e30

OFF_AQ = 0
OFF_AK = OFF_AQ + W_BRANCH
OFF_AV = OFF_AK + W_BRANCH
OFF_B = OFF_AV + W_BRANCH
OFF_C = OFF_B + W_BRANCH
OFF_D = OFF_C + RW_NPROJ * W_BRANCH
OFF_G = OFF_D + W_BRANCH
D_IN = OFF_G + D_MODEL
_SEGS = ((OFF_AQ, OFF_AK), (OFF_AK, OFF_AV), (OFF_AV, OFF_B), (OFF_B, OFF_C),
         (OFF_C, OFF_D), (OFF_D, OFF_G), (OFF_G, D_IN))

SUBLANES = 8
LANES = 128
VMEM_LIMIT_MIB = 56


def _cparams(*sem):
    return pltpu.CompilerParams(dimension_semantics=sem,
                                vmem_limit_bytes=VMEM_LIMIT_MIB * 1024 * 1024)


def _nt(a, b):
    return lax.dot_general(a, b, (((1,), (1,)), ((), ())), preferred_element_type=F32)


def _mm(a, b):
    return jnp.dot(a, b, preferred_element_type=F32)


def _segsum(x, ones_blk):
    hi = x.astype(BF16)
    lo = (x - hi.astype(F32)).astype(BF16)
    return _mm(hi, ones_blk) + _mm(lo, ones_blk)


def _sigmoid(x):
    return jax.nn.sigmoid(x)


def _softplus(x):
    return jnp.maximum(x, 0.0) + jnp.log1p(jnp.exp(-jnp.abs(x)))


_SEG_K, _SEG_V, _SEG_PC, _SEG_GATE = 1, 2, 4, 6
_PLAIN_SEGS = tuple(s for n, s in enumerate(_SEGS) if n not in (_SEG_PC, _SEG_GATE))
_N_RWKV_W = 11


def _rwkv_mix(p, prev8, wrefs):
    mu_ref, w0_ref, w1_ref, w2_ref, a0_ref, a1_ref, a2_ref, kkw_ref, ka_ref, rk_ref, ones_ref = wrefs
    bb, tt, _ = p.shape
    w = W_BRANCH
    p_prev = _shift_in(prev8, p, 1)
    xm = (p + (p_prev - p) * mu_ref[...]).reshape(bb * tt, RW_NPROJ * w)
    xr, xw, xk, xv, xa = (xm[:, n * w:(n + 1) * w] for n in range(RW_NPROJ))
    lw = _mm(jnp.tanh(_mm(xw.astype(BF16), w1_ref[...])).astype(BF16), w2_ref[...])
    wlog = -_softplus(-(w0_ref[...] + lw)) - 0.5
    decay = jnp.exp(-jnp.exp(wlog))
    a = _sigmoid(a0_ref[...] + _mm(_mm(xa.astype(BF16), a1_ref[...]).astype(BF16), a2_ref[...]))
    kk = xk * kkw_ref[...]
    ones_blk = ones_ref[...]
    kk = kk / jnp.maximum(jnp.sqrt(_segsum(kk * kk, ones_blk)), 1e-12)
    k = xk * (1.0 + (a - 1.0) * ka_ref[...])
    bonus = _segsum(xr * k * rk_ref[...], ones_blk) * xv
    return (decay, -kk, kk * a, k, xr), xv, bonus


def _inproj_body(x_ref, g_ref, w_ref, wkvt_ref, sh0_ref, *rest, kv_t):
    wrefs = rest[:_N_RWKV_W]
    seg_refs = rest[_N_RWKV_W:_N_RWKV_W + len(_PLAIN_SEGS)]
    jv_ref, rv_ref, bonus_ref, shout_ref, prev_ref = rest[_N_RWKV_W + len(_PLAIN_SEGS):]
    bb, tt, _ = x_ref.shape
    w = W_BRANCH

    @pl.when(pl.program_id(1) == 0)
    def _():
        prev_ref[...] = jnp.broadcast_to(sh0_ref[...], prev_ref.shape)

    x = x_ref[...].reshape(bb * tt, D_MODEL)
    h = x * lax.rsqrt(jnp.mean(x * x, axis=-1, keepdims=True) + NORM_EPS) * g_ref[...]
    hb = h.astype(BF16)
    p = _mm(hb, w_ref[:, OFF_C:OFF_D]).reshape(bb, tt, RW_NPROJ * w)
    jvs, xv, bonus = _rwkv_mix(p, prev_ref[...], wrefs)
    prev_ref[...] = p[:, tt - SUBLANES:tt]
    shout_ref[...] = p[:, tt - 1:tt]
    if kv_t:
        for n, val in enumerate(jvs):
            jv_ref[n] = val.T
        rv_ref[...] = xv.T
    else:
        for n, val in enumerate(jvs):
            jv_ref[:, :, n * w:(n + 1) * w] = val.reshape(bb, tt, w)
        rv_ref[...] = xv.reshape(bb, tt, w)
    bonus_ref[...] = bonus.reshape(bb, tt, w)

    segs = _PLAIN_SEGS
    if kv_t:
        kvt = _nt(wkvt_ref[...], hb)
        seg_refs[_SEG_K][0] = kvt[:w]
        seg_refs[_SEG_V][0] = kvt[w:]
    for n, ((lo, hi), o_ref) in enumerate(zip(segs, seg_refs)):
        if not (kv_t and n in (_SEG_K, _SEG_V)):
            o_ref[...] = _mm(hb, w_ref[:, lo:hi]).reshape(bb, tt, hi - lo)


def _inproj(x, g, w_bf, wkvt_bf, sh0, rwkv_w, bb, tt, kv_t):
    b, t, _ = x.shape
    assert bb == 1 or not kv_t
    w = W_BRANCH
    wp = RW_NPROJ * w
    seq = lambda i, j: (i, j, 0)
    per_b = lambda i, j: (i, 0, 0)
    const = lambda i, j: (0, 0)
    out_specs, out_shape = [], []
    for n, (lo, hi) in enumerate(_PLAIN_SEGS):
        if kv_t and n in (_SEG_K, _SEG_V):
            out_specs.append(pl.BlockSpec((1, hi - lo, tt), lambda i, j: (i, 0, j)))
            out_shape.append(jax.ShapeDtypeStruct((b, hi - lo, t), F32))
        else:
            out_specs.append(pl.BlockSpec((bb, tt, hi - lo), seq))
            out_shape.append(jax.ShapeDtypeStruct((b, t, hi - lo), F32))
    if kv_t:
        out_specs += [pl.BlockSpec((RW_NJV, w, tt), lambda i, j: (0, i, j)),
                      pl.BlockSpec((w, tt), lambda i, j: (i, j))]
        out_shape += [jax.ShapeDtypeStruct((RW_NJV, b * w, t), F32), jax.ShapeDtypeStruct((b * w, t), F32)]
    else:
        out_specs += [pl.BlockSpec((bb, tt, wp), seq), pl.BlockSpec((bb, tt, w), seq)]
        out_shape += [jax.ShapeDtypeStruct((b, t, wp), F32), jax.ShapeDtypeStruct((b, t, w), F32)]
    out_specs.append(pl.BlockSpec((bb, tt, w), seq))
    out_shape.append(jax.ShapeDtypeStruct((b, t, w), F32))
    out_specs.append(pl.BlockSpec((bb, 1, wp), per_b))
    out_shape.append(jax.ShapeDtypeStruct((b, 1, wp), F32))
    return pl.pallas_call(
        functools.partial(_inproj_body, kv_t=kv_t),
        grid=(b // bb, t // tt),
        in_specs=[pl.BlockSpec((bb, tt, D_MODEL), seq),
                  pl.BlockSpec((1, D_MODEL), const),
                  pl.BlockSpec((D_MODEL, OFF_G), const),
                  pl.BlockSpec((2 * w, D_MODEL), const),
                  pl.BlockSpec((bb, 1, wp), per_b)]
        + [pl.BlockSpec(r.shape, const) for r in rwkv_w],
        out_specs=out_specs,
        out_shape=out_shape,
        scratch_shapes=[pltpu.VMEM((bb, SUBLANES, wp), F32)],
        compiler_params=_cparams("parallel", "arbitrary"),
        name="inproj",
    )(x, g, w_bf, wkvt_bf, sh0, *rwkv_w)


def _attn_prompt_body(lam_ref, q_ref, k_ref, v_ref, o_ref, kb_ref, vb_ref, qm_ref, acc_ref, m_ref, l_ref, *, tq):
    qi = pl.program_id(1)
    nhc = 2 * A_HEADS
    lane = lax.broadcasted_iota(jnp.int32, (1, W_BRANCH), 1)

    @pl.when(qi == 0)
    def _():
        kb_ref[...] = k_ref[0].astype(BF16)
        vb_ref[...] = v_ref[0].astype(BF16)

    q = q_ref[0] * (A_QK ** -0.5 * math.log2(math.e))
    for hc in range(nhc):
        qm_ref[hc * tq:(hc + 1) * tq, :] = jnp.where(lane // A_QK == hc, q, 0.0).astype(BF16)
    def block(kb, width, diagonal):
        ks = pl.multiple_of(kb * tq, tq)
        s = _mm(qm_ref[...], kb_ref[:, pl.ds(ks, width)])
        if diagonal:
            r = lax.broadcasted_iota(jnp.int32, (nhc * tq, 1), 0) % tq
            c = lax.broadcasted_iota(jnp.int32, (1, tq), 1)
            s = jnp.where(c <= r, s, NEG_BIG)
            m_new = jnp.broadcast_to(jnp.max(s, axis=-1, keepdims=True), (nhc * tq, LANES))
        else:
            m_old = m_ref[...]
            m_new = jnp.maximum(m_old, jnp.max(s, axis=-1, keepdims=True))
            alpha = jnp.exp2(m_old - m_new)
            a_v = alpha[:, :A_V]
        p = jnp.exp2(s - jnp.concatenate([m_new] * (width // LANES), axis=1))
        l_new = jnp.sum(p, axis=-1, keepdims=True)
        l_ref[...] = jnp.broadcast_to(l_new, (nhc * tq, LANES)) if diagonal else alpha * l_ref[...] + l_new
        m_ref[...] = m_new
        pb = p.astype(BF16)
        for h in range(A_HEADS):
            rows = slice(2 * h * tq, (2 * h + 2) * tq)
            pv = _nt(pb[rows], vb_ref[h * A_V:(h + 1) * A_V, pl.ds(ks, width)])
            acc_ref[rows, :] = pv if diagonal else a_v[rows] * acc_ref[rows, :] + pv

    block(qi, tq, True)

    def off_diagonal_pair(kp, carry):
        block(2 * kp, 2 * tq, False)
        return carry

    lax.fori_loop(0, qi // 2, off_diagonal_pair, 0)

    @pl.when(qi % 2 == 1)
    def _():
        block(qi - 1, tq, False)

    lam = lam_ref[0]
    inv_l = (1.0 / l_ref[...])[:, :A_V]
    heads = []
    for h in range(A_HEADS):
        r0 = slice(2 * h * tq, (2 * h + 1) * tq)
        r1 = slice((2 * h + 1) * tq, (2 * h + 2) * tq)
        heads.append(acc_ref[r0, :] * inv_l[r0] - lam * (acc_ref[r1, :] * inv_l[r1]))
    o_ref[0] = jnp.concatenate(heads, axis=1)


def _attn_prompt(lam, q, kt, vt, tq):
    b, t, _ = q.shape
    nhc = 2 * A_HEADS
    return pl.pallas_call(
        functools.partial(_attn_prompt_body, tq=tq),
        grid=(b, t // tq),
        in_specs=[pl.BlockSpec(memory_space=pltpu.SMEM),
                  pl.BlockSpec((1, tq, W_BRANCH), lambda i, j: (i, j, 0)),
                  pl.BlockSpec((1, W_BRANCH, t), lambda i, j: (i, 0, 0)),
                  pl.BlockSpec((1, W_BRANCH, t), lambda i, j: (i, 0, 0))],
        out_specs=pl.BlockSpec((1, tq, W_BRANCH), lambda i, j: (i, j, 0)),
        out_shape=jax.ShapeDtypeStruct((b, t, W_BRANCH), F32),
        scratch_shapes=[pltpu.VMEM((W_BRANCH, t), BF16),
                        pltpu.VMEM((W_BRANCH, t), BF16),
                        pltpu.VMEM((nhc * tq, W_BRANCH), BF16),
                        pltpu.VMEM((nhc * tq, A_V), F32),
                        pltpu.VMEM((nhc * tq, LANES), F32),
                        pltpu.VMEM((nhc * tq, LANES), F32)],
        compiler_params=_cparams("parallel", "arbitrary"),
        name="attn_prompt",
    )(lam, q, kt, vt)


def _attn_sample_body(pt_ref, lam_ref, q_ref, kn_ref, vn_ref, *rest, n_pages, tq, nseq):
    del pt_ref
    kp_refs = rest[:nseq * n_pages]
    vp_refs = rest[nseq * n_pages:2 * nseq * n_pages]
    o_ref = rest[2 * nseq * n_pages]
    nhc = 2 * A_HEADS
    lane = lax.broadcasted_iota(jnp.int32, (1, W_BRANCH), 1)
    rowhc = lax.broadcasted_iota(jnp.int32, (nhc * tq, 1), 0) // tq
    tcol = lax.broadcasted_iota(jnp.int32, (nhc * tq, tq), 1)
    trow = lax.broadcasted_iota(jnp.int32, (nhc * tq, tq), 0) % tq
    lam = lam_ref[0]

    def diff(p):
        return jnp.concatenate(
            [p[(2 * h) * tq:(2 * h + 1) * tq] - lam * p[(2 * h + 1) * tq:(2 * h + 2) * tq]
             for h in range(A_HEADS)], axis=0)

    for n in range(nseq):
        kp = kp_refs[n * n_pages:(n + 1) * n_pages]
        vp = vp_refs[n * n_pages:(n + 1) * n_pages]
        q = q_ref[n] * (A_QK ** -0.5)
        qbd = jnp.where(lane // A_QK == rowhc, jnp.tile(q, (nhc, 1)), 0.0).astype(BF16)
        s_past = jnp.concatenate([_mm(qbd, kp[j][0, 0].astype(BF16)) for j in range(n_pages)], axis=1)
        s_new = _nt(qbd, kn_ref[n].astype(BF16))
        s_new = jnp.where(tcol <= trow, s_new, NEG_BIG)
        m = jnp.maximum(jnp.max(s_past, axis=-1, keepdims=True), jnp.max(s_new, axis=-1, keepdims=True))
        p_past = jnp.exp(s_past - m)
        p_new = jnp.exp(s_new - m)
        inv = 1.0 / (jnp.sum(p_past, axis=-1, keepdims=True) + jnp.sum(p_new, axis=-1, keepdims=True))
        w_past = diff(p_past * inv).astype(BF16)
        w_new = diff(p_new * inv).astype(BF16)
        out_all = _mm(w_new, vn_ref[n].astype(BF16))
        for j in range(n_pages):
            out_all = out_all + _nt(w_past[:, j * PAGE_SIZE:(j + 1) * PAGE_SIZE], vp[j][0, 0].astype(BF16))
        out = jnp.zeros((tq, W_BRANCH), F32)
        for h in range(A_HEADS):
            out = out + jnp.where(lane // A_V == h, out_all[h * tq:(h + 1) * tq], 0.0)
        o_ref[n] = out


def _attn_sample(lam, q, k, v, cache_kt, cache_vt, layer, page_table, nseq):
    b, tq, _ = q.shape
    n_pages = page_table.shape[1]
    pt = page_table.reshape(-1)
    seq = pl.BlockSpec((nseq, tq, W_BRANCH), lambda i, pt_r, lam_r: (i, 0, 0))

    def page_spec(n, j):
        return pl.BlockSpec((1, 1, W_BRANCH, PAGE_SIZE),
                            lambda i, pt_r, lam_r: (layer, pt_r[(i * nseq + n) * n_pages + j], 0, 0))

    pages = [page_spec(n, j) for n in range(nseq) for j in range(n_pages)]
    grid_spec = pltpu.PrefetchScalarGridSpec(
        num_scalar_prefetch=2,
        grid=(b // nseq,),
        in_specs=[seq, seq, seq] + pages * 2,
        out_specs=seq,
    )
    return pl.pallas_call(
        functools.partial(_attn_sample_body, n_pages=n_pages, tq=tq, nseq=nseq),
        grid_spec=grid_spec,
        out_shape=jax.ShapeDtypeStruct((b, tq, W_BRANCH), F32),
        compiler_params=_cparams("parallel"),
        name="attn_sample",
    )(pt, lam, q, k, v, *([cache_kt] * (nseq * n_pages)), *([cache_vt] * (nseq * n_pages)))


def _shift_in(prev8, x, s):
    tt = x.shape[1]
    ext = jnp.concatenate([prev8, x], axis=1)
    return pltpu.roll(ext, s, axis=1)[:, SUBLANES:SUBLANES + tt]


def _to_time_major(src_ref, dst_ref, row0, nb, tt):
    for b in range(nb):
        for c in range(dst_ref.shape[0]):
            dst_ref[c, pl.ds(row0 + b, tt, stride=nb), :] = src_ref[b, :, c * LANES:(c + 1) * LANES]


def _from_time_major(src_ref, dst_ref, nb, tt):
    for b in range(nb):
        for c in range(src_ref.shape[0]):
            dst_ref[b, :, c * LANES:(c + 1) * LANES] = src_ref[c, pl.ds(b, tt, stride=nb), :]


def _planes_get(ref, r0, r1):
    return jnp.concatenate([ref[c, r0:r1, :] for c in range(ref.shape[0])], axis=1)


def _planes_set(ref, r0, r1, val):
    for c in range(ref.shape[0]):
        ref[c, r0:r1, :] = val[:, c * LANES:(c + 1) * LANES]


def _mixb_body(xb_ref, cst_ref, h0_ref, cw_ref, cb_ref, wg_ref, bg_ref, lam_ref,
               ob_ref, cout_ref, hout_ref, xt_ref, ht_ref, hcar_ref, *, nb, tt):
    w = W_BRANCH
    rows = tt * nb
    hist = (CONV_W - 1) * nb

    @pl.when(pl.program_id(1) == 0)
    def _():
        _planes_set(xt_ref, 0, hist, cst_ref[...].reshape(hist, w))
        hcar_ref[...] = h0_ref[...]

    _to_time_major(xb_ref, xt_ref, hist, nb, tt)
    cw = cw_ref[...]
    xc = cb_ref[...] + _planes_get(xt_ref, 0, rows) * cw[0:1]
    for s in range(1, CONV_W):
        xc = xc + _planes_get(xt_ref, s * nb, s * nb + rows) * cw[s:s + 1]
    tail = _planes_get(xt_ref, rows, rows + hist)
    cout_ref[...] = tail.reshape(CONV_W - 1, nb, w)
    _planes_set(xt_ref, 0, hist, tail)

    gates = _mm(xc.astype(BF16), wg_ref[...]) + bg_ref[...]
    r = _sigmoid(gates[:, :w])
    i = _sigmoid(gates[:, w:])
    log_a = -LRU_C * r * _softplus(-lam_ref[...])
    a = jnp.exp(log_a)
    b = jnp.sqrt(1.0 - jnp.exp(2.0 * log_a)) * (i * xc)
    h = hcar_ref[...]
    for t in range(tt):
        sl = slice(t * nb, (t + 1) * nb)
        h = a[sl] * h + b[sl]
        _planes_set(ht_ref, t * nb, (t + 1) * nb, h)
    hcar_ref[...] = h
    hout_ref[...] = h
    _from_time_major(ht_ref, ob_ref, nb, tt)


def _mixb(xb, cst, h0, cw, cb, wg, bg, lru_lam, nb, tt):
    b, t, w = xb.shape
    const = lambda i, j: (0, 0)
    return pl.pallas_call(
        functools.partial(_mixb_body, nb=nb, tt=tt),
        grid=(b // nb, t // tt),
        in_specs=[pl.BlockSpec((nb, tt, w), lambda i, j: (i, j, 0)),
                  pl.BlockSpec((CONV_W - 1, nb, w), lambda i, j: (0, i, 0)),
                  pl.BlockSpec((nb, w), lambda i, j: (i, 0)),
                  pl.BlockSpec((CONV_W, w), const),
                  pl.BlockSpec((1, w), const),
                  pl.BlockSpec((w, 2 * w), const),
                  pl.BlockSpec((1, 2 * w), const),
                  pl.BlockSpec((1, w), const)],
        out_specs=[pl.BlockSpec((nb, tt, w), lambda i, j: (i, j, 0)),
                   pl.BlockSpec((CONV_W - 1, nb, w), lambda i, j: (0, i, 0)),
                   pl.BlockSpec((nb, w), lambda i, j: (i, 0))],
        out_shape=[jax.ShapeDtypeStruct((b, t, w), F32),
                   jax.ShapeDtypeStruct((CONV_W - 1, b, w), F32),
                   jax.ShapeDtypeStruct((b, w), F32)],
        scratch_shapes=[pltpu.VMEM((w // LANES, (tt + CONV_W - 1) * nb, LANES), F32),
                        pltpu.VMEM((w // LANES, tt * nb, LANES), F32),
                        pltpu.VMEM((nb, w), F32)],
        compiler_params=_cparams("parallel", "arbitrary"),
        name="conv_rglru",
    )(xb, cst, h0, cw, cb, wg, bg, lru_lam)


_JV_DECAY, _JV_NKK, _JV_KKA, _JV_K, _JV_R = range(5)
RW_NJV = 5
XT = LANES
RELAY_UNROLL = 32
Z_PAD_ROWS = SUBLANES


def _tree_sum(parts):
    while len(parts) > 1:
        parts = [parts[n] + parts[n + 1] for n in range(0, len(parts), 2)]
    return parts[0]


def _rwkv_steps(zrow, zv_ref, zy_ref, s_ref, nsteps, rows, n_acc=4):
    def accumulate(parts, j, term):
        parts[j % n_acc] = term if parts[j % n_acc] is None else parts[j % n_acc] + term

    sa_parts = [None] * n_acc
    for j in range(RW_HS):
        accumulate(sa_parts, j, s_ref[j] * zrow(_JV_NKK, j, 0))

    def step(t, sa):
        r0 = pl.multiple_of(t * rows, rows)
        v = zv_ref[pl.ds(r0, rows), :]
        t_next = jnp.minimum(t + 1, nsteps - 1)
        y_parts = [None] * n_acc
        sa_parts = [None] * n_acc
        for j in range(RW_HS):
            s = s_ref[j] * zrow(_JV_DECAY, j, t) + sa * zrow(_JV_KKA, j, t) + v * zrow(_JV_K, j, t)
            s_ref[j] = s
            accumulate(y_parts, j, s * zrow(_JV_R, j, t))
            accumulate(sa_parts, j, s * zrow(_JV_NKK, j, t_next))
        zy_ref[pl.ds(r0, rows), :] = _tree_sum(y_parts)
        return _tree_sum(sa_parts)

    lax.fori_loop(0, nsteps, step, _tree_sum(sa_parts))


def _rwkv_scan_fm_body(jv_ref, v_ref, s0_ref, y_ref, s_ref, z_ref, zv_ref, zy_ref, *, nb):
    reps = LANES // (nb * RW_HEADS)
    rows = RW_HS // reps
    nbh = nb * RW_HEADS

    @pl.when(pl.program_id(0) == 0)
    def _():
        s_ref[...] = s0_ref[...]

    for q in range(RW_NJV):
        def relay(jg, carry, q=q):
            for jj in range(RELAY_UNROLL):
                j = jg * RELAY_UNROLL + jj
                r = jv_ref[q, pl.ds(j, nbh, stride=RW_HS), :]
                z_ref[q, j, 0:XT, :] = jnp.concatenate([r] * reps, axis=0).T
            return carry

        lax.fori_loop(0, RW_HS // RELAY_UNROLL, relay, 0)
    for ih in range(rows):
        tile = jnp.concatenate([v_ref[pl.ds(rep * rows + ih, nbh, stride=RW_HS), :] for rep in range(reps)], axis=0)
        zv_ref[pl.ds(ih, XT, stride=rows), :] = tile.T

    _rwkv_steps(lambda which, j, t: z_ref[which, j, pl.ds(t, 1), :], zv_ref, zy_ref, s_ref, XT, rows)

    for ih in range(rows):
        tile = zy_ref[pl.ds(ih, XT, stride=rows), :].T
        for rep in range(reps):
            y_ref[pl.ds(rep * rows + ih, nbh, stride=RW_HS), :] = tile[rep * nbh:(rep + 1) * nbh]


def _rwkv_scan_fm(jv, v, s0):
    nq, nrow, t = jv.shape
    rows = s0.shape[1]
    st = lambda i: (0, 0, 0)
    return pl.pallas_call(
        functools.partial(_rwkv_scan_fm_body, nb=nrow // W_BRANCH),
        grid=(t // XT,),
        in_specs=[pl.BlockSpec((nq, nrow, XT), lambda i: (0, 0, i)),
                  pl.BlockSpec((nrow, XT), lambda i: (0, i)),
                  pl.BlockSpec((RW_HS, rows, LANES), st)],
        out_specs=[pl.BlockSpec((nrow, XT), lambda i: (0, i)),
                   pl.BlockSpec((RW_HS, rows, LANES), st)],
        out_shape=[jax.ShapeDtypeStruct((nrow, t), F32),
                   jax.ShapeDtypeStruct((RW_HS, rows, LANES), F32)],
        scratch_shapes=[pltpu.VMEM((RW_NJV, RW_HS, XT + Z_PAD_ROWS, LANES), F32),
                        pltpu.VMEM((XT * rows, LANES), F32),
                        pltpu.VMEM((XT * rows, LANES), F32)],
        compiler_params=_cparams("arbitrary"),
        name="rwkv_scan",
    )(jv, v, s0)


SEQLANES_ROWS = 4


def _rwkv_scan_seqlanes_body(z_ref, zv_ref, s0_ref, zy_ref, s_ref, *, nsteps):
    def value_rows(ig, carry):
        for ii in range(SEQLANES_ROWS):
            i = ig * SEQLANES_ROWS + ii
            s = s0_ref[0, 0, i]
            for t in range(nsteps):
                sa = jnp.sum(s * z_ref[t, _JV_NKK, 0], axis=0, keepdims=True)
                v = zv_ref[t, 0, pl.ds(i, 1), :]
                s = s * z_ref[t, _JV_DECAY, 0] + sa * z_ref[t, _JV_KKA, 0] + v * z_ref[t, _JV_K, 0]
                zy_ref[t, 0, pl.ds(i, 1), :] = jnp.sum(s * z_ref[t, _JV_R, 0], axis=0, keepdims=True)
            s_ref[0, i] = s
        return carry

    lax.fori_loop(0, RW_HS // SEQLANES_ROWS, value_rows, 0)


def _rwkv_scan_seqlanes(z, zv, s_all, layer):
    t, nq, nh, hs, _ = z.shape
    return pl.pallas_call(
        functools.partial(_rwkv_scan_seqlanes_body, nsteps=t),
        grid=(nh,),
        in_specs=[pl.BlockSpec((t, nq, 1, hs, LANES), lambda i: (0, 0, i, 0, 0)),
                  pl.BlockSpec((t, 1, hs, LANES), lambda i: (0, i, 0, 0)),
                  pl.BlockSpec((1, 1, hs, hs, LANES), lambda i: (layer, i, 0, 0, 0))],
        out_specs=[pl.BlockSpec((t, 1, hs, LANES), lambda i: (0, i, 0, 0)),
                   pl.BlockSpec((1, hs, hs, LANES), lambda i: (i, 0, 0, 0))],
        out_shape=[jax.ShapeDtypeStruct((t, nh, hs, LANES), F32),
                   jax.ShapeDtypeStruct((nh, hs, hs, LANES), F32)],
        compiler_params=_cparams("parallel"),
        name="rwkv_scan_seqlanes",
    )(z, zv, s_all)


def _lane_split(b):
    seqs = min(b, LANES // RW_HEADS)
    return b // seqs, seqs, LANES // (RW_HEADS * seqs)


def _state_to_lanes(s, g, bl, reps):
    rows = RW_HS // reps
    s = s.reshape(g, bl, RW_HEADS, reps, rows, RW_HS).transpose(0, 5, 4, 3, 1, 2)
    return s.reshape(g, RW_HS, rows, LANES)


def _state_from_lanes(s, bl, reps):
    g, _, rows, _ = s.shape
    s = s.reshape(g, RW_HS, rows, reps, bl, RW_HEADS).transpose(0, 4, 5, 3, 2, 1)
    return s.reshape(g * bl, RW_HEADS, RW_HS, RW_HS)


def _rwkv_scan(jv, v, wkv_s, feature_major):
    if feature_major:
        b = wkv_s.shape[0]
        g, bl, reps = _lane_split(b)
        assert g == 1 and jv.shape[-1] % XT == 0
        y, s = _rwkv_scan_fm(jv, v, _state_to_lanes(wkv_s, g, bl, reps)[0])
        return y, _state_from_lanes(s[None], bl, reps)
    s_all, layer = wkv_s
    b, t, _ = jv.shape
    assert b == LANES
    z = jv.reshape(b, t, RW_NJV, RW_HEADS, RW_HS).transpose(1, 2, 3, 4, 0)
    zv = v.reshape(b, t, RW_HEADS, RW_HS).transpose(1, 2, 3, 0)
    zy, s = _rwkv_scan_seqlanes(z, zv, s_all, layer)
    return zy.transpose(3, 0, 1, 2).reshape(b, t, W_BRANCH), s


S5_CHUNK_STEPS = 32


def _s5_body(u_ref, xre_ref, xim_ref, bmat_ref, cre_ref, cim_ref, ar_ref, ai_ref,
             d_ref, wglu_ref, bglu_ref, od_ref, hre_out, him_out,
             ut_ref, yt_ref, hre_s, him_s, car_re, car_im, *, nb, tt):
    @pl.when(pl.program_id(1) == 0)
    def _():
        car_re[...] = xre_ref[...]
        car_im[...] = xim_ref[...]

    _to_time_major(u_ref, ut_ref, 0, nb, tt)
    ar = jnp.broadcast_to(ar_ref[...], (nb, S5_W))
    ai = jnp.broadcast_to(ai_ref[...], (nb, S5_W))
    hr = car_re[...]
    hi = car_im[...]
    chunk = min(tt, S5_CHUNK_STEPS)
    for c0 in range(0, tt, chunk):
        bu = _mm(_planes_get(ut_ref, c0 * nb, (c0 + chunk) * nb).astype(BF16), bmat_ref[...])
        for s in range(chunk):
            src = slice(s * nb, (s + 1) * nb)
            hr, hi = (ar * hr - ai * hi + bu[src, :S5_W]), (ar * hi + ai * hr + bu[src, S5_W:])
            dst = slice((c0 + s) * nb, (c0 + s + 1) * nb)
            hre_s[dst, :] = hr
            him_s[dst, :] = hi
    car_re[...] = hr
    car_im[...] = hi
    hre_out[...] = hr
    him_out[...] = hi
    y = _mm(hre_s[...].astype(BF16), cre_ref[...]) - _mm(him_s[...].astype(BF16), cim_ref[...])
    y = y + d_ref[...] * _planes_get(ut_ref, 0, tt * nb)
    z = 0.5 * y * (1.0 + jnp.tanh(math.sqrt(2.0 / math.pi) * (y + 0.044715 * (y * y * y))))
    _planes_set(yt_ref, 0, tt * nb, z * _sigmoid(_mm(z.astype(BF16), wglu_ref[...]) + bglu_ref[...]))
    _from_time_major(yt_ref, od_ref, nb, tt)


def _s5(u, xre, xim, bmat, cre, cim, ar, ai, d, wglu, bglu, nb, tt):
    b, t, w = u.shape
    seq = lambda i, j: (i, j, 0)
    per_b = lambda i, j: (i, 0)
    c2 = lambda i, j: (0, 0)
    rows = tt * nb
    return pl.pallas_call(
        functools.partial(_s5_body, nb=nb, tt=tt),
        grid=(b // nb, t // tt),
        in_specs=[pl.BlockSpec((nb, tt, w), seq),
                  pl.BlockSpec((nb, S5_W), per_b),
                  pl.BlockSpec((nb, S5_W), per_b),
                  pl.BlockSpec(bmat.shape, c2),
                  pl.BlockSpec(cre.shape, c2),
                  pl.BlockSpec(cim.shape, c2),
                  pl.BlockSpec((1, S5_W), c2),
                  pl.BlockSpec((1, S5_W), c2),
                  pl.BlockSpec((1, w), c2),
                  pl.BlockSpec((w, w), c2),
                  pl.BlockSpec((1, w), c2)],
        out_specs=[pl.BlockSpec((nb, tt, w), seq),
                   pl.BlockSpec((nb, S5_W), per_b),
                   pl.BlockSpec((nb, S5_W), per_b)],
        out_shape=[jax.ShapeDtypeStruct((b, t, w), F32),
                   jax.ShapeDtypeStruct((b, S5_W), F32),
                   jax.ShapeDtypeStruct((b, S5_W), F32)],
        scratch_shapes=[pltpu.VMEM((w // LANES, rows, LANES), F32), pltpu.VMEM((w // LANES, rows, LANES), F32),
                        pltpu.VMEM((rows, S5_W), F32), pltpu.VMEM((rows, S5_W), F32),
                        pltpu.VMEM((nb, S5_W), F32), pltpu.VMEM((nb, S5_W), F32)],
        compiler_params=_cparams("parallel", "arbitrary"),
        name="s5",
    )(u, xre, xim, bmat, cre, cim, ar, ai, d, wglu, bglu)


def _s5_params(lre, lim, logdt, b_re, b_im, c_re, c_im):
    dt = jnp.exp(logdt)[:, None]
    mag = jnp.exp(lre * dt)
    ab_re, ab_im = mag * jnp.cos(lim * dt), mag * jnp.sin(lim * dt)
    den = lre * lre + lim * lim
    pr = ab_re - 1.0
    f_re = (pr * lre + ab_im * lim) / den
    f_im = (ab_im * lre - pr * lim) / den
    bb_re = f_re[..., None] * b_re - f_im[..., None] * b_im
    bb_im = f_re[..., None] * b_im + f_im[..., None] * b_re
    eye = jnp.eye(S5_GROUPS, dtype=F32)
    in_re = jnp.einsum('gnc,gh->gchn', bb_re, eye).reshape(W_BRANCH, S5_W)
    in_im = jnp.einsum('gnc,gh->gchn', bb_im, eye).reshape(W_BRANCH, S5_W)
    bmat = jnp.concatenate([in_re, in_im], axis=1).astype(BF16)
    cre = jnp.einsum('gcn,gh->gnhc', c_re, eye).reshape(S5_W, W_BRANCH).astype(BF16)
    cim = jnp.einsum('gcn,gh->gnhc', c_im, eye).reshape(S5_W, W_BRANCH).astype(BF16)
    return bmat, cre, cim, ab_re.reshape(1, S5_W), ab_im.reshape(1, S5_W)


def _outproj_body(x_ref, oa_ref, ob_ref, yc_ref, bonus_ref, od_ref, npre_ref, wgate_ref,
                  subln_ref, gnw_ref, gnb_ref, ones_ref, wout_ref, npost_ref, y_ref, *, attn_scale, yc_t):
    w = W_BRANCH
    x = x_ref[...]
    h = x * lax.rsqrt(jnp.mean(x * x, axis=-1, keepdims=True) + NORM_EPS) * npre_ref[...]
    gate = _mm(h.astype(BF16), wgate_ref[...])
    ones_blk = ones_ref[...]
    oa = oa_ref[...]
    oa = oa * lax.rsqrt(_segsum(oa * oa, ones_blk) * (1.0 / A_V) + NORM_EPS) * subln_ref[...] * attn_scale
    yc = yc_ref[...].T if yc_t else yc_ref[...]
    mean = _segsum(yc, ones_blk) * (1.0 / RW_HS)
    cen = yc - mean
    var = _segsum(cen * cen, ones_blk) * (1.0 / RW_HS)
    oc = cen * lax.rsqrt(var + RW_GN_EPS) * gnw_ref[...] + gnb_ref[...] + bonus_ref[...]
    z = None
    for n, branch in enumerate((oa, ob_ref[...], oc, od_ref[...])):
        g = gate[:, n * w:(n + 1) * w]
        part = _mm((branch * (g * _sigmoid(g))).astype(BF16), wout_ref[n * w:(n + 1) * w, :])
        z = part if z is None else z + part
    zn = z * lax.rsqrt(jnp.mean(z * z, axis=-1, keepdims=True) + NORM_EPS) * npost_ref[...]
    y_ref[...] = x + zn


def _outproj(x2, oa, ob, yc, bonus, od, npre, wgate_bf, subln, gnw, gnb, ones_blk, wout_bf, npost,
             attn_scale, tm, yc_t):
    n = x2.shape[0]
    w = W_BRANCH
    row = lambda i: (i, 0)
    const = lambda i: (0, 0)
    rw = pl.BlockSpec((tm, w), row)
    rd = pl.BlockSpec((tm, D_MODEL), row)
    vec = pl.BlockSpec((1, w), const)
    if yc_t:
        tiles_per_seq = yc.shape[1] // tm
        yc_spec = pl.BlockSpec((w, tm), lambda i: (i // tiles_per_seq, i % tiles_per_seq))
    else:
        yc_spec = rw
    return pl.pallas_call(
        functools.partial(_outproj_body, attn_scale=attn_scale, yc_t=yc_t),
        grid=(n // tm,),
        in_specs=[rd, rw, rw, yc_spec, rw, rw,
                  pl.BlockSpec((1, D_MODEL), const),
                  pl.BlockSpec((D_MODEL, D_MODEL), const),
                  vec, vec, vec,
                  pl.BlockSpec((w, w), const),
                  pl.BlockSpec((D_MODEL, D_MODEL), const),
                  pl.BlockSpec((1, D_MODEL), const)],
        out_specs=rd,
        out_shape=jax.ShapeDtypeStruct((n, D_MODEL), F32),
        compiler_params=_cparams("parallel"),
        name="outproj",
    )(x2, oa, ob, yc, bonus, od, npre, wgate_bf, subln, gnw, gnb, ones_blk, wout_bf, npost)


def _block_diag(blocks):
    n, k, _ = blocks.shape
    eye = jnp.eye(n, dtype=blocks.dtype)
    return jnp.einsum('nij,nm->nimj', blocks, eye).reshape(n * k, n * k)


def _prep_layer(wts, lam_init):
    (norm_pre, norm_post, w_in, w_out, lam_q1, lam_k1, lam_q2, lam_k2, subln_w, conv_w, conv_b,
     lru_wa, lru_ba, lru_wx, lru_bx, lru_lam,
     rw_mu, rw_w0, rw_w1, rw_w2, rw_a0, rw_a1, rw_a2, rw_kk, rw_ka, rw_rk, rw_gnw, rw_gnb,
     s5_lre, s5_lim, s5_logdt, s5_bre, s5_bim, s5_cre, s5_cim, s5_d, s5_wglu, s5_bglu) = wts
    w = W_BRANCH
    row = lambda v: v.reshape(1, -1)
    return dict(
        norm_pre=row(norm_pre), w_in=w_in[:, :OFF_G].astype(BF16), w_gate=w_in[:, OFF_G:].astype(BF16),
        w_out=w_out.astype(BF16), norm_post=row(norm_post),
        w_kvt=w_in[:, OFF_AK:OFF_B].T.astype(BF16),
        lam=(jnp.exp(jnp.sum(lam_q1 * lam_k1)) - jnp.exp(jnp.sum(lam_q2 * lam_k2)) + lam_init).reshape(1),
        attn_scale=1.0 - lam_init, subln=row(jnp.tile(subln_w, A_HEADS)),
        conv_w=conv_w, conv_b=row(conv_b),
        lru_w=jnp.concatenate([_block_diag(lru_wa), _block_diag(lru_wx)], axis=1).astype(BF16),
        lru_b=jnp.concatenate([lru_ba, lru_bx]).reshape(1, 2 * w), lru_lam=row(lru_lam),
        rwkv=(row(rw_mu), row(rw_w0), rw_w1.astype(BF16), rw_w2.astype(BF16), row(rw_a0),
              rw_a1.astype(BF16), rw_a2.astype(BF16), row(rw_kk), row(rw_ka), row(rw_rk)),
        gnw=row(rw_gnw), gnb=row(rw_gnb),
        ones_blk=_block_diag(jnp.ones((RW_HEADS, RW_HS, RW_HS), BF16)),
        s5=_s5_params(s5_lre, s5_lim, s5_logdt, s5_bre, s5_bim, s5_cre, s5_cim)
        + (row(s5_d), s5_wglu.astype(BF16), row(s5_bglu)),
    )


def _layer(x, attend, states, p, tiles):
    conv_buf, lru_h, shift_prev, wkv_s, ssm_x = states
    b, t, _ = x.shape
    n = b * t
    w = W_BRANCH
    tm, bb_proj, tt_proj, nb, tt_b, tt_d, kv_t = tiles

    q, k, v, xb, u, jv, rv, bonus, new_shift = _inproj(
        x, p["norm_pre"], p["w_in"], p["w_kvt"], shift_prev.reshape(b, 1, RW_NPROJ * w),
        p["rwkv"] + (p["ones_blk"],), bb_proj, tt_proj, kv_t)

    oa = attend(p["lam"], q, k, v)
    if kv_t:
        new_k = k.reshape(b, A_HEADS, A_V, t).transpose(0, 3, 1, 2)
        new_v = v.reshape(b, A_HEADS, A_V, t).transpose(0, 3, 1, 2)
    else:
        new_k, new_v = k.reshape(b, t, A_HEADS, A_V), v.reshape(b, t, A_HEADS, A_V)

    ob, conv_t, new_h = _mixb(xb, conv_buf.transpose(1, 0, 2), lru_h, p["conv_w"], p["conv_b"],
                              p["lru_w"], p["lru_b"], p["lru_lam"], nb, tt_b)
    new_conv = conv_t.transpose(1, 0, 2)

    yc, new_s = _rwkv_scan(jv, rv, wkv_s, kv_t)

    od, hre, him = _s5(u, ssm_x[..., 0].reshape(b, S5_W), ssm_x[..., 1].reshape(b, S5_W),
                       *p["s5"], nb, tt_d)
    new_ssm = jnp.stack([hre.reshape(b, S5_GROUPS, S5_STATE), him.reshape(b, S5_GROUPS, S5_STATE)], axis=-1)

    y = _outproj(x.reshape(n, D_MODEL), oa.reshape(n, w), ob.reshape(n, w), yc if kv_t else yc.reshape(n, w),
                 bonus.reshape(n, w), od.reshape(n, w), p["norm_pre"], p["w_gate"], p["subln"], p["gnw"], p["gnb"],
                 p["ones_blk"], p["w_out"], p["norm_post"], p["attn_scale"], tm, kv_t)
    new_states = (new_k, new_v, new_conv,
                  new_h.reshape(b, w), new_shift.reshape(b, RW_NPROJ * w), new_s, new_ssm)
    return y.reshape(b, t, D_MODEL), new_states


def kernel(x_prompt, x_sample, cache_k, cache_v, page_table, state_conv, state_lru, state_shift, state_wkv, state_ssm, norm_pre, norm_post, w_in, w_out, lam_q1, lam_k1, lam_q2, lam_k2, subln_w, conv_w, conv_b, lru_wa, lru_ba, lru_wx, lru_bx, lru_lam, rw_mu, rw_w0, rw_w1, rw_w2, rw_a0, rw_a1, rw_a2, rw_kk, rw_ka, rw_rk, rw_gnw, rw_gnb, s5_lre, s5_lim, s5_logdt, s5_bre, s5_bim, s5_cre, s5_cim, s5_d, s5_wglu, s5_bglu):
    weights = (norm_pre, norm_post, w_in, w_out, lam_q1, lam_k1, lam_q2, lam_k2, subln_w,
               conv_w, conv_b, lru_wa, lru_ba, lru_wx, lru_bx, lru_lam,
               rw_mu, rw_w0, rw_w1, rw_w2, rw_a0, rw_a1, rw_a2, rw_kk, rw_ka, rw_rk, rw_gnw, rw_gnb,
               s5_lre, s5_lim, s5_logdt, s5_bre, s5_bim, s5_cre, s5_cim, s5_d, s5_wglu, s5_bglu)
    depth = w_in.shape[0]
    nbp, tp, _ = x_prompt.shape
    nbs, ts, _ = x_sample.shape
    dt = x_prompt.dtype
    w = W_BRANCH
    zero_states = (jnp.zeros((nbp, CONV_W - 1, w), dt), jnp.zeros((nbp, w), dt),
                   jnp.zeros((nbp, RW_NPROJ * w), dt), jnp.zeros((nbp, RW_HEADS, RW_HS, RW_HS), dt),
                   jnp.zeros((nbp, S5_GROUPS, S5_STATE, 2), dt))
    nb = min(SUBLANES, nbp, nbs)
    tiles_p = (min(512, nbp * tp), 1, min(512, tp), nb, min(256, tp), min(128, tp), True)
    tiles_s = (min(512, nbs * ts), min(64, nbs), ts, nb, ts, ts, False)
    tq = min(256, tp)
    n_pool = cache_k.shape[1]
    cache_kt = cache_k.transpose(0, 1, 3, 4, 2).reshape(depth, n_pool, W_BRANCH, PAGE_SIZE)
    cache_vt = cache_v.transpose(0, 1, 3, 4, 2).reshape(depth, n_pool, W_BRANCH, PAGE_SIZE)
    wkv_seqlanes = state_wkv.transpose(0, 2, 3, 4, 1)
    xp, xs = x_prompt, x_sample
    outs_p = [[] for _ in range(7)]
    outs_s = [[] for _ in range(7)]
    for l in range(depth):
        prep = _prep_layer(tuple(wt[l] for wt in weights), 0.8 - 0.6 * math.exp(-0.3 * l))
        xp, st_p = _layer(xp, functools.partial(_attn_prompt, tq=tq), zero_states, prep, tiles_p)
        attend_s = functools.partial(_attn_sample, cache_kt=cache_kt, cache_vt=cache_vt, layer=l,
                                     page_table=page_table, nseq=min(2, nbs))
        st_in = (state_conv[l], state_lru[l], state_shift[l], (wkv_seqlanes, l), state_ssm[l])
        xs, st_s = _layer(xs, attend_s, st_in, prep, tiles_s)
        for n in range(7):
            outs_p[n].append(st_p[n])
            outs_s[n].append(st_s[n])
    k_p, v_p, conv_p, lru_p, shift_p, wkv_p, ssm_p = [jnp.stack(z) for z in outs_p]
    k_s, v_s, conv_s, lru_s, shift_s, wkv_s, ssm_s = [jnp.stack(z) for z in outs_s]
    wkv_s = wkv_s.transpose(0, 4, 1, 2, 3)
    return (xp, xs, k_p, k_s, v_p, v_s, conv_p, conv_s, lru_p, lru_s,
            shift_p, shift_s, wkv_p, wkv_s, ssm_p, ssm_s)
```
